```python
import math
import functools
import jax
import jax.numpy as jnp
from jax import lax
import numpy as np

D_MODEL = 2048
BATCH = 2
SEQ = 4096
DEPTH = 1
DEC_BATCH = 128
DEC_SEQ = 1
PAST_LEN = 2048
PAGE_SIZE = 128

HEAD_DIM = 128
ATTN_HEADS = 8
ATTN_WIDTH = ATTN_HEADS * HEAD_DIM
DILATIONS = ((128, 1), (512, 4), (2048, 16))
WINDOW_MAX = 2048
QUERY_BLOCK = 128
ATTN_SCALE = HEAD_DIM ** -0.5
ROPE_THETA = 500000.0
ROT_DIM = HEAD_DIM // 4
GLA_HEADS = 4
GLA_WIDTH = D_MODEL - ATTN_WIDTH
GLA_DV = GLA_WIDTH // GLA_HEADS
GLA_DK = GLA_DV // 2
GLA_LOWRANK = 16
GLA_TAU = 16.0
GLA_CHUNK = 64
PEER_HEADS = 8
PEER_NKEYS = 128
PEER_EXPERTS = PEER_NKEYS * PEER_NKEYS
PEER_QDIM = 256
PEER_HALF = PEER_QDIM // 2
PEER_TOPK = 16
PEER_BLOCK = 128
NORM_EPS = 1e-6
IN_SIZES = (ATTN_WIDTH, ATTN_WIDTH, ATTN_WIDTH,
            GLA_HEADS * GLA_DK, GLA_HEADS * GLA_DK,
            GLA_WIDTH, GLA_WIDTH, GLA_LOWRANK)

kernel_name = "hymba_dilated_gla_peer_step"

F32 = jnp.float32


def _split_points():
    pts, acc = [], 0
    for s in IN_SIZES[:-1]:
        acc += s
        pts.append(acc)
    return pts


def rms_norm(x, g):
    xf = x.astype(F32)
    y = xf * lax.rsqrt(jnp.mean(xf * xf, axis=-1, keepdims=True) + NORM_EPS)
    return (y * g.astype(F32)).astype(x.dtype)


def partial_rotary(x, pos):
    half = ROT_DIM // 2
    inv = jnp.exp(-math.log(ROPE_THETA) * jnp.arange(half, dtype=F32) * (2.0 / ROT_DIM))
    ang = pos.astype(F32)[:, None] * inv[None, :]
    cos = jnp.cos(ang)[None, :, None, :]
    sin = jnp.sin(ang)[None, :, None, :]
    xf = x.astype(F32)
    x1, x2 = xf[..., :half], xf[..., half:ROT_DIM]
    out = jnp.concatenate([x1 * cos - x2 * sin, x2 * cos + x1 * sin, xf[..., ROT_DIM:]], axis=-1)
    return out.astype(x.dtype)


def dilation_offsets(window, dilation):
    return jnp.arange(window // dilation + 1) * dilation


def dilated_branch(q, k_g, v_g, valid):
    s = jnp.einsum('bqhd,bqmhd->bqhm', q, k_g).astype(F32) * ATTN_SCALE
    s = jnp.where(valid[None, :, None, :], s, -jnp.inf)
    m = jnp.max(s, axis=-1, keepdims=True)
    e = jnp.exp(s - m)
    den = jnp.sum(e, axis=-1)
    o = jnp.einsum('bqhm,bqmhd->bqhd', e, v_g.astype(F32)) / den[..., None]
    return m[..., 0] + jnp.log(den), o


def combine_by_denominator(lses, outs):
    wts = jax.nn.softmax(jnp.stack(lses), axis=0)
    return jnp.einsum('rbqh,rbqhd->bqhd', wts, jnp.stack(outs))


def dilated_attention_prompt(q, k, v):
    B, S, H, Dh = q.shape
    n_blocks = S // QUERY_BLOCK

    def block(i):
        t = i * QUERY_BLOCK + jnp.arange(QUERY_BLOCK)
        qb = lax.dynamic_slice_in_dim(q, i * QUERY_BLOCK, QUERY_BLOCK, axis=1)
        lses, outs = [], []
        for w, d in DILATIONS:
            idx = t[:, None] - dilation_offsets(w, d)[None, :]
            valid = idx >= 0
            idx = jnp.clip(idx, 0, S - 1)
            lse, o = dilated_branch(qb, jnp.take(k, idx, axis=1), jnp.take(v, idx, axis=1), valid)
            lses.append(lse)
            outs.append(o)
        return combine_by_denominator(lses, outs)

    out = lax.map(block, jnp.arange(n_blocks))
    return out.transpose(1, 0, 2, 3, 4).reshape(B, S, H, Dh)


def dilated_attention_sample(q, k_new, v_new, k_buf, v_buf):
    Q = q.shape[1]
    wbuf = k_buf.shape[1]
    t = wbuf + jnp.arange(Q)
    lses, outs = [], []
    for w, d in DILATIONS:
        idx = t[:, None] - dilation_offsets(w, d)[None, :]
        in_new = (idx >= wbuf)[None, :, :, None, None]
        i_buf = jnp.clip(idx, 0, wbuf - 1)
        i_new = jnp.clip(idx - wbuf, 0, Q - 1)
        k_g = jnp.where(in_new, jnp.take(k_new, i_new, axis=1), jnp.take(k_buf, i_buf, axis=1))
        v_g = jnp.where(in_new, jnp.take(v_new, i_new, axis=1), jnp.take(v_buf, i_buf, axis=1))
        lse, o = dilated_branch(q, k_g, v_g, idx >= 0)
        lses.append(lse)
        outs.append(o)
    return combine_by_denominator(lses, outs)


def gla_chunked(q, k, v, log_a, s0):
    B, L, H, DK = q.shape
    DV = v.shape[-1]
    C = math.gcd(L, GLA_CHUNK)
    n = L // C

    def to_chunks(t):
        return t.astype(F32).reshape(B, n, C, H, t.shape[-1]).transpose(1, 0, 3, 2, 4)

    qc, kc, vc, ac = to_chunks(q), to_chunks(k), to_chunks(v), to_chunks(log_a)
    causal = jnp.tril(jnp.ones((C, C), dtype=bool))[:, :, None]

    def step(S, inp):
        qi, ki, vi, ai = inp
        b = jnp.cumsum(ai, axis=2)
        diff = b[:, :, :, None, :] - b[:, :, None, :, :]
        decay = jnp.exp(jnp.where(causal, diff, -jnp.inf))
        A = jnp.einsum('bhtk,bhsk,bhtsk->bhts', qi, ki, decay)
        o = jnp.einsum('bhtk,bhkv->bhtv', qi * jnp.exp(b), S) + jnp.einsum('bhts,bhsv->bhtv', A, vi)
        b_end = b[:, :, -1:, :]
        S = jnp.exp(b_end[:, :, 0, :])[..., None] * S + jnp.einsum(
            'bhsk,bhsv->bhkv', ki * jnp.exp(b_end - b), vi)
        return S, o

    S, o = lax.scan(step, s0.astype(F32), (qc, kc, vc, ac))
    o = o.transpose(1, 0, 3, 2, 4).reshape(B, L, H, DV)
    return o, S


def peer_ffn(xn, peer_query, peer_keys_1, peer_keys_2, expert_u, expert_v):
    shp = xn.shape
    x2 = xn.reshape(-1, shp[-1])
    n = x2.shape[0]
    blk = min(PEER_BLOCK, n)
    n_pad = -(-n // blk) * blk
    x2 = jnp.pad(x2, ((0, n_pad - n), (0, 0)))

    def block(xb):
        T = xb.shape[0]
        q = (xb @ peer_query).reshape(T, PEER_HEADS, 2, PEER_HALF)
        s1 = jnp.einsum('thc,hkc->thk', q[:, :, 0], peer_keys_1).astype(F32)
        s2 = jnp.einsum('thc,hkc->thk', q[:, :, 1], peer_keys_2).astype(F32)
        v1, i1 = lax.top_k(s1, PEER_TOPK)
        v2, i2 = lax.top_k(s2, PEER_TOPK)
        cand = (v1[..., :, None] + v2[..., None, :]).reshape(T, PEER_HEADS, PEER_TOPK * PEER_TOPK)
        cand_idx = (i1[..., :, None] * PEER_NKEYS + i2[..., None, :]).reshape(T, PEER_HEADS, PEER_TOPK * PEER_TOPK)
        best, sel = lax.top_k(cand, PEER_TOPK)
        idx = jnp.take_along_axis(cand_idx, sel, axis=-1)
        gate = jax.nn.softmax(best, axis=-1)
        u = jnp.take(expert_u, idx, axis=0)
        act = jax.nn.gelu(jnp.einsum('td,thed->the', xb, u).astype(F32), approximate=False) * gate
        v = jnp.take(expert_v, idx, axis=0)
        return jnp.einsum('the,thed->td', act.astype(v.dtype), v)

    y = lax.map(block, x2.reshape(n_pad // blk, blk, shp[-1])).reshape(n_pad, shp[-1])[:n]
    return y.reshape(shp).astype(xn.dtype)


def decoder_layer(h, pos, attend, gla_s0, norm_attn, w_in, w_gate_up, b_gate, gla_norm, w_o,
                  norm_ffn, peer_query, peer_keys_1, peer_keys_2, expert_u, expert_v):
    B, L, _ = h.shape
    xn = rms_norm(h, norm_attn)
    z = xn @ w_in
    qa, ka, va, qg, kg, vg, gate_o, gate_lr = jnp.split(z, _split_points(), axis=-1)
    qa = partial_rotary(qa.reshape(B, L, ATTN_HEADS, HEAD_DIM), pos)
    ka = partial_rotary(ka.reshape(B, L, ATTN_HEADS, HEAD_DIM), pos)
    va = va.reshape(B, L, ATTN_HEADS, HEAD_DIM)
    o_attn = attend(qa, ka, va)
    log_a = (jax.nn.log_sigmoid((gate_lr @ w_gate_up).astype(F32) + b_gate.astype(F32))
             / GLA_TAU).reshape(B, L, GLA_HEADS, GLA_DK)
    o_gla, gla_state = gla_chunked(qg.reshape(B, L, GLA_HEADS, GLA_DK) * (GLA_DK ** -0.5),
                                   kg.reshape(B, L, GLA_HEADS, GLA_DK),
                                   vg.reshape(B, L, GLA_HEADS, GLA_DV), log_a, gla_s0)
    o_gla = o_gla * lax.rsqrt(jnp.mean(o_gla * o_gla, axis=-1, keepdims=True) + NORM_EPS) * gla_norm.astype(F32)
    o_gla = o_gla.reshape(B, L, GLA_WIDTH) * jax.nn.silu(gate_o.astype(F32))
    mixed = jnp.concatenate([o_attn.reshape(B, L, ATTN_WIDTH), o_gla], axis=-1).astype(h.dtype)
    h = h + mixed @ w_o
    h = h + peer_ffn(rms_norm(h, norm_ffn), peer_query, peer_keys_1, peer_keys_2, expert_u, expert_v)
    return h, ka, va, gla_state


def setup_inputs(seed: int = 0) -> dict:
    key = jax.random.key(seed)
    ks = jax.random.split(key, 18)
    wbuf = min(WINDOW_MAX, PAST_LEN)
    in_cols = sum(IN_SIZES)

    def nrm(k, shape, scale):
        return jax.random.normal(k, shape, F32) * scale

    return {
        "x_prompt": nrm(ks[0], (BATCH, SEQ, D_MODEL), 1.0),
        "x_sample": nrm(ks[1], (DEC_BATCH, DEC_SEQ, D_MODEL), 1.0),
        "cache_attn_k": nrm(ks[2], (DEPTH, DEC_BATCH, wbuf, ATTN_HEADS, HEAD_DIM), 1.0),
        "cache_attn_v": nrm(ks[3], (DEPTH, DEC_BATCH, wbuf, ATTN_HEADS, HEAD_DIM), 1.0),
        "state_gla": nrm(ks[4], (DEPTH, DEC_BATCH, GLA_HEADS, GLA_DK, GLA_DV), 1.0),
        "norm_attn": 1.0 + nrm(ks[5], (DEPTH, D_MODEL), 0.02),
        "w_in": nrm(ks[6], (DEPTH, D_MODEL, in_cols), D_MODEL ** -0.5),
        "w_gate_up": nrm(ks[7], (DEPTH, GLA_LOWRANK, GLA_HEADS * GLA_DK), GLA_LOWRANK ** -0.5),
        "b_gate": nrm(ks[8], (DEPTH, GLA_HEADS * GLA_DK), 0.1),
        "gla_norm": 1.0 + nrm(ks[9], (DEPTH, GLA_DV), 0.02),
        "w_o": nrm(ks[10], (DEPTH, D_MODEL, D_MODEL), D_MODEL ** -0.5),
        "norm_ffn": 1.0 + nrm(ks[11], (DEPTH, D_MODEL), 0.02),
        "peer_query": nrm(ks[12], (DEPTH, D_MODEL, PEER_HEADS * PEER_QDIM), D_MODEL ** -0.5),
        "peer_keys_1": nrm(ks[13], (DEPTH, PEER_HEADS, PEER_NKEYS, PEER_HALF), PEER_HALF ** -0.5),
        "peer_keys_2": nrm(ks[14], (DEPTH, PEER_HEADS, PEER_NKEYS, PEER_HALF), PEER_HALF ** -0.5),
        "expert_u": nrm(ks[15], (DEPTH, PEER_EXPERTS, D_MODEL), D_MODEL ** -0.5),
        "expert_v": nrm(ks[16], (DEPTH, PEER_EXPERTS, D_MODEL), (PEER_HEADS * PEER_TOPK) ** -0.5),
        "final_norm": 1.0 + nrm(ks[17], (D_MODEL,), 0.02),
    }


def reference(x_prompt, x_sample, cache_attn_k, cache_attn_v, state_gla, norm_attn, w_in, w_gate_up,
              b_gate, gla_norm, w_o, norm_ffn, peer_query, peer_keys_1, peer_keys_2, expert_u,
              expert_v, final_norm):
    Bp, Lp, _ = x_prompt.shape
    Bs, Ls, _ = x_sample.shape
    pos_p = jnp.arange(Lp)
    pos_s = PAST_LEN + jnp.arange(Ls)
    win_p = min(WINDOW_MAX, Lp)
    hp, hs = x_prompt, x_sample
    kp_l, vp_l, sp_l, ks_l, vs_l, ss_l = [], [], [], [], [], []
    for l in range(DEPTH):
        weights = (norm_attn[l], w_in[l], w_gate_up[l], b_gate[l], gla_norm[l], w_o[l], norm_ffn[l],
                   peer_query[l], peer_keys_1[l], peer_keys_2[l], expert_u[l], expert_v[l])
        gla0 = jnp.zeros((Bp, GLA_HEADS, GLA_DK, GLA_DV), F32)
        hp, kp, vp, sp = decoder_layer(hp, pos_p, dilated_attention_prompt, gla0, *weights)
        kp_l.append(kp[:, Lp - win_p:])
        vp_l.append(vp[:, Lp - win_p:])
        sp_l.append(sp)
        attend_s = functools.partial(dilated_attention_sample, k_buf=cache_attn_k[l], v_buf=cache_attn_v[l])
        hs, ks_new, vs_new, ss = decoder_layer(hs, pos_s, attend_s, state_gla[l], *weights)
        ks_l.append(ks_new)
        vs_l.append(vs_new)
        ss_l.append(ss)
    y_prompt = rms_norm(hp, final_norm)
    y_sample = rms_norm(hs, final_norm)
    new_k_prompt = jnp.stack(kp_l)
    new_v_prompt = jnp.stack(vp_l)
    new_gla_prompt = jnp.stack(sp_l)
    new_k_sample = jnp.stack(ks_l)
    new_v_sample = jnp.stack(vs_l)
    new_gla_sample = jnp.stack(ss_l)
    return (y_prompt, y_sample, new_k_prompt, new_v_prompt, new_gla_prompt, new_k_sample, new_v_sample, new_gla_sample)
```

```python
import functools
import math

import jax
import jax.numpy as jnp
import numpy as np
from jax import lax
from jax.experimental import pallas as pl
from jax.experimental.pallas import tpu as pltpu

F32 = jnp.float32
BF16 = jnp.bfloat16

D_MODEL = 2048
PAST_LEN = 2048
HEAD_DIM = 128
ATTN_HEADS = 8
ATTN_WIDTH = ATTN_HEADS * HEAD_DIM
DILATIONS = ((128, 1), (512, 4), (2048, 16))
WINDOW_MAX = 2048
QUERY_BLOCK = 128
ATTN_SCALE = HEAD_DIM ** -0.5
ROPE_THETA = 500000.0
ROT_DIM = HEAD_DIM // 4
GLA_HEADS = 4
GLA_WIDTH = D_MODEL - ATTN_WIDTH
GLA_DV = GLA_WIDTH // GLA_HEADS
GLA_DK = GLA_DV // 2
GLA_LOWRANK = 16
GLA_TAU = 16.0
GLA_CHUNK = 64
PEER_HEADS = 8
PEER_NKEYS = 128
PEER_EXPERTS = PEER_NKEYS * PEER_NKEYS
PEER_QDIM = 256
PEER_HALF = PEER_QDIM // 2
PEER_TOPK = 16
NORM_EPS = 1e-6
MAIN_COLS = 3 * ATTN_WIDTH + 2 * GLA_HEADS * GLA_DK + 2 * GLA_WIDTH

LANES = 128
VMEM_LIMIT = 56 * 1024 * 1024


def _cparams(sem):
    return pltpu.CompilerParams(dimension_semantics=sem, vmem_limit_bytes=VMEM_LIMIT)


def _rms_matmul_body(x_ref, g_ref, w_ref, o_ref, xn_out_ref, xn_ref):
    @pl.when(pl.program_id(1) == 0)
    def _():
        x = x_ref[...]
        ms = jnp.mean(x * x, axis=-1, keepdims=True)
        xn = (x * lax.rsqrt(ms + NORM_EPS) * g_ref[...]).astype(BF16)
        xn_ref[...] = xn
        xn_out_ref[...] = xn

    o_ref[...] = jnp.dot(xn_ref[...], w_ref[...], preferred_element_type=F32)


def rms_matmul(x, g, w, tm, tn):
    T, D = x.shape
    N = w.shape[1]
    assert T % tm == 0 and N % tn == 0
    return pl.pallas_call(
        _rms_matmul_body,
        grid=(T // tm, N // tn),
        in_specs=[
            pl.BlockSpec((tm, D), lambda i, j: (i, 0)),
            pl.BlockSpec((1, D), lambda i, j: (0, 0)),
            pl.BlockSpec((D, tn), lambda i, j: (0, j)),
        ],
        out_specs=[
            pl.BlockSpec((tm, tn), lambda i, j: (i, j)),
            pl.BlockSpec((tm, D), lambda i, j: (i, 0)),
        ],
        out_shape=[
            jax.ShapeDtypeStruct((T, N), F32),
            jax.ShapeDtypeStruct((T, D), BF16),
        ],
        scratch_shapes=[pltpu.VMEM((tm, D), BF16)],
        compiler_params=_cparams(("parallel", "arbitrary")),
        name="rms_matmul",
    )(x, g.reshape(1, D).astype(F32), w)


def _proj_residual_body(a_ref, b_ref, wa_ref, wb_ref, res_ref, o_ref):
    acc = jnp.dot(a_ref[...], wa_ref[...], preferred_element_type=F32)
    acc += jnp.dot(b_ref[...], wb_ref[...], preferred_element_type=F32)
    o_ref[...] = res_ref[...] + acc


def proj_residual(a, b, wa, wb, res, tm, tn):
    T, Ka = a.shape
    Kb = b.shape[1]
    N = wa.shape[1]
    assert T % tm == 0 and N % tn == 0
    return pl.pallas_call(
        _proj_residual_body,
        grid=(T // tm, N // tn),
        in_specs=[
            pl.BlockSpec((tm, Ka), lambda i, j: (i, 0)),
            pl.BlockSpec((tm, Kb), lambda i, j: (i, 0)),
            pl.BlockSpec((Ka, tn), lambda i, j: (0, j)),
            pl.BlockSpec((Kb, tn), lambda i, j: (0, j)),
            pl.BlockSpec((tm, tn), lambda i, j: (i, j)),
        ],
        out_specs=pl.BlockSpec((tm, tn), lambda i, j: (i, j)),
        out_shape=jax.ShapeDtypeStruct((T, N), F32),
        compiler_params=_cparams(("parallel", "arbitrary")),
        name="proj_residual",
    )(a, b, wa, wb, res)


def _peer_dense_body(xT_ref, u_ref, vT_ref, s1_ref, a1_ref, s2_ref, e2_ref, tau_ref,
                     yT_ref, act_ref, *, te, tm):
    e = pl.program_id(1)

    @pl.when(e == 0)
    def _():
        yT_ref[...] = jnp.zeros_like(yT_ref)

    hT = jnp.dot(u_ref[...], xT_ref[...], preferred_element_type=F32)
    act_ref[...] = 0.5 * hT * (1.0 + lax.erf(hT * (2.0 ** -0.5)))

    def per_i1(c, carry):
        rows = pl.ds(pl.multiple_of(c * PEER_NKEYS, PEER_NKEYS), PEER_NKEYS)
        for tc in range(tm // LANES):
            lanes = pl.ds(tc * LANES, LANES)
            gate = jnp.zeros((PEER_NKEYS, LANES), F32)
            for h in range(PEER_HEADS):
                s1row = s1_ref[h, c, :, lanes]
                a1row = a1_ref[h, c, :, lanes]
                sel = (s2_ref[h, :, lanes] + s1row) >= tau_ref[pl.ds(h, 1), lanes]
                gate += jnp.where(sel, e2_ref[h, :, lanes] * a1row, 0.0)
            act_ref[rows, lanes] = act_ref[rows, lanes] * gate
        return carry

    lax.fori_loop(0, te // PEER_NKEYS, per_i1, 0)

    yT_ref[...] += jnp.dot(vT_ref[...], act_ref[...].astype(BF16),
                           preferred_element_type=F32)


def peer_dense(xT, u, vT, s1, a1, s2, e2, tau, tm, te):
    D, T = xT.shape
    E = u.shape[0]
    assert T % tm == 0 and E % te == 0 and te % PEER_NKEYS == 0 and tm % LANES == 0
    n_i1 = te // PEER_NKEYS
    row_spec = pl.BlockSpec((PEER_HEADS, n_i1, 1, tm), lambda i, e: (0, e, 0, i))
    tile_spec = pl.BlockSpec((PEER_HEADS, PEER_NKEYS, tm), lambda i, e: (0, 0, i))
    s1 = s1.reshape(PEER_HEADS, PEER_NKEYS, 1, T)
    a1 = a1.reshape(PEER_HEADS, PEER_NKEYS, 1, T)
    return pl.pallas_call(
        functools.partial(_peer_dense_body, te=te, tm=tm),
        grid=(T // tm, E // te),
        in_specs=[
            pl.BlockSpec((D, tm), lambda i, e: (0, i)),
            pl.BlockSpec((te, D), lambda i, e: (e, 0)),
            pl.BlockSpec((D, te), lambda i, e: (0, e)),
            row_spec, row_spec, tile_spec, tile_spec,
            pl.BlockSpec((PEER_HEADS, tm), lambda i, e: (0, i)),
        ],
        out_specs=pl.BlockSpec((D, tm), lambda i, e: (0, i)),
        out_shape=jax.ShapeDtypeStruct((D, T), F32),
        scratch_shapes=[pltpu.VMEM((te, tm), F32)],
        compiler_params=_cparams(("parallel", "arbitrary")),
        name="peer_dense",
    )(xT, u, vT, s1, a1, s2, e2, tau)


def _final_norm_body(h_ref, y_ref, g_ref, o_ref):
    x = h_ref[...] + y_ref[...]
    ms = jnp.mean(x * x, axis=-1, keepdims=True)
    o_ref[...] = x * lax.rsqrt(ms + NORM_EPS) * g_ref[...]


def final_norm(h, y, g, tm):
    T, D = h.shape
    return pl.pallas_call(
        _final_norm_body,
        grid=(T // tm,),
        in_specs=[
            pl.BlockSpec((tm, D), lambda i: (i, 0)),
            pl.BlockSpec((tm, D), lambda i: (i, 0)),
            pl.BlockSpec((1, D), lambda i: (0, 0)),
        ],
        out_specs=pl.BlockSpec((tm, D), lambda i: (i, 0)),
        out_shape=jax.ShapeDtypeStruct((T, D), F32),
        compiler_params=_cparams(("parallel",)),
        name="final_norm",
    )(h, y, g.reshape(1, D).astype(F32))


def _partial_rotary(x, pos):
    half = ROT_DIM // 2
    inv = jnp.exp(-math.log(ROPE_THETA) * jnp.arange(half, dtype=F32) * (2.0 / ROT_DIM))
    ang = pos.astype(F32)[:, None] * inv[None, :]
    cos = jnp.cos(ang)[None, :, None, :]
    sin = jnp.sin(ang)[None, :, None, :]
    x1, x2 = x[..., :half], x[..., half:ROT_DIM]
    return jnp.concatenate([x1 * cos - x2 * sin, x2 * cos + x1 * sin, x[..., ROT_DIM:]], axis=-1)


def _offsets(window, dilation):
    return jnp.arange(window // dilation + 1) * dilation


def _branch(q, k_g, v_g, valid):
    s = jnp.einsum('bqhd,bqmhd->bqhm', q, k_g, precision="highest") * ATTN_SCALE
    s = jnp.where(valid[None, :, None, :], s, -jnp.inf)
    m = jnp.max(s, axis=-1, keepdims=True)
    e = jnp.exp(s - m)
    den = jnp.sum(e, axis=-1)
    o = jnp.einsum('bqhm,bqmhd->bqhd', e, v_g, precision="highest") / den[..., None]
    return m[..., 0] + jnp.log(den), o


def _combine(lses, outs):
    wts = jax.nn.softmax(jnp.stack(lses), axis=0)
    return jnp.einsum('rbqh,rbqhd->bqhd', wts, jnp.stack(outs))


def _attn_prompt(q, k, v):
    B, S, H, Dh = q.shape
    n_blocks = S // QUERY_BLOCK

    def block(i):
        t = i * QUERY_BLOCK + jnp.arange(QUERY_BLOCK)
        qb = lax.dynamic_slice_in_dim(q, i * QUERY_BLOCK, QUERY_BLOCK, axis=1)
        lses, outs = [], []
        for w, d in DILATIONS:
            idx = t[:, None] - _offsets(w, d)[None, :]
            valid = idx >= 0
            idx = jnp.clip(idx, 0, S - 1)
            lse, o = _branch(qb, jnp.take(k, idx, axis=1), jnp.take(v, idx, axis=1), valid)
            lses.append(lse)
            outs.append(o)
        return _combine(lses, outs)

    out = lax.map(block, jnp.arange(n_blocks))
    return out.transpose(1, 0, 2, 3, 4).reshape(B, S, H, Dh)


def _attn_sample(q, k_new, v_new, k_buf, v_buf):
    Q = q.shape[1]
    wbuf = k_buf.shape[1]
    t = wbuf + jnp.arange(Q)
    lses, outs = [], []
    for w, d in DILATIONS:
        idx = t[:, None] - _offsets(w, d)[None, :]
        in_new = (idx >= wbuf)[None, :, :, None, None]
        i_buf = jnp.clip(idx, 0, wbuf - 1)
        i_new = jnp.clip(idx - wbuf, 0, Q - 1)
        k_g = jnp.where(in_new, jnp.take(k_new, i_new, axis=1), jnp.take(k_buf, i_buf, axis=1))
        v_g = jnp.where(in_new, jnp.take(v_new, i_new, axis=1), jnp.take(v_buf, i_buf, axis=1))
        lse, o = _branch(q, k_g, v_g, idx >= 0)
        lses.append(lse)
        outs.append(o)
    return _combine(lses, outs)


def _gla_chunked(q, k, v, log_a, s0):
    B, L, H, DK = q.shape
    DV = v.shape[-1]
    C = math.gcd(L, GLA_CHUNK)
    n = L // C
    hp = "highest"

    def to_chunks(t):
        return t.reshape(B, n, C, H, t.shape[-1]).transpose(1, 0, 3, 2, 4)

    qc, kc, vc, ac = to_chunks(q), to_chunks(k), to_chunks(v), to_chunks(log_a)
    causal = jnp.tril(jnp.ones((C, C), dtype=bool))[:, :, None]

    def step(S, inp):
        qi, ki, vi, ai = inp
        b = jnp.cumsum(ai, axis=2)
        diff = b[:, :, :, None, :] - b[:, :, None, :, :]
        decay = jnp.exp(jnp.where(causal, diff, -jnp.inf))
        A = jnp.einsum('bhtk,bhsk,bhtsk->bhts', qi, ki, decay, precision=hp)
        o = (jnp.einsum('bhtk,bhkv->bhtv', qi * jnp.exp(b), S, precision=hp)
             + jnp.einsum('bhts,bhsv->bhtv', A, vi, precision=hp))
        b_end = b[:, :, -1:, :]
        S = jnp.exp(b_end[:, :, 0, :])[..., None] * S + jnp.einsum(
            'bhsk,bhsv->bhkv', ki * jnp.exp(b_end - b), vi, precision=hp)
        return S, o

    S, o = lax.scan(step, s0, (qc, kc, vc, ac))
    o = o.transpose(1, 0, 3, 2, 4).reshape(B, L, H, DV)
    return o, S


def _peer_routing(q, keys1, keys2):
    T = q.shape[0]
    q4 = q.reshape(T, PEER_HEADS, 2, PEER_HALF)
    s1 = jnp.einsum('thc,hkc->hkt', q4[:, :, 0], keys1, precision="highest")
    s2 = jnp.einsum('thc,hkc->hkt', q4[:, :, 1], keys2, precision="highest")
    v1 = -jnp.sort(-s1, axis=1)[:, :PEER_TOPK]
    v2 = -jnp.sort(-s2, axis=1)[:, :PEER_TOPK]
    cand = (v1[:, :, None, :] + v2[:, None, :, :]).reshape(PEER_HEADS, PEER_TOPK * PEER_TOPK, T)
    best = -jnp.sort(-cand, axis=1)[:, :PEER_TOPK]
    tau = best[:, PEER_TOPK - 1]
    m = best[:, 0]
    z = jnp.sum(jnp.exp(best - m[:, None, :]), axis=1)
    a1 = jnp.exp(s1 - v1[:, :1]) / z[:, None, :]
    e2 = jnp.exp(s2 - v2[:, :1])
    return s1, a1, s2, e2, tau


def _layer(x, pos, attend, gla_s0, w, B, L):
    T = B * L
    tm = min(512, T)
    z, _ = rms_matmul(x, w["norm_attn"], w["w_in_main"], tm, 512)
    lr, _ = rms_matmul(x, w["norm_attn"], w["w_in_lr"], tm, LANES)
    gate_lr = lr[:, :GLA_LOWRANK]
    pts = np.cumsum([ATTN_WIDTH, ATTN_WIDTH, ATTN_WIDTH, GLA_HEADS * GLA_DK, GLA_HEADS * GLA_DK,
                     GLA_WIDTH])[:-1]
    qa, ka, va, qg, kg, vg, gate_o = jnp.split(z, list(pts) + [MAIN_COLS - GLA_WIDTH], axis=-1)[:7]
    qa = _partial_rotary(qa.reshape(B, L, ATTN_HEADS, HEAD_DIM), pos)
    ka = _partial_rotary(ka.reshape(B, L, ATTN_HEADS, HEAD_DIM), pos)
    va = va.reshape(B, L, ATTN_HEADS, HEAD_DIM)
    o_attn = attend(qa, ka, va)
    log_a = (jax.nn.log_sigmoid(jnp.dot(gate_lr, w["w_gate_up"], precision="highest") + w["b_gate"])
             / GLA_TAU).reshape(B, L, GLA_HEADS, GLA_DK)
    o_gla, gla_state = _gla_chunked(qg.reshape(B, L, GLA_HEADS, GLA_DK) * (GLA_DK ** -0.5),
                                    kg.reshape(B, L, GLA_HEADS, GLA_DK),
                                    vg.reshape(B, L, GLA_HEADS, GLA_DV), log_a, gla_s0)
    o_gla = o_gla * lax.rsqrt(jnp.mean(o_gla * o_gla, axis=-1, keepdims=True) + NORM_EPS) * w["gla_norm"]
    o_gla = o_gla.reshape(T, GLA_WIDTH) * jax.nn.silu(gate_o)
    h = proj_residual(o_attn.reshape(T, ATTN_WIDTH).astype(BF16), o_gla.astype(BF16),
                      w["w_o_a"], w["w_o_b"], x, tm, 512)
    qp, xn2 = rms_matmul(h, w["norm_ffn"], w["peer_query"], tm, 512)
    s1, a1, s2, e2, tau = _peer_routing(qp, w["peer_keys_1"], w["peer_keys_2"])
    yT = peer_dense(xn2.T, w["expert_u"], w["expert_vT"], s1, a1, s2, e2, tau, tm, 512)
    out = final_norm(h, yT.T, w["final_norm"], tm)
    return out, ka, va, gla_state


def kernel(x_prompt, x_sample, cache_attn_k, cache_attn_v, state_gla, norm_attn, w_in, w_gate_up,
           b_gate, gla_norm, w_o, norm_ffn, peer_query, peer_keys_1, peer_keys_2, expert_u,
           expert_v, final_norm):
    Bp, Lp, _ = x_prompt.shape
    Bs, Ls, _ = x_sample.shape
    w = _prepare(norm_attn, w_in, w_gate_up, b_gate, gla_norm, w_o, norm_ffn, peer_query, peer_keys_1,
                 peer_keys_2, expert_u, expert_v, final_norm)
    win_p = min(WINDOW_MAX, Lp)
    yp, kp, vp, sp = _prompt_group(x_prompt, w)
    ys, ks, vs, ss = _sample_group(x_sample, cache_attn_k[0], cache_attn_v[0], state_gla[0], w)
    return (yp.reshape(Bp, Lp, D_MODEL), ys.reshape(Bs, Ls, D_MODEL),
            kp[:, Lp - win_p:][None], vp[:, Lp - win_p:][None], sp[None],
            ks[None], vs[None], ss[None])


def _prepare(norm_attn, w_in, w_gate_up, b_gate, gla_norm, w_o, norm_ffn, peer_query, peer_keys_1,
             peer_keys_2, expert_u, expert_v, final_norm):
    assert w_in.shape[0] == 1
    l = 0
    w_in_l = w_in[l]
    return {
        "norm_attn": norm_attn[l],
        "w_in_main": w_in_l[:, :MAIN_COLS].astype(BF16),
        "w_in_lr": jnp.pad(w_in_l[:, MAIN_COLS:], ((0, 0), (0, LANES - GLA_LOWRANK))).astype(BF16),
        "w_gate_up": w_gate_up[l],
        "b_gate": b_gate[l],
        "gla_norm": gla_norm[l],
        "w_o_a": w_o[l][:ATTN_WIDTH].astype(BF16),
        "w_o_b": w_o[l][ATTN_WIDTH:].astype(BF16),
        "norm_ffn": norm_ffn[l],
        "peer_query": peer_query[l].astype(BF16),
        "peer_keys_1": peer_keys_1[l],
        "peer_keys_2": peer_keys_2[l],
        "expert_u": expert_u[l].astype(BF16),
        "expert_vT": expert_v[l].astype(BF16).T,
        "final_norm": final_norm,
    }


def _sample_group(x_sample, cache_k, cache_v, state, w):
    Bs, Ls, _ = x_sample.shape
    pos_s = PAST_LEN + jnp.arange(Ls)
    attend_s = functools.partial(_attn_sample, k_buf=cache_k, v_buf=cache_v)
    return _layer(x_sample.reshape(Bs * Ls, D_MODEL), pos_s, attend_s, state, w, Bs, Ls)


def _prompt_group(x_prompt, w):
    Bp, Lp, _ = x_prompt.shape
    gla0 = jnp.zeros((Bp, GLA_HEADS, GLA_DK, GLA_DV), F32)
    return _layer(x_prompt.reshape(Bp * Lp, D_MODEL), jnp.arange(Lp), _attn_prompt, gla0, w, Bp, Lp)


def sample_group(inp):
    names = ("norm_attn", "w_in", "w_gate_up", "b_gate", "gla_norm", "w_o", "norm_ffn", "peer_query",
             "peer_keys_1", "peer_keys_2", "expert_u", "expert_v", "final_norm")
    w = _prepare(*[inp[n] for n in names])
    return _sample_group(inp["x_sample"], inp["cache_attn_k"][0], inp["cache_attn_v"][0],
                         inp["state_gla"][0], w)
```

```python
import functools
import math

import jax
import jax.numpy as jnp
import numpy as np
from jax import lax
from jax.experimental import pallas as pl
from jax.experimental.pallas import tpu as pltpu

F32 = jnp.float32
BF16 = jnp.bfloat16

D_MODEL = 2048
PAST_LEN = 2048
HEAD_DIM = 128
ATTN_HEADS = 8
ATTN_WIDTH = ATTN_HEADS * HEAD_DIM
DILATIONS = ((128, 1), (512, 4), (2048, 16))
WINDOW_MAX = 2048
QUERY_BLOCK = 128
ATTN_SCALE = HEAD_DIM ** -0.5
ROPE_THETA = 500000.0
ROT_DIM = HEAD_DIM // 4
ROT_HALF = ROT_DIM // 2
GLA_HEADS = 4
GLA_WIDTH = D_MODEL - ATTN_WIDTH
GLA_DV = GLA_WIDTH // GLA_HEADS
GLA_DK = GLA_DV // 2
GLA_KW = GLA_HEADS * GLA_DK
GLA_LOWRANK = 16
GLA_TAU = 16.0
GLA_SUB = 16
PEER_HEADS = 8
PEER_NKEYS = 128
PEER_EXPERTS = PEER_NKEYS * PEER_NKEYS
PEER_QDIM = 256
PEER_HALF = PEER_QDIM // 2
PEER_TOPK = 16
NORM_EPS = 1e-6
MAIN_COLS = 3 * ATTN_WIDTH + 2 * GLA_KW + 2 * GLA_WIDTH

LANES = 128
SUBLANES = 8
VMEM_LIMIT = 56 * 1024 * 1024
NEG = -1e30
HIGHEST = lax.Precision.HIGHEST
NT = (((1,), (1,)), ((), ()))
TN = (((0,), (0,)), ((), ()))


def _cparams(*sem):
    return pltpu.CompilerParams(dimension_semantics=sem, vmem_limit_bytes=VMEM_LIMIT)


def _rms(x, g):
    return x * lax.rsqrt(jnp.mean(x * x, axis=-1, keepdims=True) + NORM_EPS) * g


def _in_proj_body(x_ref, g_ref, w_ref, wlr_ref, wup_ref, bg_ref, cos_ref, sin_ref,
                  z_ref, la_ref, xn_ref, *, tn):
    j = pl.program_id(1)

    @pl.when(j == 0)
    def _():
        xn = _rms(x_ref[...], g_ref[...]).astype(BF16)
        xn_ref[...] = xn
        lr = jnp.dot(xn, wlr_ref[...], preferred_element_type=F32)
        zg = jnp.dot(lr, wup_ref[...], precision=HIGHEST, preferred_element_type=F32) + bg_ref[...]
        la_ref[...] = (jnp.minimum(zg, 0.0) - jnp.log1p(jnp.exp(-jnp.abs(zg)))) * (1.0 / GLA_TAU)

    acc = jnp.dot(xn_ref[...], w_ref[...], preferred_element_type=F32)

    @pl.when(j < 2 * ATTN_WIDTH // tn)
    def _():
        lane = lax.broadcasted_iota(jnp.int32, (acc.shape[0], HEAD_DIM), 1)
        cos = cos_ref[...]
        sin = sin_ref[...]
        for c in range(tn // HEAD_DIM):
            a = acc[:, c * HEAD_DIM:(c + 1) * HEAD_DIM]
            partner = jnp.where(lane < ROT_HALF, pltpu.roll(a, HEAD_DIM - ROT_HALF, 1),
                                pltpu.roll(a, ROT_HALF, 1))
            z_ref[:, c * HEAD_DIM:(c + 1) * HEAD_DIM] = a * cos + partner * sin

    @pl.when(j >= 2 * ATTN_WIDTH // tn)
    def _():
        z_ref[...] = acc


def _rotary_tables(pos):
    inv = jnp.exp(-math.log(ROPE_THETA) * jnp.arange(ROT_HALF, dtype=F32) * (2.0 / ROT_DIM))
    ang = pos.astype(F32)[:, None] * inv[None, :]
    n = pos.shape[0]
    cos = jnp.concatenate([jnp.cos(ang), jnp.cos(ang), jnp.ones((n, HEAD_DIM - ROT_DIM), F32)], axis=1)
    sin = jnp.concatenate([-jnp.sin(ang), jnp.sin(ang), jnp.zeros((n, HEAD_DIM - ROT_DIM), F32)], axis=1)
    return cos, sin


def in_proj(x, w, pos, tm, tn=512):
    T, D = x.shape
    P = pos.shape[0]
    assert T % tm == 0 and MAIN_COLS % tn == 0 and P % tm == 0 and (2 * ATTN_WIDTH) % tn == 0
    cos, sin = _rotary_tables(pos)
    const = lambda i, j: (0, 0)
    return pl.pallas_call(
        functools.partial(_in_proj_body, tn=tn),
        grid=(T // tm, MAIN_COLS // tn),
        in_specs=[
            pl.BlockSpec((tm, D), lambda i, j: (i, 0)),
            pl.BlockSpec((1, D), const),
            pl.BlockSpec((D, tn), lambda i, j: (0, j)),
            pl.BlockSpec((D, LANES), const),
            pl.BlockSpec((LANES, GLA_KW), const),
            pl.BlockSpec((1, GLA_KW), const),
            pl.BlockSpec((tm, HEAD_DIM), lambda i, j: (i % (P // tm), 0)),
            pl.BlockSpec((tm, HEAD_DIM), lambda i, j: (i % (P // tm), 0)),
        ],
        out_specs=[
            pl.BlockSpec((tm, tn), lambda i, j: (i, j)),
            pl.BlockSpec((tm, GLA_KW), lambda i, j: (i, 0)),
        ],
        out_shape=[
            jax.ShapeDtypeStruct((T, MAIN_COLS), F32),
            jax.ShapeDtypeStruct((T, GLA_KW), F32),
        ],
        scratch_shapes=[pltpu.VMEM((tm, D), BF16)],
        compiler_params=_cparams("parallel", "arbitrary"),
        name="in_proj",
    )(x, w["norm_attn"], w["w_in_main"], w["w_in_lr"], w["w_gate_up"], w["b_gate"], cos, sin)


def _attn_prompt_body(q_ref, k_ref, v_ref, o_ref, ob_ref, lse_ref, *, S):
    QB = QUERY_BLOCK
    row = lax.broadcasted_iota(jnp.int32, (QB, QB), 0)
    col = lax.broadcasted_iota(jnp.int32, (QB, QB), 1)
    cur_ok = col <= row
    prev_ok = col >= row

    for bi, (window, d) in enumerate(DILATIONS):
        assert window == d * QB
        nblk = S // d // QB

        def body(idx, carry, d=d, nblk=nblk, bi=bi):
            r = idx // nblk
            i = idx % nblk
            start = r + i * (d * QB)
            pstart = jnp.maximum(start - d * QB, r)
            if d == 1:
                rows = pl.ds(pl.multiple_of(start, QB), QB)
                prows = pl.ds(pl.multiple_of(pstart, QB), QB)
            else:
                rows = pl.ds(start, QB, stride=d)
                prows = pl.ds(pstart, QB, stride=d)
            qs = (q_ref[rows, :] * ATTN_SCALE).astype(BF16)
            s_c = lax.dot_general(qs, k_ref[rows, :].astype(BF16), NT, preferred_element_type=F32)
            s_p = lax.dot_general(qs, k_ref[prows, :].astype(BF16), NT, preferred_element_type=F32)
            s_c = jnp.where(cur_ok, s_c, NEG)
            s_p = jnp.where(jnp.logical_and(prev_ok, i > 0), s_p, NEG)
            m = jnp.maximum(jnp.max(s_c, axis=1, keepdims=True), jnp.max(s_p, axis=1, keepdims=True))
            p_c = jnp.exp(s_c - m)
            p_p = jnp.exp(s_p - m)
            den = jnp.sum(p_c, axis=1, keepdims=True) + jnp.sum(p_p, axis=1, keepdims=True)
            o = jnp.dot(p_c.astype(BF16), v_ref[rows, :].astype(BF16), preferred_element_type=F32)
            o += jnp.dot(p_p.astype(BF16), v_ref[prows, :].astype(BF16), preferred_element_type=F32)
            ob_ref[bi, rows, :] = o / den
            lse_ref[bi, rows, :] = jnp.broadcast_to(m + jnp.log(den), (QB, HEAD_DIM))
            return carry

        lax.fori_loop(0, d * nblk, body, 0)

    CH = 256

    def combine(c, carry):
        rows = pl.ds(pl.multiple_of(c * CH, CH), CH)
        l0, l1, l2 = lse_ref[0, rows, :], lse_ref[1, rows, :], lse_ref[2, rows, :]
        mx = jnp.maximum(jnp.maximum(l0, l1), l2)
        w0, w1, w2 = jnp.exp(l0 - mx), jnp.exp(l1 - mx), jnp.exp(l2 - mx)
        num = w0 * ob_ref[0, rows, :] + w1 * ob_ref[1, rows, :] + w2 * ob_ref[2, rows, :]
        o_ref[rows, :] = (num / (w0 + w1 + w2)).astype(o_ref.dtype)
        return carry

    lax.fori_loop(0, S // CH, combine, 0)


def attn_prompt(z, B, S):
    H = ATTN_HEADS
    assert S % (DILATIONS[-1][1] * QUERY_BLOCK) == 0
    return pl.pallas_call(
        functools.partial(_attn_prompt_body, S=S),
        grid=(B, H),
        in_specs=[
            pl.BlockSpec((S, HEAD_DIM), lambda b, h: (b, h)),
            pl.BlockSpec((S, HEAD_DIM), lambda b, h: (b, H + h)),
            pl.BlockSpec((S, HEAD_DIM), lambda b, h: (b, 2 * H + h)),
        ],
        out_specs=pl.BlockSpec((S, HEAD_DIM), lambda b, h: (b, h)),
        out_shape=jax.ShapeDtypeStruct((B * S, ATTN_WIDTH), BF16),
        scratch_shapes=[pltpu.VMEM((3, S, HEAD_DIM), F32), pltpu.VMEM((3, S, HEAD_DIM), F32)],
        compiler_params=_cparams("parallel", "parallel"),
        name="attn_prompt",
    )(z, z, z)


def _attn_sample_body(z_ref, k1_ref, k4_ref, k16_ref, v1_ref, v4_ref, v16_ref, sel_ref, selT_ref,
                      o_ref, *, bb):
    sel = sel_ref[...]
    selT = selT_ref[...]
    n_branch = len(DILATIONS)
    for b in range(bb):
        q = z_ref[0, b:b + 1, 0:ATTN_WIDTH] * ATTN_SCALE
        k_new = z_ref[0, b:b + 1, ATTN_WIDTH:2 * ATTN_WIDTH]
        v_new = z_ref[0, b:b + 1, 2 * ATTN_WIDTH:3 * ATTN_WIDTH]
        s_new = jnp.dot(k_new * q, sel, precision=HIGHEST, preferred_element_type=F32)
        s_win = [jnp.dot(kr[b] * q, sel, precision=HIGHEST, preferred_element_type=F32)
                 for kr in (k1_ref, k4_ref, k16_ref)]
        m = s_new
        for s in s_win:
            m = jnp.maximum(m, jnp.max(s, axis=0, keepdims=True))
        p_new = jnp.exp(s_new - m) * float(n_branch)
        den = p_new
        acc = jnp.dot(p_new, selT, precision=HIGHEST, preferred_element_type=F32) * v_new
        for s, vr in zip(s_win, (v1_ref, v4_ref, v16_ref)):
            p = jnp.exp(s - m)
            den = den + jnp.sum(p, axis=0, keepdims=True)
            pe = jnp.dot(p, selT, precision=HIGHEST, preferred_element_type=F32)
            acc = acc + jnp.sum(pe * vr[b], axis=0, keepdims=True)
        den_e = jnp.dot(den, selT, precision=HIGHEST, preferred_element_type=F32)
        o_ref[0, b:b + 1, :] = (acc / den_e).astype(o_ref.dtype)


def attn_sample(z, cache_k, cache_v, bb=4):
    Bs = z.shape[0]
    wbuf = cache_k.shape[1]
    QB = QUERY_BLOCK
    assert wbuf == PAST_LEN == WINDOW_MAX and Bs % bb == 0
    ck = cache_k.reshape(Bs, wbuf, ATTN_WIDTH)
    cv = cache_v.reshape(Bs, wbuf, ATTN_WIDTH)

    def views(c):
        out, specs = [], []
        for window, d in DILATIONS:
            n = wbuf // d
            out.append(c.reshape(Bs, n, d * ATTN_WIDTH))
            specs.append(pl.BlockSpec((bb, QB, ATTN_WIDTH), lambda i, n=n: (i, n // QB - 1, 0)))
        return out, specs

    kv, kspecs = views(ck)
    vv, vspecs = views(cv)
    head_of_lane = np.arange(ATTN_WIDTH) // HEAD_DIM
    sel = jnp.asarray(head_of_lane[:, None] == np.arange(LANES)[None, :], F32)
    z3 = z.reshape(Bs // bb, bb, MAIN_COLS)
    const = lambda i: (0, 0)
    out = pl.pallas_call(
        functools.partial(_attn_sample_body, bb=bb),
        grid=(Bs // bb,),
        in_specs=[pl.BlockSpec((1, bb, MAIN_COLS), lambda i: (i, 0, 0))] + kspecs + vspecs + [
            pl.BlockSpec((ATTN_WIDTH, LANES), const),
            pl.BlockSpec((LANES, ATTN_WIDTH), const),
        ],
        out_specs=pl.BlockSpec((1, bb, ATTN_WIDTH), lambda i: (i, 0, 0)),
        out_shape=jax.ShapeDtypeStruct((Bs // bb, bb, ATTN_WIDTH), BF16),
        compiler_params=_cparams("parallel"),
        name="attn_sample",
    )(z3, *kv, *vv, sel, sel.T)
    return out.reshape(Bs, ATTN_WIDTH)


def _gla_prompt_body(q_ref, k_ref, v_ref, go_ref, la_ref, gn_ref, o_ref, st_ref, sT_ref, b_ref, oi_ref,
                     *, C):
    c = pl.program_id(1)
    nsub = C // GLA_SUB

    @pl.when(c == 0)
    def _():
        sT_ref[...] = jnp.zeros_like(sT_ref)

    tri = (lax.broadcasted_iota(jnp.int32, (C, C), 1)
           <= lax.broadcasted_iota(jnp.int32, (C, C), 0)).astype(F32)
    srow = lax.broadcasted_iota(jnp.int32, (GLA_SUB, 1), 0)

    for h in range(GLA_HEADS):
        kl = pl.ds(h * GLA_DK, GLA_DK)
        vl = pl.ds(h * GLA_DV, GLA_DV)
        b = jnp.dot(tri, la_ref[:, kl], precision=HIGHEST, preferred_element_type=F32)
        b_ref[...] = b
        qh = q_ref[:, kl] * (GLA_DK ** -0.5)
        kh = k_ref[:, kl]
        vh = v_ref[:, vl]
        b_end = b[C - 1:C, :]
        sT = sT_ref[h]
        oi_ref[...] = lax.dot_general((qh * jnp.exp(b)).astype(BF16), sT.astype(BF16), NT,
                                      preferred_element_type=F32)
        k_end = (kh * jnp.exp(b_end - b)).astype(BF16)
        sT_ref[h] = sT * jnp.exp(b_end) + lax.dot_general(vh.astype(BF16), k_end, TN,
                                                          preferred_element_type=F32)
        for I in range(1, nsub):
            r0 = I * GLA_SUB
            cI = b[r0 - 1:r0, :]
            qI = (qh[r0:r0 + GLA_SUB] * jnp.exp(b[r0:r0 + GLA_SUB] - cI)).astype(BF16)
            kI = (kh[:r0] * jnp.exp(cI - b[:r0])).astype(BF16)
            a = lax.dot_general(qI, kI, NT, preferred_element_type=F32)
            oi_ref[r0:r0 + GLA_SUB, :] += jnp.dot(a.astype(BF16), vh[:r0].astype(BF16),
                                                  preferred_element_type=F32)

        def diag(I, carry, h=h):
            rows = pl.ds(pl.multiple_of(I * GLA_SUB, GLA_SUB), GLA_SUB)
            bI = b_ref[rows, :]
            qI = q_ref[rows, pl.ds(h * GLA_DK, GLA_DK)] * (GLA_DK ** -0.5)
            kI = k_ref[rows, pl.ds(h * GLA_DK, GLA_DK)]
            vI = v_ref[rows, pl.ds(h * GLA_DV, GLA_DV)]
            acc = jnp.zeros((GLA_SUB, GLA_DV), F32)
            for s in range(GLA_SUB):
                e = jnp.exp(jnp.minimum(bI - bI[s:s + 1, :], 0.0))
                a_col = jnp.sum(qI * kI[s:s + 1, :] * e, axis=1, keepdims=True)
                a_col = jnp.where(srow >= s, a_col, 0.0)
                acc += a_col * vI[s:s + 1, :]
            oi_ref[rows, :] += acc
            return carry

        lax.fori_loop(0, nsub, diag, 0)

        g = go_ref[:, vl]
        o_ref[:, vl] = (_rms(oi_ref[...], gn_ref[...]) * (g * jax.nn.sigmoid(g))).astype(o_ref.dtype)

    @pl.when(c == pl.num_programs(1) - 1)
    def _():
        for h in range(GLA_HEADS):
            st_ref[0, h] = sT_ref[h].T


def gla_prompt(z, log_a, gla_norm, B, S, C=128):
    assert S % C == 0 and C % GLA_SUB == 0
    nc = S // C
    q_blk = 3 * ATTN_WIDTH // GLA_KW
    v_blk = (3 * ATTN_WIDTH + 2 * GLA_KW) // GLA_WIDTH
    return pl.pallas_call(
        functools.partial(_gla_prompt_body, C=C),
        grid=(B, nc),
        in_specs=[
            pl.BlockSpec((C, GLA_KW), lambda b, c: (b * nc + c, q_blk)),
            pl.BlockSpec((C, GLA_KW), lambda b, c: (b * nc + c, q_blk + 1)),
            pl.BlockSpec((C, GLA_WIDTH), lambda b, c: (b * nc + c, v_blk)),
            pl.BlockSpec((C, GLA_WIDTH), lambda b, c: (b * nc + c, v_blk + 1)),
            pl.BlockSpec((C, GLA_KW), lambda b, c: (b * nc + c, 0)),
            pl.BlockSpec((1, GLA_DV), lambda b, c: (0, 0)),
        ],
        out_specs=[
            pl.BlockSpec((C, GLA_WIDTH), lambda b, c: (b * nc + c, 0)),
            pl.BlockSpec((1, GLA_HEADS, GLA_DK, GLA_DV), lambda b, c: (b, 0, 0, 0)),
        ],
        out_shape=[
            jax.ShapeDtypeStruct((B * S, GLA_WIDTH), BF16),
            jax.ShapeDtypeStruct((B, GLA_HEADS, GLA_DK, GLA_DV), F32),
        ],
        scratch_shapes=[
            pltpu.VMEM((GLA_HEADS, GLA_DV, GLA_DK), F32),
            pltpu.VMEM((C, GLA_DK), F32),
            pltpu.VMEM((C, GLA_DV), F32),
        ],
        compiler_params=_cparams("parallel", "arbitrary"),
        name="gla_prompt",
    )(z, z, z, z, log_a, gla_norm)


def _gla_sample_body(q_ref, k_ref, v_ref, go_ref, la_ref, gn_ref, s_ref, o_ref, so_ref, *, bb):
    for h in range(GLA_HEADS):
        kl = pl.ds(h * GLA_DK, GLA_DK)
        vl = pl.ds(h * GLA_DV, GLA_DV)
        aT = jnp.exp(la_ref[:, kl]).T
        kT = k_ref[:, kl].T
        qT = (q_ref[:, kl] * (GLA_DK ** -0.5)).T
        outs = []
        for b in range(bb):
            s_new = aT[:, b:b + 1] * s_ref[b, h] + kT[:, b:b + 1] * v_ref[b:b + 1, vl]
            so_ref[b, h] = s_new
            outs.append(jnp.sum(qT[:, b:b + 1] * s_new, axis=0, keepdims=True))
        o = jnp.concatenate(outs, axis=0)
        g = go_ref[:, vl]
        o_ref[:, vl] = (_rms(o, gn_ref[...]) * (g * jax.nn.sigmoid(g))).astype(o_ref.dtype)


def gla_sample(z, log_a, gla_norm, state, bb=SUBLANES):
    Bs = z.shape[0]
    assert Bs % bb == 0
    q_blk = 3 * ATTN_WIDTH // GLA_KW
    v_blk = (3 * ATTN_WIDTH + 2 * GLA_KW) // GLA_WIDTH
    st_spec = pl.BlockSpec((bb, GLA_HEADS, GLA_DK, GLA_DV), lambda i: (i, 0, 0, 0))
    return pl.pallas_call(
        functools.partial(_gla_sample_body, bb=bb),
        grid=(Bs // bb,),
        in_specs=[
            pl.BlockSpec((bb, GLA_KW), lambda i: (i, q_blk)),
            pl.BlockSpec((bb, GLA_KW), lambda i: (i, q_blk + 1)),
            pl.BlockSpec((bb, GLA_WIDTH), lambda i: (i, v_blk)),
            pl.BlockSpec((bb, GLA_WIDTH), lambda i: (i, v_blk + 1)),
            pl.BlockSpec((bb, GLA_KW), lambda i: (i, 0)),
            pl.BlockSpec((1, GLA_DV), lambda i: (0, 0)),
            st_spec,
        ],
        out_specs=[pl.BlockSpec((bb, GLA_WIDTH), lambda i: (i, 0)), st_spec],
        out_shape=[
            jax.ShapeDtypeStruct((Bs, GLA_WIDTH), BF16),
            jax.ShapeDtypeStruct(state.shape, F32),
        ],
        compiler_params=_cparams("parallel"),
        name="gla_sample",
    )(z, z, z, z, log_a, gla_norm, state)


def _proj_residual_body(a_ref, b_ref, wa_ref, wb_ref, res_ref, o_ref):
    acc = jnp.dot(a_ref[...], wa_ref[...], preferred_element_type=F32)
    acc += jnp.dot(b_ref[...], wb_ref[...], preferred_element_type=F32)
    o_ref[...] = res_ref[...] + acc


def proj_residual(a, b, wa, wb, res, tm, tn=512):
    T, Ka = a.shape
    Kb = b.shape[1]
    N = wa.shape[1]
    assert T % tm == 0 and N % tn == 0
    return pl.pallas_call(
        _proj_residual_body,
        grid=(T // tm, N // tn),
        in_specs=[
            pl.BlockSpec((tm, Ka), lambda i, j: (i, 0)),
            pl.BlockSpec((tm, Kb), lambda i, j: (i, 0)),
            pl.BlockSpec((Ka, tn), lambda i, j: (0, j)),
            pl.BlockSpec((Kb, tn), lambda i, j: (0, j)),
            pl.BlockSpec((tm, tn), lambda i, j: (i, j)),
        ],
        out_specs=pl.BlockSpec((tm, tn), lambda i, j: (i, j)),
        out_shape=jax.ShapeDtypeStruct((T, N), F32),
        compiler_params=_cparams("parallel", "arbitrary"),
        name="proj_residual",
    )(a, b, wa, wb, res)


def _peer_query_body(h_ref, g_ref, w_ref, q_ref, xT_ref, xn_ref):
    @pl.when(pl.program_id(1) == 0)
    def _():
        xn = _rms(h_ref[...], g_ref[...])
        xn_ref[...] = xn.astype(BF16)
        xT_ref[...] = xn.T.astype(BF16)

    q_ref[...] = jnp.dot(xn_ref[...], w_ref[...], preferred_element_type=F32)


def peer_query(h, g, w, tm, tn=512):
    T, D = h.shape
    N = w.shape[1]
    assert T % tm == 0 and N % tn == 0
    return pl.pallas_call(
        _peer_query_body,
        grid=(T // tm, N // tn),
        in_specs=[
            pl.BlockSpec((tm, D), lambda i, j: (i, 0)),
            pl.BlockSpec((1, D), lambda i, j: (0, 0)),
            pl.BlockSpec((D, tn), lambda i, j: (0, j)),
        ],
        out_specs=[
            pl.BlockSpec((tm, tn), lambda i, j: (i, j)),
            pl.BlockSpec((D, tm), lambda i, j: (0, i)),
        ],
        out_shape=[
            jax.ShapeDtypeStruct((T, N), F32),
            jax.ShapeDtypeStruct((D, T), BF16),
        ],
        scratch_shapes=[pltpu.VMEM((tm, D), BF16)],
        compiler_params=_cparams("parallel", "arbitrary"),
        name="peer_query",
    )(h, g, w)


def _peer_route_body(q_ref, k1_ref, k2_ref, s1_ref, a1_ref, s2_ref, e2_ref, tau_ref, *, tm):
    K = PEER_TOPK
    rowk = lax.broadcasted_iota(jnp.int32, (K, tm), 0)
    row8 = lax.broadcasted_iota(jnp.int32, (SUBLANES, tm), 0)
    rowh = lax.broadcasted_iota(jnp.int32, (PEER_HEADS, tm), 0)
    tau_all = jnp.zeros((PEER_HEADS, tm), F32)

    def top_values(s):
        vals = jnp.zeros((K, tm), F32)
        for it in range(K):
            mx = jnp.max(s, axis=0, keepdims=True)
            vals = jnp.where(rowk == it, mx, vals)
            s = jnp.where(s == mx, -jnp.inf, s)
        return vals

    for h in range(PEER_HEADS):
        q1 = q_ref[:, h * PEER_QDIM:h * PEER_QDIM + PEER_HALF]
        q2 = q_ref[:, h * PEER_QDIM + PEER_HALF:(h + 1) * PEER_QDIM]
        s1 = lax.dot_general(k1_ref[h], q1, NT, precision=HIGHEST, preferred_element_type=F32)
        s2 = lax.dot_general(k2_ref[h], q2, NT, precision=HIGHEST, preferred_element_type=F32)
        v1 = top_values(s1)
        v2 = top_values(s2)
        pieces = [v1[0:1] + v2[0:8], v1[0:1] + v2[8:16]]
        for a in range(1, SUBLANES):
            pieces.append(jnp.where(row8 < K // (a + 1), v1[a:a + 1] + v2[0:8], -jnp.inf))
        pieces.append(v1[8:16] + v2[0:1])
        cand = jnp.concatenate(pieces, axis=0)
        m = v1[0:1] + v2[0:1]
        z = jnp.zeros((1, tm), F32)
        tau = m
        for it in range(K):
            tau = jnp.max(cand, axis=0, keepdims=True)
            z = z + jnp.exp(tau - m)
            cand = jnp.where(cand == tau, -jnp.inf, cand)
        s1_ref[h] = s1
        s2_ref[h] = s2
        a1_ref[h] = jnp.exp(s1 - v1[0:1]) / z
        e2_ref[h] = jnp.exp(s2 - v2[0:1])
        tau_all = jnp.where(rowh == h, tau, tau_all)
    tau_ref[...] = tau_all


def peer_route(q, keys1, keys2, tm):
    T = q.shape[0]
    assert T % tm == 0 and tm % LANES == 0
    tile = pl.BlockSpec((PEER_HEADS, PEER_NKEYS, tm), lambda i: (0, 0, i))
    tile_shape = jax.ShapeDtypeStruct((PEER_HEADS, PEER_NKEYS, T), F32)
    kspec = pl.BlockSpec((PEER_HEADS, PEER_NKEYS, PEER_HALF), lambda i: (0, 0, 0))
    return pl.pallas_call(
        functools.partial(_peer_route_body, tm=tm),
        grid=(T // tm,),
        in_specs=[pl.BlockSpec((tm, PEER_HEADS * PEER_QDIM), lambda i: (i, 0)), kspec, kspec],
        out_specs=[tile, tile, tile, tile, pl.BlockSpec((PEER_HEADS, tm), lambda i: (0, i))],
        out_shape=[tile_shape, tile_shape, tile_shape, tile_shape,
                   jax.ShapeDtypeStruct((PEER_HEADS, T), F32)],
        compiler_params=_cparams("parallel"),
        name="peer_route",
    )(q, keys1, keys2)


def _peer_dense_body(xT_ref, u_ref, vT_ref, s1_ref, a1_ref, s2_ref, e2_ref, tau_ref,
                     yT_ref, act_ref, *, te, tm):
    e = pl.program_id(1)

    @pl.when(e == 0)
    def _():
        yT_ref[...] = jnp.zeros_like(yT_ref)

    hT = jnp.dot(u_ref[...], xT_ref[...], preferred_element_type=F32)
    act_ref[...] = 0.5 * hT * (1.0 + lax.erf(hT * (2.0 ** -0.5)))

    def per_i1(c, carry):
        rows = pl.ds(pl.multiple_of(c * PEER_NKEYS, PEER_NKEYS), PEER_NKEYS)
        for tc in range(tm // LANES):
            lanes = pl.ds(tc * LANES, LANES)
            gate = jnp.zeros((PEER_NKEYS, LANES), F32)
            for h in range(PEER_HEADS):
                s1row = s1_ref[h, c, :, lanes]
                a1row = a1_ref[h, c, :, lanes]
                sel = (s2_ref[h, :, lanes] + s1row) >= tau_ref[pl.ds(h, 1), lanes]
                gate += jnp.where(sel, e2_ref[h, :, lanes] * a1row, 0.0)
            act_ref[rows, lanes] = act_ref[rows, lanes] * gate
        return carry

    lax.fori_loop(0, te // PEER_NKEYS, per_i1, 0)

    yT_ref[...] += jnp.dot(vT_ref[...], act_ref[...].astype(BF16),
                           preferred_element_type=F32)


def peer_dense(xT, u, vT, s1, a1, s2, e2, tau, tm, te=512):
    D, T = xT.shape
    E = u.shape[0]
    assert T % tm == 0 and E % te == 0 and te % PEER_NKEYS == 0 and tm % LANES == 0
    n_i1 = te // PEER_NKEYS
    row_spec = pl.BlockSpec((PEER_HEADS, n_i1, 1, tm), lambda i, e: (0, e, 0, i))
    tile_spec = pl.BlockSpec((PEER_HEADS, PEER_NKEYS, tm), lambda i, e: (0, 0, i))
    s1 = s1.reshape(PEER_HEADS, PEER_NKEYS, 1, T)
    a1 = a1.reshape(PEER_HEADS, PEER_NKEYS, 1, T)
    return pl.pallas_call(
        functools.partial(_peer_dense_body, te=te, tm=tm),
        grid=(T // tm, E // te),
        in_specs=[
            pl.BlockSpec((D, tm), lambda i, e: (0, i)),
            pl.BlockSpec((te, D), lambda i, e: (e, 0)),
            pl.BlockSpec((D, te), lambda i, e: (0, e)),
            row_spec, row_spec, tile_spec, tile_spec,
            pl.BlockSpec((PEER_HEADS, tm), lambda i, e: (0, i)),
        ],
        out_specs=pl.BlockSpec((D, tm), lambda i, e: (0, i)),
        out_shape=jax.ShapeDtypeStruct((D, T), F32),
        scratch_shapes=[pltpu.VMEM((te, tm), F32)],
        compiler_params=_cparams("parallel", "arbitrary"),
        name="peer_dense",
    )(xT, u, vT, s1, a1, s2, e2, tau)


def _final_norm_body(h_ref, yT_ref, g_ref, o_ref):
    o_ref[...] = _rms(h_ref[...] + yT_ref[...].T, g_ref[...])


def final_norm(h, yT, g, tm):
    T, D = h.shape
    return pl.pallas_call(
        _final_norm_body,
        grid=(T // tm,),
        in_specs=[
            pl.BlockSpec((tm, D), lambda i: (i, 0)),
            pl.BlockSpec((D, tm), lambda i: (0, i)),
            pl.BlockSpec((1, D), lambda i: (0, 0)),
        ],
        out_specs=pl.BlockSpec((tm, D), lambda i: (i, 0)),
        out_shape=jax.ShapeDtypeStruct((T, D), F32),
        compiler_params=_cparams("parallel"),
        name="final_norm",
    )(h, yT, g)


def _post_mix(x, o_attn, o_gla, w, tm):
    h = proj_residual(o_attn, o_gla, w["w_o_a"], w["w_o_b"], x, tm)
    qp, xT = peer_query(h, w["norm_ffn"], w["peer_query"], tm)
    s1, a1, s2, e2, tau = peer_route(qp, w["peer_keys_1"], w["peer_keys_2"], tm)
    yT = peer_dense(xT, w["expert_u"], w["expert_vT"], s1, a1, s2, e2, tau, tm)
    return final_norm(h, yT, w["final_norm"], tm)


def _prompt_group(x_prompt, w):
    B, S, _ = x_prompt.shape
    x = x_prompt.reshape(B * S, D_MODEL)
    tm = 512
    z, log_a = in_proj(x, w, jnp.arange(S), tm)
    o_attn = attn_prompt(z, B, S)
    o_gla, state = gla_prompt(z, log_a, w["gla_norm"], B, S)
    out = _post_mix(x, o_attn, o_gla, w, tm)
    return out, z, state


def _sample_group(x_sample, cache_k, cache_v, state, w):
    Bs, Ls, _ = x_sample.shape
    assert Ls == 1
    x = x_sample.reshape(Bs, D_MODEL)
    tm = Bs
    z, log_a = in_proj(x, w, jnp.full((tm,), PAST_LEN, jnp.int32), tm)
    o_attn = attn_sample(z, cache_k, cache_v)
    o_gla, new_state = gla_sample(z, log_a, w["gla_norm"], state)
    out = _post_mix(x, o_attn, o_gla, w, tm)
    return out, z, new_state


def _prepare(norm_attn, w_in, w_gate_up, b_gate, gla_norm, w_o, norm_ffn, peer_query_w, peer_keys_1,
             peer_keys_2, expert_u, expert_v, final_norm_w):
    assert w_in.shape[0] == 1
    l = 0
    w_in_l = w_in[l]
    row = lambda v: v.reshape(1, -1).astype(F32)
    return {
        "norm_attn": row(norm_attn[l]),
        "w_in_main": w_in_l[:, :MAIN_COLS].astype(BF16),
        "w_in_lr": jnp.pad(w_in_l[:, MAIN_COLS:], ((0, 0), (0, LANES - GLA_LOWRANK))).astype(BF16),
        "w_gate_up": jnp.pad(w_gate_up[l], ((0, LANES - GLA_LOWRANK), (0, 0))),
        "b_gate": row(b_gate[l]),
        "gla_norm": row(gla_norm[l]),
        "w_o_a": w_o[l][:ATTN_WIDTH].astype(BF16),
        "w_o_b": w_o[l][ATTN_WIDTH:].astype(BF16),
        "norm_ffn": row(norm_ffn[l]),
        "peer_query": peer_query_w[l].astype(BF16),
        "peer_keys_1": peer_keys_1[l],
        "peer_keys_2": peer_keys_2[l],
        "expert_u": expert_u[l].astype(BF16),
        "expert_vT": expert_v[l].astype(BF16).T,
        "final_norm": row(final_norm_w),
    }


def kernel(x_prompt, x_sample, cache_attn_k, cache_attn_v, state_gla, norm_attn, w_in, w_gate_up,
           b_gate, gla_norm, w_o, norm_ffn, peer_query, peer_keys_1, peer_keys_2, expert_u,
           expert_v, final_norm):
    Bp, Lp, _ = x_prompt.shape
    Bs, Ls, _ = x_sample.shape
    w = _prepare(norm_attn, w_in, w_gate_up, b_gate, gla_norm, w_o, norm_ffn, peer_query, peer_keys_1,
                 peer_keys_2, expert_u, expert_v, final_norm)
    win_p = min(WINDOW_MAX, Lp)
    yp, zp, sp = _prompt_group(x_prompt, w)
    ys, zs, ss = _sample_group(x_sample, cache_attn_k[0], cache_attn_v[0], state_gla[0], w)

    def heads(z, lo, B, L):
        return z[:, lo:lo + ATTN_WIDTH].reshape(B, L, ATTN_HEADS, HEAD_DIM)

    kp = heads(zp, ATTN_WIDTH, Bp, Lp)[:, Lp - win_p:]
    vp = heads(zp, 2 * ATTN_WIDTH, Bp, Lp)[:, Lp - win_p:]
    return (yp.reshape(Bp, Lp, D_MODEL), ys.reshape(Bs, Ls, D_MODEL),
            kp[None], vp[None], sp[None],
            heads(zs, ATTN_WIDTH, Bs, Ls)[None], heads(zs, 2 * ATTN_WIDTH, Bs, Ls)[None], ss[None])


def sample_group(inp):
    names = ("norm_attn", "w_in", "w_gate_up", "b_gate", "gla_norm", "w_o", "norm_ffn", "peer_query",
             "peer_keys_1", "peer_keys_2", "expert_u", "expert_v", "final_norm")
    w = _prepare(*[inp[n] for n in names])
    out, z, st = _sample_group(inp["x_sample"], inp["cache_attn_k"][0], inp["cache_attn_v"][0],
                               inp["state_gla"][0], w)
    Bs = out.shape[0]
    return (out, z[:, ATTN_WIDTH:2 * ATTN_WIDTH].reshape(Bs, 1, ATTN_HEADS, HEAD_DIM),
            z[:, 2 * ATTN_WIDTH:3 * ATTN_WIDTH].reshape(Bs, 1, ATTN_HEADS, HEAD_DIM), st)
```

```python
import functools
import math

import jax
import jax.numpy as jnp
import numpy as np
from jax import lax
from jax.experimental import pallas as pl
from jax.experimental.pallas import tpu as pltpu

F32 = jnp.float32
BF16 = jnp.bfloat16

D_MODEL = 2048
PAST_LEN = 2048
HEAD_DIM = 128
ATTN_HEADS = 8
ATTN_WIDTH = ATTN_HEADS * HEAD_DIM
DILATIONS = ((128, 1), (512, 4), (2048, 16))
WINDOW_MAX = 2048
QUERY_BLOCK = 128
ATTN_SCALE = HEAD_DIM ** -0.5
ROPE_THETA = 500000.0
ROT_DIM = HEAD_DIM // 4
ROT_HALF = ROT_DIM // 2
GLA_HEADS = 4
GLA_WIDTH = D_MODEL - ATTN_WIDTH
GLA_DV = GLA_WIDTH // GLA_HEADS
GLA_DK = GLA_DV // 2
GLA_KW = GLA_HEADS * GLA_DK
GLA_LOWRANK = 16
GLA_TAU = 16.0
GLA_SUB = 16
PEER_HEADS = 8
PEER_NKEYS = 128
PEER_EXPERTS = PEER_NKEYS * PEER_NKEYS
PEER_QDIM = 256
PEER_HALF = PEER_QDIM // 2
PEER_TOPK = 16
NORM_EPS = 1e-6
MAIN_COLS = 3 * ATTN_WIDTH + 2 * GLA_KW + 2 * GLA_WIDTH

LANES = 128
SUBLANES = 8
VMEM_LIMIT = 56 * 1024 * 1024
NEG = -1e30
HIGHEST = lax.Precision.HIGHEST
NT = (((1,), (1,)), ((), ()))
TN = (((0,), (0,)), ((), ()))


def _cparams(*sem):
    return pltpu.CompilerParams(dimension_semantics=sem, vmem_limit_bytes=VMEM_LIMIT)


def _rms(x, g):
    return x * lax.rsqrt(jnp.mean(x * x, axis=-1, keepdims=True) + NORM_EPS) * g


def _in_proj_body(x_ref, g_ref, w_ref, wlr_ref, wup_ref, bg_ref, cos_ref, sin_ref,
                  z_ref, la_ref, xn_ref, *, tn):
    j = pl.program_id(1)

    @pl.when(j == 0)
    def _():
        xn = _rms(x_ref[...], g_ref[...]).astype(BF16)
        xn_ref[...] = xn
        lr = jnp.dot(xn, wlr_ref[...], preferred_element_type=F32)
        zg = jnp.dot(lr, wup_ref[...], precision=HIGHEST, preferred_element_type=F32) + bg_ref[...]
        la_ref[...] = (jnp.minimum(zg, 0.0) - jnp.log1p(jnp.exp(-jnp.abs(zg)))) * (1.0 / GLA_TAU)

    acc = jnp.dot(xn_ref[...], w_ref[...], preferred_element_type=F32)

    @pl.when(j < 2 * ATTN_WIDTH // tn)
    def _():
        lane = lax.broadcasted_iota(jnp.int32, (acc.shape[0], HEAD_DIM), 1)
        cos = cos_ref[...]
        sin = sin_ref[...]
        for c in range(tn // HEAD_DIM):
            a = acc[:, c * HEAD_DIM:(c + 1) * HEAD_DIM]
            partner = jnp.where(lane < ROT_HALF, pltpu.roll(a, HEAD_DIM - ROT_HALF, 1),
                                pltpu.roll(a, ROT_HALF, 1))
            z_ref[:, c * HEAD_DIM:(c + 1) * HEAD_DIM] = a * cos + partner * sin

    @pl.when(j >= 2 * ATTN_WIDTH // tn)
    def _():
        z_ref[...] = acc


def _rotary_tables(pos):
    inv = jnp.exp(-math.log(ROPE_THETA) * jnp.arange(ROT_HALF, dtype=F32) * (2.0 / ROT_DIM))
    ang = pos.astype(F32)[:, None] * inv[None, :]
    n = pos.shape[0]
    cos = jnp.concatenate([jnp.cos(ang), jnp.cos(ang), jnp.ones((n, HEAD_DIM - ROT_DIM), F32)], axis=1)
    sin = jnp.concatenate([-jnp.sin(ang), jnp.sin(ang), jnp.zeros((n, HEAD_DIM - ROT_DIM), F32)], axis=1)
    return cos, sin


def in_proj(x, w, pos, tm, tn=512):
    T, D = x.shape
    P = pos.shape[0]
    assert T % tm == 0 and MAIN_COLS % tn == 0 and P % tm == 0 and (2 * ATTN_WIDTH) % tn == 0
    cos, sin = _rotary_tables(pos)
    const = lambda i, j: (0, 0)
    return pl.pallas_call(
        functools.partial(_in_proj_body, tn=tn),
        grid=(T // tm, MAIN_COLS // tn),
        in_specs=[
            pl.BlockSpec((tm, D), lambda i, j: (i, 0)),
            pl.BlockSpec((1, D), const),
            pl.BlockSpec((D, tn), lambda i, j: (0, j)),
            pl.BlockSpec((D, LANES), const),
            pl.BlockSpec((LANES, GLA_KW), const),
            pl.BlockSpec((1, GLA_KW), const),
            pl.BlockSpec((tm, HEAD_DIM), lambda i, j: (i % (P // tm), 0)),
            pl.BlockSpec((tm, HEAD_DIM), lambda i, j: (i % (P // tm), 0)),
        ],
        out_specs=[
            pl.BlockSpec((tm, tn), lambda i, j: (i, j)),
            pl.BlockSpec((tm, GLA_KW), lambda i, j: (i, 0)),
        ],
        out_shape=[
            jax.ShapeDtypeStruct((T, MAIN_COLS), F32),
            jax.ShapeDtypeStruct((T, GLA_KW), F32),
        ],
        scratch_shapes=[pltpu.VMEM((tm, D), BF16)],
        compiler_params=_cparams("parallel", "arbitrary"),
        name="in_proj",
    )(x, w["norm_attn"], w["w_in_main"], w["w_in_lr"], w["w_gate_up"], w["b_gate"], cos, sin)


def _attn_prompt_body(q_ref, k_ref, v_ref, o_ref, ob_ref, lse_ref, *, S):
    QB = QUERY_BLOCK
    row = lax.broadcasted_iota(jnp.int32, (QB, QB), 0)
    col = lax.broadcasted_iota(jnp.int32, (QB, QB), 1)
    cur_ok = col <= row
    prev_ok = col >= row

    for bi, (window, d) in enumerate(DILATIONS):
        assert window == d * QB
        nblk = S // d // QB

        def body(idx, carry, d=d, nblk=nblk, bi=bi):
            r = idx // nblk
            i = idx % nblk
            start = r + i * (d * QB)
            pstart = jnp.maximum(start - d * QB, r)
            if d == 1:
                rows = pl.ds(pl.multiple_of(start, QB), QB)
                prows = pl.ds(pl.multiple_of(pstart, QB), QB)
            else:
                rows = pl.ds(start, QB, stride=d)
                prows = pl.ds(pstart, QB, stride=d)
            qs = (q_ref[rows, :] * ATTN_SCALE).astype(BF16)
            s_c = lax.dot_general(qs, k_ref[rows, :].astype(BF16), NT, preferred_element_type=F32)
            s_p = lax.dot_general(qs, k_ref[prows, :].astype(BF16), NT, preferred_element_type=F32)
            s_c = jnp.where(cur_ok, s_c, NEG)
            s_p = jnp.where(jnp.logical_and(prev_ok, i > 0), s_p, NEG)
            m = jnp.maximum(jnp.max(s_c, axis=1, keepdims=True), jnp.max(s_p, axis=1, keepdims=True))
            p_c = jnp.exp(s_c - m)
            p_p = jnp.exp(s_p - m)
            den = jnp.sum(p_c, axis=1, keepdims=True) + jnp.sum(p_p, axis=1, keepdims=True)
            o = jnp.dot(p_c.astype(BF16), v_ref[rows, :].astype(BF16), preferred_element_type=F32)
            o += jnp.dot(p_p.astype(BF16), v_ref[prows, :].astype(BF16), preferred_element_type=F32)
            ob_ref[bi, rows, :] = o / den
            lse_ref[bi, rows, :] = jnp.broadcast_to(m + jnp.log(den), (QB, HEAD_DIM))
            return carry

        lax.fori_loop(0, d * nblk, body, 0)

    CH = 256

    def combine(c, carry):
        rows = pl.ds(pl.multiple_of(c * CH, CH), CH)
        l0, l1, l2 = lse_ref[0, rows, :], lse_ref[1, rows, :], lse_ref[2, rows, :]
        mx = jnp.maximum(jnp.maximum(l0, l1), l2)
        w0, w1, w2 = jnp.exp(l0 - mx), jnp.exp(l1 - mx), jnp.exp(l2 - mx)
        num = w0 * ob_ref[0, rows, :] + w1 * ob_ref[1, rows, :] + w2 * ob_ref[2, rows, :]
        o_ref[rows, :] = (num / (w0 + w1 + w2)).astype(o_ref.dtype)
        return carry

    lax.fori_loop(0, S // CH, combine, 0)


def attn_prompt(z, B, S):
    H = ATTN_HEADS
    assert S % (DILATIONS[-1][1] * QUERY_BLOCK) == 0
    return pl.pallas_call(
        functools.partial(_attn_prompt_body, S=S),
        grid=(B, H),
        in_specs=[
            pl.BlockSpec((S, HEAD_DIM), lambda b, h: (b, h)),
            pl.BlockSpec((S, HEAD_DIM), lambda b, h: (b, H + h)),
            pl.BlockSpec((S, HEAD_DIM), lambda b, h: (b, 2 * H + h)),
        ],
        out_specs=pl.BlockSpec((S, HEAD_DIM), lambda b, h: (b, h)),
        out_shape=jax.ShapeDtypeStruct((B * S, ATTN_WIDTH), BF16),
        scratch_shapes=[pltpu.VMEM((3, S, HEAD_DIM), F32), pltpu.VMEM((3, S, HEAD_DIM), F32)],
        compiler_params=_cparams("parallel", "parallel"),
        name="attn_prompt",
    )(z, z, z)


def _attn_sample_body(z_ref, k1_ref, k4_ref, k16_ref, v1_ref, v4_ref, v16_ref, o_ref, *, bb):
    H = ATTN_HEADS
    for b in range(bb):
        q = z_ref[b, 0:H, :] * ATTN_SCALE
        k_new = z_ref[b, H:2 * H, :]
        v_new = z_ref[b, 2 * H:3 * H, :]
        s_new = jnp.sum(k_new * q, axis=-1, keepdims=True)
        s_win = [jnp.sum(kr[b] * q, axis=-1, keepdims=True)
                 for kr in (k1_ref, k4_ref, k16_ref)]
        m = s_new
        for s in s_win:
            m = jnp.maximum(m, jnp.max(s, axis=0))
        p_new = jnp.exp(s_new - m) * float(len(DILATIONS))
        den = p_new
        acc = p_new * v_new
        for s, vr in zip(s_win, (v1_ref, v4_ref, v16_ref)):
            p = jnp.exp(s - m)
            den = den + jnp.sum(p, axis=0)
            acc = acc + jnp.sum(p * vr[b], axis=0)
        o_ref[b] = acc / den


def attn_sample(z, cache_k, cache_v, bb=4):
    Bs = z.shape[0]
    wbuf = cache_k.shape[1]
    QB, H = QUERY_BLOCK, ATTN_HEADS
    assert wbuf == PAST_LEN == WINDOW_MAX and Bs % bb == 0

    def views(c):
        out, specs = [], []
        for window, d in DILATIONS:
            n = wbuf // d
            if d == 1:
                out.append(c)
                specs.append(pl.BlockSpec((bb, QB, H, HEAD_DIM), lambda i, n=n: (i, n // QB - 1, 0, 0)))
            else:
                out.append(c.reshape(Bs, n, d, H, HEAD_DIM))
                specs.append(pl.BlockSpec((bb, QB, None, H, HEAD_DIM),
                                          lambda i, n=n: (i, n // QB - 1, 0, 0, 0)))
        return out, specs

    kv, kspecs = views(cache_k)
    vv, vspecs = views(cache_v)
    z3 = z.reshape(Bs, MAIN_COLS // HEAD_DIM, HEAD_DIM)
    out = pl.pallas_call(
        functools.partial(_attn_sample_body, bb=bb),
        grid=(Bs // bb,),
        in_specs=[pl.BlockSpec((bb, MAIN_COLS // HEAD_DIM, HEAD_DIM), lambda i: (i, 0, 0))] + kspecs + vspecs,
        out_specs=pl.BlockSpec((bb, H, HEAD_DIM), lambda i: (i, 0, 0)),
        out_shape=jax.ShapeDtypeStruct((Bs, H, HEAD_DIM), F32),
        compiler_params=_cparams("parallel"),
        name="attn_sample",
    )(z3, *kv, *vv)
    return out.reshape(Bs, ATTN_WIDTH).astype(BF16)


def _gla_prompt_body(q_ref, k_ref, v_ref, go_ref, la_ref, gn_ref, o_ref, st_ref, sT_ref, b_ref, oi_ref,
                     *, C):
    c = pl.program_id(1)
    nsub = C // GLA_SUB

    @pl.when(c == 0)
    def _():
        sT_ref[...] = jnp.zeros_like(sT_ref)

    tri = (lax.broadcasted_iota(jnp.int32, (C, C), 1)
           <= lax.broadcasted_iota(jnp.int32, (C, C), 0)).astype(F32)
    srow = lax.broadcasted_iota(jnp.int32, (GLA_SUB, 1), 0)

    for h in range(GLA_HEADS):
        kl = pl.ds(h * GLA_DK, GLA_DK)
        vl = pl.ds(h * GLA_DV, GLA_DV)
        b = jnp.dot(tri, la_ref[:, kl], precision=HIGHEST, preferred_element_type=F32)
        b_ref[...] = b
        qh = q_ref[:, kl] * (GLA_DK ** -0.5)
        kh = k_ref[:, kl]
        vh = v_ref[:, vl]
        b_end = b[C - 1:C, :]
        sT = sT_ref[h]
        oi_ref[...] = lax.dot_general((qh * jnp.exp(b)).astype(BF16), sT.astype(BF16), NT,
                                      preferred_element_type=F32)
        k_end = (kh * jnp.exp(b_end - b)).astype(BF16)
        sT_ref[h] = sT * jnp.exp(b_end) + lax.dot_general(vh.astype(BF16), k_end, TN,
                                                          preferred_element_type=F32)
        for I in range(1, nsub):
            r0 = I * GLA_SUB
            cI = b[r0 - 1:r0, :]
            qI = (qh[r0:r0 + GLA_SUB] * jnp.exp(b[r0:r0 + GLA_SUB] - cI)).astype(BF16)
            kI = (kh[:r0] * jnp.exp(cI - b[:r0])).astype(BF16)
            a = lax.dot_general(qI, kI, NT, preferred_element_type=F32)
            oi_ref[r0:r0 + GLA_SUB, :] += jnp.dot(a.astype(BF16), vh[:r0].astype(BF16),
                                                  preferred_element_type=F32)

        def diag(I, carry, h=h):
            rows = pl.ds(pl.multiple_of(I * GLA_SUB, GLA_SUB), GLA_SUB)
            bI = b_ref[rows, :]
            qI = q_ref[rows, pl.ds(h * GLA_DK, GLA_DK)] * (GLA_DK ** -0.5)
            kI = k_ref[rows, pl.ds(h * GLA_DK, GLA_DK)]
            vI = v_ref[rows, pl.ds(h * GLA_DV, GLA_DV)]
            acc = jnp.zeros((GLA_SUB, GLA_DV), F32)
            for s in range(GLA_SUB):
                e = jnp.exp(jnp.minimum(bI - bI[s:s + 1, :], 0.0))
                a_col = jnp.sum(qI * kI[s:s + 1, :] * e, axis=1, keepdims=True)
                a_col = jnp.where(srow >= s, a_col, 0.0)
                acc += a_col * vI[s:s + 1, :]
            oi_ref[rows, :] += acc
            return carry

        lax.fori_loop(0, nsub, diag, 0)

        g = go_ref[:, vl]
        o_ref[:, vl] = (_rms(oi_ref[...], gn_ref[...]) * (g * jax.nn.sigmoid(g))).astype(o_ref.dtype)

    @pl.when(c == pl.num_programs(1) - 1)
    def _():
        for h in range(GLA_HEADS):
            st_ref[0, h] = sT_ref[h].T


def gla_prompt(z, log_a, gla_norm, B, S, C=128):
    assert S % C == 0 and C % GLA_SUB == 0
    nc = S // C
    q_blk = 3 * ATTN_WIDTH // GLA_KW
    v_blk = (3 * ATTN_WIDTH + 2 * GLA_KW) // GLA_WIDTH
    return pl.pallas_call(
        functools.partial(_gla_prompt_body, C=C),
        grid=(B, nc),
        in_specs=[
            pl.BlockSpec((C, GLA_KW), lambda b, c: (b * nc + c, q_blk)),
            pl.BlockSpec((C, GLA_KW), lambda b, c: (b * nc + c, q_blk + 1)),
            pl.BlockSpec((C, GLA_WIDTH), lambda b, c: (b * nc + c, v_blk)),
            pl.BlockSpec((C, GLA_WIDTH), lambda b, c: (b * nc + c, v_blk + 1)),
            pl.BlockSpec((C, GLA_KW), lambda b, c: (b * nc + c, 0)),
            pl.BlockSpec((1, GLA_DV), lambda b, c: (0, 0)),
        ],
        out_specs=[
            pl.BlockSpec((C, GLA_WIDTH), lambda b, c: (b * nc + c, 0)),
            pl.BlockSpec((1, GLA_HEADS, GLA_DK, GLA_DV), lambda b, c: (b, 0, 0, 0)),
        ],
        out_shape=[
            jax.ShapeDtypeStruct((B * S, GLA_WIDTH), BF16),
            jax.ShapeDtypeStruct((B, GLA_HEADS, GLA_DK, GLA_DV), F32),
        ],
        scratch_shapes=[
            pltpu.VMEM((GLA_HEADS, GLA_DV, GLA_DK), F32),
            pltpu.VMEM((C, GLA_DK), F32),
            pltpu.VMEM((C, GLA_DV), F32),
        ],
        compiler_params=_cparams("parallel", "arbitrary"),
        name="gla_prompt",
    )(z, z, z, z, log_a, gla_norm)


def _gla_sample_body(q_ref, k_ref, v_ref, go_ref, la_ref, gn_ref, s_ref, o_ref, so_ref, *, bb):
    for h in range(GLA_HEADS):
        kl = pl.ds(h * GLA_DK, GLA_DK)
        vl = pl.ds(h * GLA_DV, GLA_DV)
        aT = jnp.exp(la_ref[:, kl]).T
        kT = k_ref[:, kl].T
        qT = (q_ref[:, kl] * (GLA_DK ** -0.5)).T
        outs = []
        for b in range(bb):
            s_new = aT[:, b:b + 1] * s_ref[b, h] + kT[:, b:b + 1] * v_ref[b:b + 1, vl]
            so_ref[b, h] = s_new
            outs.append(jnp.sum(qT[:, b:b + 1] * s_new, axis=0, keepdims=True))
        o = jnp.concatenate(outs, axis=0)
        g = go_ref[:, vl]
        o_ref[:, vl] = (_rms(o, gn_ref[...]) * (g * jax.nn.sigmoid(g))).astype(o_ref.dtype)


def gla_sample(z, log_a, gla_norm, state, bb=SUBLANES):
    Bs = z.shape[0]
    assert Bs % bb == 0
    q_blk = 3 * ATTN_WIDTH // GLA_KW
    v_blk = (3 * ATTN_WIDTH + 2 * GLA_KW) // GLA_WIDTH
    st_spec = pl.BlockSpec((bb, GLA_HEADS, GLA_DK, GLA_DV), lambda i: (i, 0, 0, 0))
    return pl.pallas_call(
        functools.partial(_gla_sample_body, bb=bb),
        grid=(Bs // bb,),
        in_specs=[
            pl.BlockSpec((bb, GLA_KW), lambda i: (i, q_blk)),
            pl.BlockSpec((bb, GLA_KW), lambda i: (i, q_blk + 1)),
            pl.BlockSpec((bb, GLA_WIDTH), lambda i: (i, v_blk)),
            pl.BlockSpec((bb, GLA_WIDTH), lambda i: (i, v_blk + 1)),
            pl.BlockSpec((bb, GLA_KW), lambda i: (i, 0)),
            pl.BlockSpec((1, GLA_DV), lambda i: (0, 0)),
            st_spec,
        ],
        out_specs=[pl.BlockSpec((bb, GLA_WIDTH), lambda i: (i, 0)), st_spec],
        out_shape=[
            jax.ShapeDtypeStruct((Bs, GLA_WIDTH), BF16),
            jax.ShapeDtypeStruct(state.shape, F32),
        ],
        compiler_params=_cparams("parallel"),
        name="gla_sample",
    )(z, z, z, z, log_a, gla_norm, state)


def _proj_residual_body(a_ref, b_ref, wa_ref, wb_ref, res_ref, o_ref):
    acc = jnp.dot(a_ref[...], wa_ref[...], preferred_element_type=F32)
    acc += jnp.dot(b_ref[...], wb_ref[...], preferred_element_type=F32)
    o_ref[...] = res_ref[...] + acc


def proj_residual(a, b, wa, wb, res, tm, tn=512):
    T, Ka = a.shape
    Kb = b.shape[1]
    N = wa.shape[1]
    assert T % tm == 0 and N % tn == 0
    return pl.pallas_call(
        _proj_residual_body,
        grid=(T // tm, N // tn),
        in_specs=[
            pl.BlockSpec((tm, Ka), lambda i, j: (i, 0)),
            pl.BlockSpec((tm, Kb), lambda i, j: (i, 0)),
            pl.BlockSpec((Ka, tn), lambda i, j: (0, j)),
            pl.BlockSpec((Kb, tn), lambda i, j: (0, j)),
            pl.BlockSpec((tm, tn), lambda i, j: (i, j)),
        ],
        out_specs=pl.BlockSpec((tm, tn), lambda i, j: (i, j)),
        out_shape=jax.ShapeDtypeStruct((T, N), F32),
        compiler_params=_cparams("parallel", "arbitrary"),
        name="proj_residual",
    )(a, b, wa, wb, res)


def _peer_query_body(h_ref, g_ref, w_ref, q_ref, xT_ref, xn_ref):
    @pl.when(pl.program_id(1) == 0)
    def _():
        xn = _rms(h_ref[...], g_ref[...])
        xn_ref[...] = xn.astype(BF16)
        xT_ref[...] = xn.T.astype(BF16)

    q_ref[...] = jnp.dot(xn_ref[...], w_ref[...], preferred_element_type=F32)


def peer_query(h, g, w, tm, tn=512):
    T, D = h.shape
    N = w.shape[1]
    assert T % tm == 0 and N % tn == 0
    return pl.pallas_call(
        _peer_query_body,
        grid=(T // tm, N // tn),
        in_specs=[
            pl.BlockSpec((tm, D), lambda i, j: (i, 0)),
            pl.BlockSpec((1, D), lambda i, j: (0, 0)),
            pl.BlockSpec((D, tn), lambda i, j: (0, j)),
        ],
        out_specs=[
            pl.BlockSpec((tm, tn), lambda i, j: (i, j)),
            pl.BlockSpec((D, tm), lambda i, j: (0, i)),
        ],
        out_shape=[
            jax.ShapeDtypeStruct((T, N), F32),
            jax.ShapeDtypeStruct((D, T), BF16),
        ],
        scratch_shapes=[pltpu.VMEM((tm, D), BF16)],
        compiler_params=_cparams("parallel", "arbitrary"),
        name="peer_query",
    )(h, g, w)


def _peer_route_body(q_ref, k1_ref, k2_ref, s1_ref, s2_ref, thr_ref, *, tm):
    K = PEER_TOPK
    KR = -(-(K + 1) // SUBLANES) * SUBLANES
    rowk = lax.broadcasted_iota(jnp.int32, (KR, tm), 0)
    row8 = lax.broadcasted_iota(jnp.int32, (SUBLANES, tm), 0)
    rowh = lax.broadcasted_iota(jnp.int32, (PEER_HEADS, tm), 0)
    thr_all = jnp.zeros((PEER_HEADS, tm), F32)

    def top_values(s):
        vals = jnp.full((KR, tm), -jnp.inf, F32)
        for it in range(K + 1):
            mx = jnp.max(s, axis=0, keepdims=True)
            vals = jnp.where(rowk == it, mx, vals)
            s = jnp.where(s == mx, -jnp.inf, s)
        return vals

    for h in range(PEER_HEADS):
        q1 = q_ref[:, h * PEER_QDIM:h * PEER_QDIM + PEER_HALF]
        q2 = q_ref[:, h * PEER_QDIM + PEER_HALF:(h + 1) * PEER_QDIM]
        s1 = lax.dot_general(k1_ref[h], q1, NT, precision=HIGHEST, preferred_element_type=F32)
        s2 = lax.dot_general(k2_ref[h], q2, NT, precision=HIGHEST, preferred_element_type=F32)
        v1 = top_values(s1)
        v2 = top_values(s2)
        pieces = [v1[0:1] + v2[r:r + SUBLANES] for r in range(0, KR, SUBLANES)]
        for a in range(1, SUBLANES):
            pieces.append(jnp.where(row8 < (K + 1) // (a + 1), v1[a:a + 1] + v2[0:SUBLANES], -jnp.inf))
        pieces += [v1[r:r + SUBLANES] + v2[0:1] for r in range(SUBLANES, KR, SUBLANES)]
        cand = jnp.concatenate(pieces, axis=0)
        m = v1[0:1] + v2[0:1]
        z = jnp.zeros((1, tm), F32)
        kth = m
        for it in range(K):
            kth = jnp.max(cand, axis=0, keepdims=True)
            z = z + jnp.exp(kth - m)
            cand = jnp.where(cand == kth, -jnp.inf, cand)
        nxt = jnp.max(cand, axis=0, keepdims=True)
        c = m + jnp.log(z)
        s1_ref[h] = s1 - c
        s2_ref[h] = s2
        thr_all = jnp.where(rowh == h, 0.5 * (kth + nxt) - c, thr_all)
    thr_ref[...] = thr_all


def peer_route(q, keys1, keys2, tm):
    T = q.shape[0]
    assert T % tm == 0 and tm % LANES == 0 and (PEER_TOPK + 1) // 2 <= SUBLANES
    tile = pl.BlockSpec((PEER_HEADS, PEER_NKEYS, tm), lambda i: (0, 0, i))
    tile_shape = jax.ShapeDtypeStruct((PEER_HEADS, PEER_NKEYS, T), F32)
    kspec = pl.BlockSpec((PEER_HEADS, PEER_NKEYS, PEER_HALF), lambda i: (0, 0, 0))
    return pl.pallas_call(
        functools.partial(_peer_route_body, tm=tm),
        grid=(T // tm,),
        in_specs=[pl.BlockSpec((tm, PEER_HEADS * PEER_QDIM), lambda i: (i, 0)), kspec, kspec],
        out_specs=[tile, tile, pl.BlockSpec((PEER_HEADS, tm), lambda i: (0, i))],
        out_shape=[tile_shape, tile_shape, jax.ShapeDtypeStruct((PEER_HEADS, T), F32)],
        compiler_params=_cparams("parallel"),
        name="peer_route",
    )(q, keys1, keys2)


def _peer_dense_body(xT_ref, u_ref, vT_ref, s1_ref, s2_ref, thr_ref, yT_ref, gate_ref, act_ref, *, te, tm):
    e = pl.program_id(1)

    @pl.when(e == 0)
    def _():
        yT_ref[...] = jnp.zeros_like(yT_ref)

    for c in range(te // PEER_NKEYS):
        for tc in range(tm // LANES):
            lanes = pl.ds(tc * LANES, LANES)
            gate = jnp.zeros((PEER_NKEYS, LANES), F32)
            for h in range(PEER_HEADS):
                ssum = s2_ref[h, :, lanes] + s1_ref[h, c, :, lanes]
                gate += jnp.where(ssum >= thr_ref[pl.ds(h, 1), lanes], jnp.exp(ssum), 0.0)
            gate_ref[pl.ds(c * PEER_NKEYS, PEER_NKEYS), lanes] = gate

    hT = jnp.dot(u_ref[...], xT_ref[...], preferred_element_type=F32)
    gelu = 0.5 * hT * (1.0 + lax.erf(hT * (2.0 ** -0.5)))
    act_ref[...] = (gelu * gate_ref[...]).astype(BF16)
    yT_ref[...] += jnp.dot(vT_ref[...], act_ref[...], preferred_element_type=F32)


def peer_dense(xT, u, vT, s1, s2, thr, tm, te=512):
    D, T = xT.shape
    E = u.shape[0]
    assert T % tm == 0 and E % te == 0 and te % PEER_NKEYS == 0 and tm % LANES == 0
    n_i1 = te // PEER_NKEYS
    s1 = s1.reshape(PEER_HEADS, PEER_NKEYS, 1, T)
    return pl.pallas_call(
        functools.partial(_peer_dense_body, te=te, tm=tm),
        grid=(T // tm, E // te),
        in_specs=[
            pl.BlockSpec((D, tm), lambda i, e: (0, i)),
            pl.BlockSpec((te, D), lambda i, e: (e, 0)),
            pl.BlockSpec((D, te), lambda i, e: (0, e)),
            pl.BlockSpec((PEER_HEADS, n_i1, 1, tm), lambda i, e: (0, e, 0, i)),
            pl.BlockSpec((PEER_HEADS, PEER_NKEYS, tm), lambda i, e: (0, 0, i)),
            pl.BlockSpec((PEER_HEADS, tm), lambda i, e: (0, i)),
        ],
        out_specs=pl.BlockSpec((D, tm), lambda i, e: (0, i)),
        out_shape=jax.ShapeDtypeStruct((D, T), F32),
        scratch_shapes=[pltpu.VMEM((te, tm), F32), pltpu.VMEM((te, tm), BF16)],
        compiler_params=_cparams("parallel", "arbitrary"),
        name="peer_dense",
    )(xT, u, vT, s1, s2, thr)


def _final_norm_body(h_ref, yT_ref, g_ref, o_ref):
    o_ref[...] = _rms(h_ref[...] + yT_ref[...].T, g_ref[...])


def final_norm(h, yT, g, tm):
    T, D = h.shape
    return pl.pallas_call(
        _final_norm_body,
        grid=(T // tm,),
        in_specs=[
            pl.BlockSpec((tm, D), lambda i: (i, 0)),
            pl.BlockSpec((D, tm), lambda i: (0, i)),
            pl.BlockSpec((1, D), lambda i: (0, 0)),
        ],
        out_specs=pl.BlockSpec((tm, D), lambda i: (i, 0)),
        out_shape=jax.ShapeDtypeStruct((T, D), F32),
        compiler_params=_cparams("parallel"),
        name="final_norm",
    )(h, yT, g)


def _post_mix(x, o_attn, o_gla, w, tm):
    h = proj_residual(o_attn, o_gla, w["w_o_a"], w["w_o_b"], x, tm)
    qp, xT = peer_query(h, w["norm_ffn"], w["peer_query"], tm)
    s1, s2, thr = peer_route(qp, w["peer_keys_1"], w["peer_keys_2"], tm)
    yT = peer_dense(xT, w["expert_u"], w["expert_vT"], s1, s2, thr, tm)
    return final_norm(h, yT, w["final_norm"], tm)


def _prompt_group(x_prompt, w):
    B, S, _ = x_prompt.shape
    x = x_prompt.reshape(B * S, D_MODEL)
    tm = 512
    z, log_a = in_proj(x, w, jnp.arange(S), tm)
    o_attn = attn_prompt(z, B, S)
    o_gla, state = gla_prompt(z, log_a, w["gla_norm"], B, S)
    out = _post_mix(x, o_attn, o_gla, w, tm)
    return out, z, state


def _sample_group(x_sample, cache_k, cache_v, state, w):
    Bs, Ls, _ = x_sample.shape
    assert Ls == 1
    x = x_sample.reshape(Bs, D_MODEL)
    tm = Bs
    z, log_a = in_proj(x, w, jnp.full((tm,), PAST_LEN, jnp.int32), tm)
    o_attn = attn_sample(z, cache_k, cache_v)
    o_gla, new_state = gla_sample(z, log_a, w["gla_norm"], state)
    out = _post_mix(x, o_attn, o_gla, w, tm)
    return out, z, new_state


def _prepare(norm_attn, w_in, w_gate_up, b_gate, gla_norm, w_o, norm_ffn, peer_query_w, peer_keys_1,
             peer_keys_2, expert_u, expert_v, final_norm_w):
    assert w_in.shape[0] == 1
    l = 0
    w_in_l = w_in[l]
    row = lambda v: v.reshape(1, -1).astype(F32)
    return {
        "norm_attn": row(norm_attn[l]),
        "w_in_main": w_in_l[:, :MAIN_COLS].astype(BF16),
        "w_in_lr": jnp.pad(w_in_l[:, MAIN_COLS:], ((0, 0), (0, LANES - GLA_LOWRANK))).astype(BF16),
        "w_gate_up": jnp.pad(w_gate_up[l], ((0, LANES - GLA_LOWRANK), (0, 0))),
        "b_gate": row(b_gate[l]),
        "gla_norm": row(gla_norm[l]),
        "w_o_a": w_o[l][:ATTN_WIDTH].astype(BF16),
        "w_o_b": w_o[l][ATTN_WIDTH:].astype(BF16),
        "norm_ffn": row(norm_ffn[l]),
        "peer_query": peer_query_w[l].astype(BF16),
        "peer_keys_1": peer_keys_1[l],
        "peer_keys_2": peer_keys_2[l],
        "expert_u": expert_u[l].astype(BF16),
        "expert_vT": expert_v[l].astype(BF16).T,
        "final_norm": row(final_norm_w),
    }


def kernel(x_prompt, x_sample, cache_attn_k, cache_attn_v, state_gla, norm_attn, w_in, w_gate_up,
           b_gate, gla_norm, w_o, norm_ffn, peer_query, peer_keys_1, peer_keys_2, expert_u,
           expert_v, final_norm):
    Bp, Lp, _ = x_prompt.shape
    Bs, Ls, _ = x_sample.shape
    w = _prepare(norm_attn, w_in, w_gate_up, b_gate, gla_norm, w_o, norm_ffn, peer_query, peer_keys_1,
                 peer_keys_2, expert_u, expert_v, final_norm)
    win_p = min(WINDOW_MAX, Lp)
    yp, zp, sp = _prompt_group(x_prompt, w)
    ys, zs, ss = _sample_group(x_sample, cache_attn_k[0], cache_attn_v[0], state_gla[0], w)

    def heads(z, lo, B, L):
        return z[:, lo:lo + ATTN_WIDTH].reshape(B, L, ATTN_HEADS, HEAD_DIM)

    kp = heads(zp, ATTN_WIDTH, Bp, Lp)[:, Lp - win_p:]
    vp = heads(zp, 2 * ATTN_WIDTH, Bp, Lp)[:, Lp - win_p:]
    return (yp.reshape(Bp, Lp, D_MODEL), ys.reshape(Bs, Ls, D_MODEL),
            kp[None], vp[None], sp[None],
            heads(zs, ATTN_WIDTH, Bs, Ls)[None], heads(zs, 2 * ATTN_WIDTH, Bs, Ls)[None], ss[None])


def sample_group(inp):
    names = ("norm_attn", "w_in", "w_gate_up", "b_gate", "gla_norm", "w_o", "norm_ffn", "peer_query",
             "peer_keys_1", "peer_keys_2", "expert_u", "expert_v", "final_norm")
    w = _prepare(*[inp[n] for n in names])
    out, z, st = _sample_group(inp["x_sample"], inp["cache_attn_k"][0], inp["cache_attn_v"][0],
                               inp["state_gla"][0], w)
    Bs = out.shape[0]
    return (out, z[:, ATTN_WIDTH:2 * ATTN_WIDTH].reshape(Bs, 1, ATTN_HEADS, HEAD_DIM),
            z[:, 2 * ATTN_WIDTH:3 * ATTN_WIDTH].reshape(Bs, 1, ATTN_HEADS, HEAD_DIM), st)
```

```python
import functools
import math

import jax
import jax.numpy as jnp
import numpy as np
from jax import lax
from jax.experimental import pallas as pl
from jax.experimental.pallas import tpu as pltpu

F32 = jnp.float32
BF16 = jnp.bfloat16

D_MODEL = 2048
PAST_LEN = 2048
HEAD_DIM = 128
ATTN_HEADS = 8
ATTN_WIDTH = ATTN_HEADS * HEAD_DIM
DILATIONS = ((128, 1), (512, 4), (2048, 16))
WINDOW_MAX = 2048
QUERY_BLOCK = 128
ATTN_SCALE = HEAD_DIM ** -0.5
ROPE_THETA = 500000.0
ROT_DIM = HEAD_DIM // 4
ROT_HALF = ROT_DIM // 2
GLA_HEADS = 4
GLA_WIDTH = D_MODEL - ATTN_WIDTH
GLA_DV = GLA_WIDTH // GLA_HEADS
GLA_DK = GLA_DV // 2
GLA_KW = GLA_HEADS * GLA_DK
GLA_LOWRANK = 16
GLA_TAU = 16.0
GLA_SUB = 16
PEER_HEADS = 8
PEER_NKEYS = 128
PEER_EXPERTS = PEER_NKEYS * PEER_NKEYS
PEER_QDIM = 256
PEER_HALF = PEER_QDIM // 2
PEER_TOPK = 16
NORM_EPS = 1e-6
MAIN_COLS = 3 * ATTN_WIDTH + 2 * GLA_KW + 2 * GLA_WIDTH

LANES = 128
SUBLANES = 8
VMEM_LIMIT = 56 * 1024 * 1024
NEG = -1e30
LOG2E = 1.0 / math.log(2.0)
HIGHEST = lax.Precision.HIGHEST
NT = (((1,), (1,)), ((), ()))
TN = (((0,), (0,)), ((), ()))


def _cparams(*sem):
    return pltpu.CompilerParams(dimension_semantics=sem, vmem_limit_bytes=VMEM_LIMIT)


def _rms(x, g):
    return x * lax.rsqrt(jnp.mean(x * x, axis=-1, keepdims=True) + NORM_EPS) * g


def _in_proj_body(x_ref, g_ref, w_ref, wlr_ref, wup_ref, bg_ref, cos_ref, sin_ref,
                  z_ref, la_ref, xn_ref, *, tn):
    j = pl.program_id(1)

    @pl.when(j == 0)
    def _():
        xn = _rms(x_ref[...], g_ref[...]).astype(BF16)
        xn_ref[...] = xn
        lr = jnp.dot(xn, wlr_ref[...], preferred_element_type=F32)
        zg = jnp.dot(lr, wup_ref[...], precision=HIGHEST, preferred_element_type=F32) + bg_ref[...]
        la_ref[...] = (jnp.minimum(zg, 0.0) - jnp.log1p(jnp.exp(-jnp.abs(zg)))) * (1.0 / GLA_TAU)

    acc = jnp.dot(xn_ref[...], w_ref[...], preferred_element_type=F32)

    @pl.when(j < 2 * ATTN_WIDTH // tn)
    def _():
        lane = lax.broadcasted_iota(jnp.int32, (acc.shape[0], HEAD_DIM), 1)
        cos = cos_ref[...]
        sin = sin_ref[...]
        for c in range(tn // HEAD_DIM):
            a = acc[:, c * HEAD_DIM:(c + 1) * HEAD_DIM]
            partner = jnp.where(lane < ROT_HALF, pltpu.roll(a, HEAD_DIM - ROT_HALF, 1),
                                pltpu.roll(a, ROT_HALF, 1))
            z_ref[:, c * HEAD_DIM:(c + 1) * HEAD_DIM] = a * cos + partner * sin

    @pl.when(j >= 2 * ATTN_WIDTH // tn)
    def _():
        z_ref[...] = acc


def _rotary_tables(pos):
    inv = jnp.exp(-math.log(ROPE_THETA) * jnp.arange(ROT_HALF, dtype=F32) * (2.0 / ROT_DIM))
    ang = pos.astype(F32)[:, None] * inv[None, :]
    n = pos.shape[0]
    cos = jnp.concatenate([jnp.cos(ang), jnp.cos(ang), jnp.ones((n, HEAD_DIM - ROT_DIM), F32)], axis=1)
    sin = jnp.concatenate([-jnp.sin(ang), jnp.sin(ang), jnp.zeros((n, HEAD_DIM - ROT_DIM), F32)], axis=1)
    return cos, sin


def in_proj(x, w, pos, tm, tn=512):
    T, D = x.shape
    P = pos.shape[0]
    assert T % tm == 0 and MAIN_COLS % tn == 0 and P % tm == 0 and (2 * ATTN_WIDTH) % tn == 0
    cos, sin = _rotary_tables(pos)
    const = lambda i, j: (0, 0)
    return pl.pallas_call(
        functools.partial(_in_proj_body, tn=tn),
        grid=(T // tm, MAIN_COLS // tn),
        in_specs=[
            pl.BlockSpec((tm, D), lambda i, j: (i, 0)),
            pl.BlockSpec((1, D), const),
            pl.BlockSpec((D, tn), lambda i, j: (0, j)),
            pl.BlockSpec((D, LANES), const),
            pl.BlockSpec((LANES, GLA_KW), const),
            pl.BlockSpec((1, GLA_KW), const),
            pl.BlockSpec((tm, HEAD_DIM), lambda i, j: (i % (P // tm), 0)),
            pl.BlockSpec((tm, HEAD_DIM), lambda i, j: (i % (P // tm), 0)),
        ],
        out_specs=[
            pl.BlockSpec((tm, tn), lambda i, j: (i, j)),
            pl.BlockSpec((tm, GLA_KW), lambda i, j: (i, 0)),
        ],
        out_shape=[
            jax.ShapeDtypeStruct((T, MAIN_COLS), F32),
            jax.ShapeDtypeStruct((T, GLA_KW), F32),
        ],
        scratch_shapes=[pltpu.VMEM((tm, D), BF16)],
        compiler_params=_cparams("parallel", "arbitrary"),
        name="in_proj",
    )(x, w["norm_attn"], w["w_in_main"], w["w_in_lr"], w["w_gate_up"], w["b_gate"], cos, sin)


def _attn_prompt_body(q_ref, k_ref, v_ref, o_ref, ob_ref, lse_ref, *, S):
    QB = QUERY_BLOCK
    row = lax.broadcasted_iota(jnp.int32, (QB, QB), 0)
    col = lax.broadcasted_iota(jnp.int32, (QB, QB), 1)
    cur_ok = col <= row
    prev_ok = col >= row

    for bi, (window, d) in enumerate(DILATIONS):
        assert window == d * QB
        nblk = S // d // QB

        def body(idx, carry, d=d, nblk=nblk, bi=bi):
            r = idx // nblk
            i = idx % nblk
            start = r + i * (d * QB)
            pstart = jnp.maximum(start - d * QB, r)
            if d == 1:
                rows = pl.ds(pl.multiple_of(start, QB), QB)
                prows = pl.ds(pl.multiple_of(pstart, QB), QB)
            else:
                rows = pl.ds(start, QB, stride=d)
                prows = pl.ds(pstart, QB, stride=d)
            qs = (q_ref[rows, :] * ATTN_SCALE).astype(BF16)
            s_c = lax.dot_general(qs, k_ref[rows, :].astype(BF16), NT, preferred_element_type=F32)
            s_p = lax.dot_general(qs, k_ref[prows, :].astype(BF16), NT, preferred_element_type=F32)
            s_c = jnp.where(cur_ok, s_c, NEG)
            s_p = jnp.where(jnp.logical_and(prev_ok, i > 0), s_p, NEG)
            m = jnp.maximum(jnp.max(s_c, axis=1, keepdims=True), jnp.max(s_p, axis=1, keepdims=True))
            p_c = jnp.exp(s_c - m)
            p_p = jnp.exp(s_p - m)
            den = jnp.sum(p_c, axis=1, keepdims=True) + jnp.sum(p_p, axis=1, keepdims=True)
            o = jnp.dot(p_c.astype(BF16), v_ref[rows, :].astype(BF16), preferred_element_type=F32)
            o += jnp.dot(p_p.astype(BF16), v_ref[prows, :].astype(BF16), preferred_element_type=F32)
            ob_ref[bi, rows, :] = o / den
            lse_ref[bi, rows, :] = jnp.broadcast_to(m + jnp.log(den), (QB, HEAD_DIM))
            return carry

        lax.fori_loop(0, d * nblk, body, 0, unroll=4)

    CH = 256

    def combine(c, carry):
        rows = pl.ds(pl.multiple_of(c * CH, CH), CH)
        l0, l1, l2 = lse_ref[0, rows, :], lse_ref[1, rows, :], lse_ref[2, rows, :]
        mx = jnp.maximum(jnp.maximum(l0, l1), l2)
        w0, w1, w2 = jnp.exp(l0 - mx), jnp.exp(l1 - mx), jnp.exp(l2 - mx)
        num = w0 * ob_ref[0, rows, :] + w1 * ob_ref[1, rows, :] + w2 * ob_ref[2, rows, :]
        o_ref[rows, :] = (num / (w0 + w1 + w2)).astype(o_ref.dtype)
        return carry

    lax.fori_loop(0, S // CH, combine, 0)


def attn_prompt(z, B, S):
    H = ATTN_HEADS
    assert S % (DILATIONS[-1][1] * QUERY_BLOCK) == 0
    return pl.pallas_call(
        functools.partial(_attn_prompt_body, S=S),
        grid=(B, H),
        in_specs=[
            pl.BlockSpec((S, HEAD_DIM), lambda b, h: (b, h)),
            pl.BlockSpec((S, HEAD_DIM), lambda b, h: (b, H + h)),
            pl.BlockSpec((S, HEAD_DIM), lambda b, h: (b, 2 * H + h)),
        ],
        out_specs=pl.BlockSpec((S, HEAD_DIM), lambda b, h: (b, h)),
        out_shape=jax.ShapeDtypeStruct((B * S, ATTN_WIDTH), BF16),
        scratch_shapes=[pltpu.VMEM((3, S, HEAD_DIM), F32), pltpu.VMEM((3, S, HEAD_DIM), F32)],
        compiler_params=_cparams("parallel", "parallel"),
        name="attn_prompt",
    )(z, z, z)


def _attn_sample_body(z_ref, k1_ref, k4_ref, k16_ref, v1_ref, v4_ref, v16_ref, o_ref, *, bb):
    H = ATTN_HEADS
    for b in range(bb):
        q = z_ref[b, 0:H, :] * ATTN_SCALE
        k_new = z_ref[b, H:2 * H, :]
        v_new = z_ref[b, 2 * H:3 * H, :]
        s_new = jnp.sum(k_new * q, axis=-1, keepdims=True)
        s_win = [jnp.sum(kr[b] * q, axis=-1, keepdims=True)
                 for kr in (k1_ref, k4_ref, k16_ref)]
        m = s_new
        for s in s_win:
            m = jnp.maximum(m, jnp.max(s, axis=0))
        p_new = jnp.exp(s_new - m) * float(len(DILATIONS))
        den = p_new
        acc = p_new * v_new
        for s, vr in zip(s_win, (v1_ref, v4_ref, v16_ref)):
            p = jnp.exp(s - m)
            den = den + jnp.sum(p, axis=0)
            acc = acc + jnp.sum(p * vr[b], axis=0)
        o_ref[b] = acc / den


def attn_sample(z, cache_k, cache_v, bb=4):
    Bs = z.shape[0]
    wbuf = cache_k.shape[1]
    QB, H = QUERY_BLOCK, ATTN_HEADS
    assert wbuf == PAST_LEN == WINDOW_MAX and Bs % bb == 0

    def views(c):
        out, specs = [], []
        for window, d in DILATIONS:
            n = wbuf // d
            if d == 1:
                out.append(c)
                specs.append(pl.BlockSpec((bb, QB, H, HEAD_DIM), lambda i, n=n: (i, n // QB - 1, 0, 0)))
            else:
                out.append(c.reshape(Bs, n, d, H, HEAD_DIM))
                specs.append(pl.BlockSpec((bb, QB, None, H, HEAD_DIM),
                                          lambda i, n=n: (i, n // QB - 1, 0, 0, 0)))
        return out, specs

    kv, kspecs = views(cache_k)
    vv, vspecs = views(cache_v)
    z3 = z.reshape(Bs, MAIN_COLS // HEAD_DIM, HEAD_DIM)
    out = pl.pallas_call(
        functools.partial(_attn_sample_body, bb=bb),
        grid=(Bs // bb,),
        in_specs=[pl.BlockSpec((bb, MAIN_COLS // HEAD_DIM, HEAD_DIM), lambda i: (i, 0, 0))] + kspecs + vspecs,
        out_specs=pl.BlockSpec((bb, H, HEAD_DIM), lambda i: (i, 0, 0)),
        out_shape=jax.ShapeDtypeStruct((Bs, H, HEAD_DIM), F32),
        compiler_params=_cparams("parallel"),
        name="attn_sample",
    )(z3, *kv, *vv)
    return out.reshape(Bs, ATTN_WIDTH).astype(BF16)


def _gla_prompt_body(q_ref, k_ref, v_ref, go_ref, la_ref, gn_ref, o_ref, st_ref, sT_ref, b_ref, oi_ref,
                     *, C):
    c = pl.program_id(1)
    nsub = C // GLA_SUB

    @pl.when(c == 0)
    def _():
        sT_ref[...] = jnp.zeros_like(sT_ref)

    tri = (lax.broadcasted_iota(jnp.int32, (C, C), 1)
           <= lax.broadcasted_iota(jnp.int32, (C, C), 0)).astype(F32)
    srow = lax.broadcasted_iota(jnp.int32, (GLA_SUB, 1), 0)

    for h in range(GLA_HEADS):
        kl = pl.ds(h * GLA_DK, GLA_DK)
        vl = pl.ds(h * GLA_DV, GLA_DV)
        b = jnp.dot(tri, la_ref[:, kl], precision=HIGHEST, preferred_element_type=F32)
        b_ref[h] = b
        qh = q_ref[:, kl] * (GLA_DK ** -0.5)
        kh = k_ref[:, kl]
        vh = v_ref[:, vl]
        b_end = b[C - 1:C, :]
        sT = sT_ref[h]
        oi_ref[h] = lax.dot_general((qh * jnp.exp(b)).astype(BF16), sT.astype(BF16), NT,
                                    preferred_element_type=F32)
        k_end = (kh * jnp.exp(b_end - b)).astype(BF16)
        sT_ref[h] = sT * jnp.exp(b_end) + lax.dot_general(vh.astype(BF16), k_end, TN,
                                                          preferred_element_type=F32)
        for I in range(1, nsub):
            r0 = I * GLA_SUB
            cI = b[r0 - 1:r0, :]
            qI = (qh[r0:r0 + GLA_SUB] * jnp.exp(b[r0:r0 + GLA_SUB] - cI)).astype(BF16)
            kI = (kh[:r0] * jnp.exp(cI - b[:r0])).astype(BF16)
            a = lax.dot_general(qI, kI, NT, preferred_element_type=F32)
            oi_ref[h, r0:r0 + GLA_SUB, :] += jnp.dot(a.astype(BF16), vh[:r0].astype(BF16),
                                                     preferred_element_type=F32)

    def diag(I, carry):
        rows = pl.ds(pl.multiple_of(I * GLA_SUB, GLA_SUB), GLA_SUB)
        for h in range(GLA_HEADS):
            bI = b_ref[h, rows, :]
            qI = q_ref[rows, pl.ds(h * GLA_DK, GLA_DK)] * (GLA_DK ** -0.5)
            kI = k_ref[rows, pl.ds(h * GLA_DK, GLA_DK)]
            vI = v_ref[rows, pl.ds(h * GLA_DV, GLA_DV)]
            acc = jnp.zeros((GLA_SUB, GLA_DV), F32)
            for s in range(GLA_SUB):
                e = jnp.exp(jnp.minimum(bI - bI[s:s + 1, :], 0.0))
                a_col = jnp.sum(qI * kI[s:s + 1, :] * e, axis=1, keepdims=True)
                a_col = jnp.where(srow >= s, a_col, 0.0)
                acc += a_col * vI[s:s + 1, :]
            oi_ref[h, rows, :] += acc
        return carry

    lax.fori_loop(0, nsub, diag, 0)

    for h in range(GLA_HEADS):
        vl = pl.ds(h * GLA_DV, GLA_DV)
        g = go_ref[:, vl]
        o_ref[:, vl] = (_rms(oi_ref[h], gn_ref[...]) * (g * jax.nn.sigmoid(g))).astype(o_ref.dtype)

    @pl.when(c == pl.num_programs(1) - 1)
    def _():
        for h in range(GLA_HEADS):
            st_ref[0, h] = sT_ref[h].T


def gla_prompt(z, log_a, gla_norm, B, S, C=128):
    assert S % C == 0 and C % GLA_SUB == 0
    nc = S // C
    q_blk = 3 * ATTN_WIDTH // GLA_KW
    v_blk = (3 * ATTN_WIDTH + 2 * GLA_KW) // GLA_WIDTH
    return pl.pallas_call(
        functools.partial(_gla_prompt_body, C=C),
        grid=(B, nc),
        in_specs=[
            pl.BlockSpec((C, GLA_KW), lambda b, c: (b * nc + c, q_blk)),
            pl.BlockSpec((C, GLA_KW), lambda b, c: (b * nc + c, q_blk + 1)),
            pl.BlockSpec((C, GLA_WIDTH), lambda b, c: (b * nc + c, v_blk)),
            pl.BlockSpec((C, GLA_WIDTH), lambda b, c: (b * nc + c, v_blk + 1)),
            pl.BlockSpec((C, GLA_KW), lambda b, c: (b * nc + c, 0)),
            pl.BlockSpec((1, GLA_DV), lambda b, c: (0, 0)),
        ],
        out_specs=[
            pl.BlockSpec((C, GLA_WIDTH), lambda b, c: (b * nc + c, 0)),
            pl.BlockSpec((1, GLA_HEADS, GLA_DK, GLA_DV), lambda b, c: (b, 0, 0, 0)),
        ],
        out_shape=[
            jax.ShapeDtypeStruct((B * S, GLA_WIDTH), BF16),
            jax.ShapeDtypeStruct((B, GLA_HEADS, GLA_DK, GLA_DV), F32),
        ],
        scratch_shapes=[
            pltpu.VMEM((GLA_HEADS, GLA_DV, GLA_DK), F32),
            pltpu.VMEM((GLA_HEADS, C, GLA_DK), F32),
            pltpu.VMEM((GLA_HEADS, C, GLA_DV), F32),
        ],
        compiler_params=_cparams("parallel", "arbitrary"),
        name="gla_prompt",
    )(z, z, z, z, log_a, gla_norm)


def _gla_sample_body(q_ref, k_ref, v_ref, go_ref, la_ref, gn_ref, s_ref, o_ref, so_ref, *, bb):
    for h in range(GLA_HEADS):
        kl = pl.ds(h * GLA_DK, GLA_DK)
        vl = pl.ds(h * GLA_DV, GLA_DV)
        aT = jnp.exp(la_ref[:, kl]).T
        kT = k_ref[:, kl].T
        qT = (q_ref[:, kl] * (GLA_DK ** -0.5)).T
        outs = []
        for b in range(bb):
            s_new = aT[:, b:b + 1] * s_ref[b, h] + kT[:, b:b + 1] * v_ref[b:b + 1, vl]
            so_ref[b, h] = s_new
            outs.append(jnp.sum(qT[:, b:b + 1] * s_new, axis=0, keepdims=True))
        o = jnp.concatenate(outs, axis=0)
        g = go_ref[:, vl]
        o_ref[:, vl] = (_rms(o, gn_ref[...]) * (g * jax.nn.sigmoid(g))).astype(o_ref.dtype)


def gla_sample(z, log_a, gla_norm, state, bb=SUBLANES):
    Bs = z.shape[0]
    assert Bs % bb == 0
    q_blk = 3 * ATTN_WIDTH // GLA_KW
    v_blk = (3 * ATTN_WIDTH + 2 * GLA_KW) // GLA_WIDTH
    st_spec = pl.BlockSpec((bb, GLA_HEADS, GLA_DK, GLA_DV), lambda i: (i, 0, 0, 0))
    return pl.pallas_call(
        functools.partial(_gla_sample_body, bb=bb),
        grid=(Bs // bb,),
        in_specs=[
            pl.BlockSpec((bb, GLA_KW), lambda i: (i, q_blk)),
            pl.BlockSpec((bb, GLA_KW), lambda i: (i, q_blk + 1)),
            pl.BlockSpec((bb, GLA_WIDTH), lambda i: (i, v_blk)),
            pl.BlockSpec((bb, GLA_WIDTH), lambda i: (i, v_blk + 1)),
            pl.BlockSpec((bb, GLA_KW), lambda i: (i, 0)),
            pl.BlockSpec((1, GLA_DV), lambda i: (0, 0)),
            st_spec,
        ],
        out_specs=[pl.BlockSpec((bb, GLA_WIDTH), lambda i: (i, 0)), st_spec],
        out_shape=[
            jax.ShapeDtypeStruct((Bs, GLA_WIDTH), BF16),
            jax.ShapeDtypeStruct(state.shape, F32),
        ],
        compiler_params=_cparams("parallel"),
        name="gla_sample",
    )(z, z, z, z, log_a, gla_norm, state)


def _proj_residual_body(a_ref, b_ref, wa_ref, wb_ref, res_ref, o_ref):
    acc = jnp.dot(a_ref[...], wa_ref[...], preferred_element_type=F32)
    acc += jnp.dot(b_ref[...], wb_ref[...], preferred_element_type=F32)
    o_ref[...] = res_ref[...] + acc


def proj_residual(a, b, wa, wb, res, tm, tn=512):
    T, Ka = a.shape
    Kb = b.shape[1]
    N = wa.shape[1]
    assert T % tm == 0 and N % tn == 0
    return pl.pallas_call(
        _proj_residual_body,
        grid=(T // tm, N // tn),
        in_specs=[
            pl.BlockSpec((tm, Ka), lambda i, j: (i, 0)),
            pl.BlockSpec((tm, Kb), lambda i, j: (i, 0)),
            pl.BlockSpec((Ka, tn), lambda i, j: (0, j)),
            pl.BlockSpec((Kb, tn), lambda i, j: (0, j)),
            pl.BlockSpec((tm, tn), lambda i, j: (i, j)),
        ],
        out_specs=pl.BlockSpec((tm, tn), lambda i, j: (i, j)),
        out_shape=jax.ShapeDtypeStruct((T, N), F32),
        compiler_params=_cparams("parallel", "arbitrary"),
        name="proj_residual",
    )(a, b, wa, wb, res)


def _peer_query_body(h_ref, g_ref, w_ref, q_ref, xT_ref, xn_ref):
    @pl.when(pl.program_id(1) == 0)
    def _():
        xn = _rms(h_ref[...], g_ref[...])
        xn_ref[...] = xn.astype(BF16)
        xT_ref[...] = xn.T.astype(BF16)

    q_ref[...] = jnp.dot(xn_ref[...], w_ref[...], preferred_element_type=F32)


def peer_query(h, g, w, tm, tn=512):
    T, D = h.shape
    N = w.shape[1]
    assert T % tm == 0 and N % tn == 0
    return pl.pallas_call(
        _peer_query_body,
        grid=(T // tm, N // tn),
        in_specs=[
            pl.BlockSpec((tm, D), lambda i, j: (i, 0)),
            pl.BlockSpec((1, D), lambda i, j: (0, 0)),
            pl.BlockSpec((D, tn), lambda i, j: (0, j)),
        ],
        out_specs=[
            pl.BlockSpec((tm, tn), lambda i, j: (i, j)),
            pl.BlockSpec((D, tm), lambda i, j: (0, i)),
        ],
        out_shape=[
            jax.ShapeDtypeStruct((T, N), F32),
            jax.ShapeDtypeStruct((D, T), BF16),
        ],
        scratch_shapes=[pltpu.VMEM((tm, D), BF16)],
        compiler_params=_cparams("parallel", "arbitrary"),
        name="peer_query",
    )(h, g, w)


def _peer_route_body(q_ref, k1_ref, k2_ref, s1_ref, s2_ref, thr_ref, *, tm):
    K = PEER_TOPK
    KR = -(-(K + 1) // SUBLANES) * SUBLANES
    rowk = lax.broadcasted_iota(jnp.int32, (KR, tm), 0)
    row8 = lax.broadcasted_iota(jnp.int32, (SUBLANES, tm), 0)
    rowh = lax.broadcasted_iota(jnp.int32, (PEER_HEADS, tm), 0)
    thr_all = jnp.zeros((PEER_HEADS, tm), F32)

    def top_values(s):
        vals = jnp.full((KR, tm), -jnp.inf, F32)
        for it in range(K + 1):
            mx = jnp.max(s, axis=0, keepdims=True)
            vals = jnp.where(rowk == it, mx, vals)
            s = jnp.where(s == mx, -jnp.inf, s)
        return vals

    for h in range(PEER_HEADS):
        q1 = q_ref[:, h * PEER_QDIM:h * PEER_QDIM + PEER_HALF]
        q2 = q_ref[:, h * PEER_QDIM + PEER_HALF:(h + 1) * PEER_QDIM]
        s1 = lax.dot_general(k1_ref[h], q1, NT, precision=HIGHEST, preferred_element_type=F32)
        s2 = lax.dot_general(k2_ref[h], q2, NT, precision=HIGHEST, preferred_element_type=F32)
        v1 = top_values(s1)
        v2 = top_values(s2)
        pieces = [v1[0:1] + v2[r:r + SUBLANES] for r in range(0, KR, SUBLANES)]
        for a in range(1, SUBLANES):
            pieces.append(jnp.where(row8 < (K + 1) // (a + 1), v1[a:a + 1] + v2[0:SUBLANES], -jnp.inf))
        pieces += [v1[r:r + SUBLANES] + v2[0:1] for r in range(SUBLANES, KR, SUBLANES)]
        cand = jnp.concatenate(pieces, axis=0)
        m = v1[0:1] + v2[0:1]
        z = jnp.zeros((1, tm), F32)
        kth = m
        for it in range(K):
            kth = jnp.max(cand, axis=0, keepdims=True)
            z = z + jnp.exp(kth - m)
            cand = jnp.where(cand == kth, -jnp.inf, cand)
        nxt = jnp.max(cand, axis=0, keepdims=True)
        c = m + jnp.log(z)
        s1_ref[h] = (s1 - c) * LOG2E
        s2_ref[h] = s2 * LOG2E
        thr_all = jnp.where(rowh == h, (0.5 * (kth + nxt) - c) * LOG2E, thr_all)
    thr_ref[...] = thr_all


def peer_route(q, keys1, keys2, tm):
    T = q.shape[0]
    assert T % tm == 0 and tm % LANES == 0 and (PEER_TOPK + 1) // 2 <= SUBLANES
    tile = pl.BlockSpec((PEER_HEADS, PEER_NKEYS, tm), lambda i: (0, 0, i))
    tile_shape = jax.ShapeDtypeStruct((PEER_HEADS, PEER_NKEYS, T), F32)
    kspec = pl.BlockSpec((PEER_HEADS, PEER_NKEYS, PEER_HALF), lambda i: (0, 0, 0))
    return pl.pallas_call(
        functools.partial(_peer_route_body, tm=tm),
        grid=(T // tm,),
        in_specs=[pl.BlockSpec((tm, PEER_HEADS * PEER_QDIM), lambda i: (i, 0)), kspec, kspec],
        out_specs=[tile, tile, pl.BlockSpec((PEER_HEADS, tm), lambda i: (0, i))],
        out_shape=[tile_shape, tile_shape, jax.ShapeDtypeStruct((PEER_HEADS, T), F32)],
        compiler_params=_cparams("parallel"),
        name="peer_route",
    )(q, keys1, keys2)


def _peer_dense_body(xT_ref, u_ref, v_ref, s1_ref, s2_ref, thr_ref, y_ref, *, te, tm, chunk):
    e = pl.program_id(1)

    @pl.when(e == 0)
    def _():
        y_ref[...] = jnp.zeros_like(y_ref)

    acc = None
    for k in range(te // chunk):
        gates = []
        for c in range(k * chunk // PEER_NKEYS, (k + 1) * chunk // PEER_NKEYS):
            tiles = []
            for tc in range(tm // LANES):
                lanes = pl.ds(tc * LANES, LANES)
                gate = jnp.zeros((PEER_NKEYS, LANES), F32)
                for h in range(PEER_HEADS):
                    ssum = s2_ref[h, :, lanes] + s1_ref[h, c:c + 1, lanes]
                    gate += jnp.where(ssum >= thr_ref[pl.ds(h, 1), lanes], jnp.exp2(ssum), 0.0)
                tiles.append(gate)
            gates.append(jnp.concatenate(tiles, axis=1))
        gate = jnp.concatenate(gates, axis=0)
        rows = pl.ds(k * chunk, chunk)
        hT = jnp.dot(u_ref[rows, :], xT_ref[...], preferred_element_type=F32)
        act = 0.5 * hT * (1.0 + lax.erf(hT * (2.0 ** -0.5))) * gate
        part = jnp.dot(act.T.astype(BF16), v_ref[rows, :], preferred_element_type=F32)
        acc = part if acc is None else acc + part
    y_ref[...] += acc


def peer_dense(xT, u, v, s1, s2, thr, tm, chunk=512):
    D, T = xT.shape
    E = u.shape[0]
    te = SUBLANES * PEER_NKEYS
    assert T % tm == 0 and E % te == 0 and te % chunk == 0 and chunk % PEER_NKEYS == 0 and tm % LANES == 0
    return pl.pallas_call(
        functools.partial(_peer_dense_body, te=te, tm=tm, chunk=chunk),
        grid=(T // tm, E // te),
        in_specs=[
            pl.BlockSpec((D, tm), lambda i, e: (0, i)),
            pl.BlockSpec((te, D), lambda i, e: (e, 0)),
            pl.BlockSpec((te, D), lambda i, e: (e, 0)),
            pl.BlockSpec((PEER_HEADS, SUBLANES, tm), lambda i, e: (0, e, i)),
            pl.BlockSpec((PEER_HEADS, PEER_NKEYS, tm), lambda i, e: (0, 0, i)),
            pl.BlockSpec((PEER_HEADS, tm), lambda i, e: (0, i)),
        ],
        out_specs=pl.BlockSpec((tm, D), lambda i, e: (i, 0)),
        out_shape=jax.ShapeDtypeStruct((T, D), F32),
        compiler_params=_cparams("parallel", "arbitrary"),
        name="peer_dense",
    )(xT, u, v, s1, s2, thr)


def _final_norm_body(h_ref, y_ref, g_ref, o_ref):
    o_ref[...] = _rms(h_ref[...] + y_ref[...], g_ref[...])


def final_norm(h, y, g, tm):
    T, D = h.shape
    return pl.pallas_call(
        _final_norm_body,
        grid=(T // tm,),
        in_specs=[
            pl.BlockSpec((tm, D), lambda i: (i, 0)),
            pl.BlockSpec((tm, D), lambda i: (i, 0)),
            pl.BlockSpec((1, D), lambda i: (0, 0)),
        ],
        out_specs=pl.BlockSpec((tm, D), lambda i: (i, 0)),
        out_shape=jax.ShapeDtypeStruct((T, D), F32),
        compiler_params=_cparams("parallel"),
        name="final_norm",
    )(h, y, g)


def _post_mix(x, o_attn, o_gla, w, tm):
    h = proj_residual(o_attn, o_gla, w["w_o_a"], w["w_o_b"], x, tm)
    qp, xT = peer_query(h, w["norm_ffn"], w["peer_query"], tm)
    s1, s2, thr = peer_route(qp, w["peer_keys_1"], w["peer_keys_2"], tm)
    y = peer_dense(xT, w["expert_u"], w["expert_v"], s1, s2, thr, tm)
    return final_norm(h, y, w["final_norm"], tm)


def _prompt_group(x_prompt, w):
    B, S, _ = x_prompt.shape
    x = x_prompt.reshape(B * S, D_MODEL)
    tm = 512
    z, log_a = in_proj(x, w, jnp.arange(S), tm)
    o_attn = attn_prompt(z, B, S)
    o_gla, state = gla_prompt(z, log_a, w["gla_norm"], B, S)
    out = _post_mix(x, o_attn, o_gla, w, tm)
    return out, z, state


def _sample_group(x_sample, cache_k, cache_v, state, w):
    Bs, Ls, _ = x_sample.shape
    assert Ls == 1
    x = x_sample.reshape(Bs, D_MODEL)
    tm = Bs
    z, log_a = in_proj(x, w, jnp.full((tm,), PAST_LEN, jnp.int32), tm)
    o_attn = attn_sample(z, cache_k, cache_v)
    o_gla, new_state = gla_sample(z, log_a, w["gla_norm"], state)
    out = _post_mix(x, o_attn, o_gla, w, tm)
    return out, z, new_state


def _prepare(norm_attn, w_in, w_gate_up, b_gate, gla_norm, w_o, norm_ffn, peer_query_w, peer_keys_1,
             peer_keys_2, expert_u, expert_v, final_norm_w):
    assert w_in.shape[0] == 1
    l = 0
    w_in_l = w_in[l]
    row = lambda v: v.reshape(1, -1).astype(F32)
    return {
        "norm_attn": row(norm_attn[l]),
        "w_in_main": w_in_l[:, :MAIN_COLS].astype(BF16),
        "w_in_lr": jnp.pad(w_in_l[:, MAIN_COLS:], ((0, 0), (0, LANES - GLA_LOWRANK))).astype(BF16),
        "w_gate_up": jnp.pad(w_gate_up[l], ((0, LANES - GLA_LOWRANK), (0, 0))),
        "b_gate": row(b_gate[l]),
        "gla_norm": row(gla_norm[l]),
        "w_o_a": w_o[l][:ATTN_WIDTH].astype(BF16),
        "w_o_b": w_o[l][ATTN_WIDTH:].astype(BF16),
        "norm_ffn": row(norm_ffn[l]),
        "peer_query": peer_query_w[l].astype(BF16),
        "peer_keys_1": peer_keys_1[l],
        "peer_keys_2": peer_keys_2[l],
        "expert_u": expert_u[l].astype(BF16),
        "expert_v": expert_v[l].astype(BF16),
        "final_norm": row(final_norm_w),
    }


def kernel(x_prompt, x_sample, cache_attn_k, cache_attn_v, state_gla, norm_attn, w_in, w_gate_up,
           b_gate, gla_norm, w_o, norm_ffn, peer_query, peer_keys_1, peer_keys_2, expert_u,
           expert_v, final_norm):
    Bp, Lp, _ = x_prompt.shape
    Bs, Ls, _ = x_sample.shape
    w = _prepare(norm_attn, w_in, w_gate_up, b_gate, gla_norm, w_o, norm_ffn, peer_query, peer_keys_1,
                 peer_keys_2, expert_u, expert_v, final_norm)
    win_p = min(WINDOW_MAX, Lp)
    yp, zp, sp = _prompt_group(x_prompt, w)
    ys, zs, ss = _sample_group(x_sample, cache_attn_k[0], cache_attn_v[0], state_gla[0], w)

    def heads(z, lo, B, L):
        return z[:, lo:lo + ATTN_WIDTH].reshape(B, L, ATTN_HEADS, HEAD_DIM)

    kp = heads(zp, ATTN_WIDTH, Bp, Lp)[:, Lp - win_p:]
    vp = heads(zp, 2 * ATTN_WIDTH, Bp, Lp)[:, Lp - win_p:]
    return (yp.reshape(Bp, Lp, D_MODEL), ys.reshape(Bs, Ls, D_MODEL),
            kp[None], vp[None], sp[None],
            heads(zs, ATTN_WIDTH, Bs, Ls)[None], heads(zs, 2 * ATTN_WIDTH, Bs, Ls)[None], ss[None])


def sample_group(inp):
    names = ("norm_attn", "w_in", "w_gate_up", "b_gate", "gla_norm", "w_o", "norm_ffn", "peer_query",
             "peer_keys_1", "peer_keys_2", "expert_u", "expert_v", "final_norm")
    w = _prepare(*[inp[n] for n in names])
    out, z, st = _sample_group(inp["x_sample"], inp["cache_attn_k"][0], inp["cache_attn_v"][0],
                               inp["state_gla"][0], w)
    Bs = out.shape[0]
    return (out, z[:, ATTN_WIDTH:2 * ATTN_WIDTH].reshape(Bs, 1, ATTN_HEADS, HEAD_DIM),
            z[:, 2 * ATTN_WIDTH:3 * ATTN_WIDTH].reshape(Bs, 1, ATTN_HEADS, HEAD_DIM), st)
```

```python
import functools
import math

import jax
import jax.numpy as jnp
import numpy as np
from jax import lax
from jax.experimental import pallas as pl
from jax.experimental.pallas import tpu as pltpu

F32 = jnp.float32
BF16 = jnp.bfloat16

D_MODEL = 2048
PAST_LEN = 2048
HEAD_DIM = 128
ATTN_HEADS = 8
ATTN_WIDTH = ATTN_HEADS * HEAD_DIM
DILATIONS = ((128, 1), (512, 4), (2048, 16))
WINDOW_MAX = 2048
QUERY_BLOCK = 128
ATTN_SCALE = HEAD_DIM ** -0.5
ROPE_THETA = 500000.0
ROT_DIM = HEAD_DIM // 4
ROT_HALF = ROT_DIM // 2
GLA_HEADS = 4
GLA_WIDTH = D_MODEL - ATTN_WIDTH
GLA_DV = GLA_WIDTH // GLA_HEADS
GLA_DK = GLA_DV // 2
GLA_KW = GLA_HEADS * GLA_DK
GLA_LOWRANK = 16
GLA_TAU = 16.0
GLA_SUB = 16
PEER_HEADS = 8
PEER_NKEYS = 128
PEER_EXPERTS = PEER_NKEYS * PEER_NKEYS
PEER_QDIM = 256
PEER_HALF = PEER_QDIM // 2
PEER_TOPK = 16
NORM_EPS = 1e-6
MAIN_COLS = 3 * ATTN_WIDTH + 2 * GLA_KW + 2 * GLA_WIDTH

LANES = 128
SUBLANES = 8
VMEM_LIMIT = 56 * 1024 * 1024
NEG = -1e30
LOG2E = 1.0 / math.log(2.0)
HIGHEST = lax.Precision.HIGHEST
NT = (((1,), (1,)), ((), ()))
TN = (((0,), (0,)), ((), ()))


def _cparams(*sem):
    return pltpu.CompilerParams(dimension_semantics=sem, vmem_limit_bytes=VMEM_LIMIT)


def _rms(x, g):
    return x * lax.rsqrt(jnp.mean(x * x, axis=-1, keepdims=True) + NORM_EPS) * g


def _in_proj_body(x_ref, g_ref, w_ref, wlr_ref, wup_ref, bg_ref, cos_ref, sin_ref,
                  z_ref, la_ref, xn_ref, *, tn):
    j = pl.program_id(1)

    @pl.when(j == 0)
    def _():
        xn = _rms(x_ref[...], g_ref[...]).astype(BF16)
        xn_ref[...] = xn
        lr = jnp.dot(xn, wlr_ref[...], preferred_element_type=F32)
        zg = jnp.dot(lr, wup_ref[...], precision=HIGHEST, preferred_element_type=F32) + bg_ref[...]
        la_ref[...] = (jnp.minimum(zg, 0.0) - jnp.log1p(jnp.exp(-jnp.abs(zg)))) * (1.0 / GLA_TAU)

    acc = jnp.dot(xn_ref[...], w_ref[...], preferred_element_type=F32)

    @pl.when(j < 2 * ATTN_WIDTH // tn)
    def _():
        lane = lax.broadcasted_iota(jnp.int32, (acc.shape[0], HEAD_DIM), 1)
        cos = cos_ref[...]
        sin = sin_ref[...]
        for c in range(tn // HEAD_DIM):
            a = acc[:, c * HEAD_DIM:(c + 1) * HEAD_DIM]
            partner = jnp.where(lane < ROT_HALF, pltpu.roll(a, HEAD_DIM - ROT_HALF, 1),
                                pltpu.roll(a, ROT_HALF, 1))
            z_ref[:, c * HEAD_DIM:(c + 1) * HEAD_DIM] = a * cos + partner * sin

    @pl.when(j >= 2 * ATTN_WIDTH // tn)
    def _():
        z_ref[...] = acc


def _rotary_tables(pos):
    inv = jnp.exp(-math.log(ROPE_THETA) * jnp.arange(ROT_HALF, dtype=F32) * (2.0 / ROT_DIM))
    ang = pos.astype(F32)[:, None] * inv[None, :]
    n = pos.shape[0]
    cos = jnp.concatenate([jnp.cos(ang), jnp.cos(ang), jnp.ones((n, HEAD_DIM - ROT_DIM), F32)], axis=1)
    sin = jnp.concatenate([-jnp.sin(ang), jnp.sin(ang), jnp.zeros((n, HEAD_DIM - ROT_DIM), F32)], axis=1)
    return cos, sin


def in_proj(x, w, pos, tm, tn=512):
    T, D = x.shape
    P = pos.shape[0]
    assert T % tm == 0 and MAIN_COLS % tn == 0 and P % tm == 0 and (2 * ATTN_WIDTH) % tn == 0
    cos, sin = _rotary_tables(pos)
    const = lambda i, j: (0, 0)
    return pl.pallas_call(
        functools.partial(_in_proj_body, tn=tn),
        grid=(T // tm, MAIN_COLS // tn),
        in_specs=[
            pl.BlockSpec((tm, D), lambda i, j: (i, 0)),
            pl.BlockSpec((1, D), const),
            pl.BlockSpec((D, tn), lambda i, j: (0, j)),
            pl.BlockSpec((D, LANES), const),
            pl.BlockSpec((LANES, GLA_KW), const),
            pl.BlockSpec((1, GLA_KW), const),
            pl.BlockSpec((tm, HEAD_DIM), lambda i, j: (i % (P // tm), 0)),
            pl.BlockSpec((tm, HEAD_DIM), lambda i, j: (i % (P // tm), 0)),
        ],
        out_specs=[
            pl.BlockSpec((tm, tn), lambda i, j: (i, j)),
            pl.BlockSpec((tm, GLA_KW), lambda i, j: (i, 0)),
        ],
        out_shape=[
            jax.ShapeDtypeStruct((T, MAIN_COLS), F32),
            jax.ShapeDtypeStruct((T, GLA_KW), F32),
        ],
        scratch_shapes=[pltpu.VMEM((tm, D), BF16)],
        compiler_params=_cparams("parallel", "arbitrary"),
        name="in_proj",
    )(x, w["norm_attn"], w["w_in_main"], w["w_in_lr"], w["w_gate_up"], w["b_gate"], cos, sin)


def _attn_prompt_body(q_ref, k_ref, v_ref, o_ref, ob_ref, lse_ref, *, S):
    QB = QUERY_BLOCK
    row = lax.broadcasted_iota(jnp.int32, (QB, QB), 0)
    col = lax.broadcasted_iota(jnp.int32, (QB, QB), 1)
    cur_ok = col <= row
    prev_ok = col >= row

    for bi, (window, d) in enumerate(DILATIONS):
        assert window == d * QB
        nblk = S // d // QB

        def body(idx, carry, d=d, nblk=nblk, bi=bi):
            r = idx // nblk
            i = idx % nblk
            start = r + i * (d * QB)
            pstart = jnp.maximum(start - d * QB, r)
            if d == 1:
                rows = pl.ds(pl.multiple_of(start, QB), QB)
                prows = pl.ds(pl.multiple_of(pstart, QB), QB)
            else:
                rows = pl.ds(start, QB, stride=d)
                prows = pl.ds(pstart, QB, stride=d)
            qs = (q_ref[rows, :] * ATTN_SCALE).astype(BF16)
            s_c = lax.dot_general(qs, k_ref[rows, :].astype(BF16), NT, preferred_element_type=F32)
            s_p = lax.dot_general(qs, k_ref[prows, :].astype(BF16), NT, preferred_element_type=F32)
            s_c = jnp.where(cur_ok, s_c, NEG)
            s_p = jnp.where(jnp.logical_and(prev_ok, i > 0), s_p, NEG)
            m = jnp.maximum(jnp.max(s_c, axis=1, keepdims=True), jnp.max(s_p, axis=1, keepdims=True))
            p_c = jnp.exp(s_c - m)
            p_p = jnp.exp(s_p - m)
            den = jnp.sum(p_c, axis=1, keepdims=True) + jnp.sum(p_p, axis=1, keepdims=True)
            o = jnp.dot(p_c.astype(BF16), v_ref[rows, :].astype(BF16), preferred_element_type=F32)
            o += jnp.dot(p_p.astype(BF16), v_ref[prows, :].astype(BF16), preferred_element_type=F32)
            ob_ref[bi, rows, :] = o / den
            lse_ref[bi, rows, :] = jnp.broadcast_to(m + jnp.log(den), (QB, HEAD_DIM))
            return carry

        lax.fori_loop(0, d * nblk, body, 0, unroll=4)

    CH = 256

    def combine(c, carry):
        rows = pl.ds(pl.multiple_of(c * CH, CH), CH)
        l0, l1, l2 = lse_ref[0, rows, :], lse_ref[1, rows, :], lse_ref[2, rows, :]
        mx = jnp.maximum(jnp.maximum(l0, l1), l2)
        w0, w1, w2 = jnp.exp(l0 - mx), jnp.exp(l1 - mx), jnp.exp(l2 - mx)
        num = w0 * ob_ref[0, rows, :] + w1 * ob_ref[1, rows, :] + w2 * ob_ref[2, rows, :]
        o_ref[rows, :] = (num / (w0 + w1 + w2)).astype(o_ref.dtype)
        return carry

    lax.fori_loop(0, S // CH, combine, 0)


def attn_prompt(z, B, S):
    H = ATTN_HEADS
    assert S % (DILATIONS[-1][1] * QUERY_BLOCK) == 0
    return pl.pallas_call(
        functools.partial(_attn_prompt_body, S=S),
        grid=(B, H),
        in_specs=[
            pl.BlockSpec((S, HEAD_DIM), lambda b, h: (b, h)),
            pl.BlockSpec((S, HEAD_DIM), lambda b, h: (b, H + h)),
            pl.BlockSpec((S, HEAD_DIM), lambda b, h: (b, 2 * H + h)),
        ],
        out_specs=pl.BlockSpec((S, HEAD_DIM), lambda b, h: (b, h)),
        out_shape=jax.ShapeDtypeStruct((B * S, ATTN_WIDTH), BF16),
        scratch_shapes=[pltpu.VMEM((3, S, HEAD_DIM), F32), pltpu.VMEM((3, S, HEAD_DIM), F32)],
        compiler_params=_cparams("parallel", "parallel"),
        name="attn_prompt",
    )(z, z, z)


def _attn_sample_body(z_ref, k1_ref, k4_ref, k16_ref, v1_ref, v4_ref, v16_ref, o_ref, *, bb):
    H = ATTN_HEADS
    for b in range(bb):
        q = z_ref[b, 0:H, :] * ATTN_SCALE
        k_new = z_ref[b, H:2 * H, :]
        v_new = z_ref[b, 2 * H:3 * H, :]
        s_new = jnp.sum(k_new * q, axis=-1, keepdims=True)
        s_win = [jnp.sum(kr[b] * q, axis=-1, keepdims=True)
                 for kr in (k1_ref, k4_ref, k16_ref)]
        m = s_new
        for s in s_win:
            m = jnp.maximum(m, jnp.max(s, axis=0))
        p_new = jnp.exp(s_new - m) * float(len(DILATIONS))
        den = p_new
        acc = p_new * v_new
        for s, vr in zip(s_win, (v1_ref, v4_ref, v16_ref)):
            p = jnp.exp(s - m)
            den = den + jnp.sum(p, axis=0)
            acc = acc + jnp.sum(p * vr[b], axis=0)
        o_ref[b] = acc / den


def attn_sample(z, cache_k, cache_v, bb=4):
    Bs = z.shape[0]
    wbuf = cache_k.shape[1]
    QB, H = QUERY_BLOCK, ATTN_HEADS
    assert wbuf == PAST_LEN == WINDOW_MAX and Bs % bb == 0

    def views(c):
        out, specs = [], []
        for window, d in DILATIONS:
            n = wbuf // d
            if d == 1:
                out.append(c)
                specs.append(pl.BlockSpec((bb, QB, H, HEAD_DIM), lambda i, n=n: (i, n // QB - 1, 0, 0)))
            else:
                out.append(c.reshape(Bs, n, d, H, HEAD_DIM))
                specs.append(pl.BlockSpec((bb, QB, None, H, HEAD_DIM),
                                          lambda i, n=n: (i, n // QB - 1, 0, 0, 0)))
        return out, specs

    kv, kspecs = views(cache_k)
    vv, vspecs = views(cache_v)
    z3 = z.reshape(Bs, MAIN_COLS // HEAD_DIM, HEAD_DIM)
    out = pl.pallas_call(
        functools.partial(_attn_sample_body, bb=bb),
        grid=(Bs // bb,),
        in_specs=[pl.BlockSpec((bb, MAIN_COLS // HEAD_DIM, HEAD_DIM), lambda i: (i, 0, 0))] + kspecs + vspecs,
        out_specs=pl.BlockSpec((bb, H, HEAD_DIM), lambda i: (i, 0, 0)),
        out_shape=jax.ShapeDtypeStruct((Bs, H, HEAD_DIM), F32),
        compiler_params=_cparams("parallel"),
        name="attn_sample",
    )(z3, *kv, *vv)
    return out.reshape(Bs, ATTN_WIDTH).astype(BF16)


def _gla_prompt_body(q_ref, k_ref, v_ref, go_ref, la_ref, gn_ref, o_ref, st_ref, sT_ref, b_ref, oi_ref,
                     *, C):
    c = pl.program_id(1)
    nsub = C // GLA_SUB

    @pl.when(c == 0)
    def _():
        sT_ref[...] = jnp.zeros_like(sT_ref)

    tri = (lax.broadcasted_iota(jnp.int32, (C, C), 1)
           <= lax.broadcasted_iota(jnp.int32, (C, C), 0)).astype(F32)
    srow = lax.broadcasted_iota(jnp.int32, (GLA_SUB, 1), 0)

    for h in range(GLA_HEADS):
        kl = pl.ds(h * GLA_DK, GLA_DK)
        vl = pl.ds(h * GLA_DV, GLA_DV)
        b = jnp.dot(tri, la_ref[:, kl], precision=HIGHEST, preferred_element_type=F32)
        b_ref[h] = b
        qh = q_ref[:, kl] * (GLA_DK ** -0.5)
        kh = k_ref[:, kl]
        vh = v_ref[:, vl]
        b_end = b[C - 1:C, :]
        sT = sT_ref[h]
        oi_ref[h] = lax.dot_general((qh * jnp.exp(b)).astype(BF16), sT.astype(BF16), NT,
                                    preferred_element_type=F32)
        k_end = (kh * jnp.exp(b_end - b)).astype(BF16)
        sT_ref[h] = sT * jnp.exp(b_end) + lax.dot_general(vh.astype(BF16), k_end, TN,
                                                          preferred_element_type=F32)
        for I in range(1, nsub):
            r0 = I * GLA_SUB
            cI = b[r0 - 1:r0, :]
            qI = (qh[r0:r0 + GLA_SUB] * jnp.exp(b[r0:r0 + GLA_SUB] - cI)).astype(BF16)
            kI = (kh[:r0] * jnp.exp(cI - b[:r0])).astype(BF16)
            a = lax.dot_general(qI, kI, NT, preferred_element_type=F32)
            oi_ref[h, r0:r0 + GLA_SUB, :] += jnp.dot(a.astype(BF16), vh[:r0].astype(BF16),
                                                     preferred_element_type=F32)

    def diag(I, carry):
        rows = pl.ds(pl.multiple_of(I * GLA_SUB, GLA_SUB), GLA_SUB)
        for h in range(GLA_HEADS):
            bI = b_ref[h, rows, :]
            qI = q_ref[rows, pl.ds(h * GLA_DK, GLA_DK)] * (GLA_DK ** -0.5)
            kI = k_ref[rows, pl.ds(h * GLA_DK, GLA_DK)]
            vI = v_ref[rows, pl.ds(h * GLA_DV, GLA_DV)]
            acc = jnp.zeros((GLA_SUB, GLA_DV), F32)
            for s in range(GLA_SUB):
                e = jnp.exp(jnp.minimum(bI - bI[s:s + 1, :], 0.0))
                a_col = jnp.sum(qI * kI[s:s + 1, :] * e, axis=1, keepdims=True)
                a_col = jnp.where(srow >= s, a_col, 0.0)
                acc += a_col * vI[s:s + 1, :]
            oi_ref[h, rows, :] += acc
        return carry

    lax.fori_loop(0, nsub, diag, 0)

    for h in range(GLA_HEADS):
        vl = pl.ds(h * GLA_DV, GLA_DV)
        g = go_ref[:, vl]
        o_ref[:, vl] = (_rms(oi_ref[h], gn_ref[...]) * (g * jax.nn.sigmoid(g))).astype(o_ref.dtype)

    @pl.when(c == pl.num_programs(1) - 1)
    def _():
        for h in range(GLA_HEADS):
            st_ref[0, h] = sT_ref[h].T


def gla_prompt(z, log_a, gla_norm, B, S, C=128):
    assert S % C == 0 and C % GLA_SUB == 0
    nc = S // C
    q_blk = 3 * ATTN_WIDTH // GLA_KW
    v_blk = (3 * ATTN_WIDTH + 2 * GLA_KW) // GLA_WIDTH
    return pl.pallas_call(
        functools.partial(_gla_prompt_body, C=C),
        grid=(B, nc),
        in_specs=[
            pl.BlockSpec((C, GLA_KW), lambda b, c: (b * nc + c, q_blk)),
            pl.BlockSpec((C, GLA_KW), lambda b, c: (b * nc + c, q_blk + 1)),
            pl.BlockSpec((C, GLA_WIDTH), lambda b, c: (b * nc + c, v_blk)),
            pl.BlockSpec((C, GLA_WIDTH), lambda b, c: (b * nc + c, v_blk + 1)),
            pl.BlockSpec((C, GLA_KW), lambda b, c: (b * nc + c, 0)),
            pl.BlockSpec((1, GLA_DV), lambda b, c: (0, 0)),
        ],
        out_specs=[
            pl.BlockSpec((C, GLA_WIDTH), lambda b, c: (b * nc + c, 0)),
            pl.BlockSpec((1, GLA_HEADS, GLA_DK, GLA_DV), lambda b, c: (b, 0, 0, 0)),
        ],
        out_shape=[
            jax.ShapeDtypeStruct((B * S, GLA_WIDTH), BF16),
            jax.ShapeDtypeStruct((B, GLA_HEADS, GLA_DK, GLA_DV), F32),
        ],
        scratch_shapes=[
            pltpu.VMEM((GLA_HEADS, GLA_DV, GLA_DK), F32),
            pltpu.VMEM((GLA_HEADS, C, GLA_DK), F32),
            pltpu.VMEM((GLA_HEADS, C, GLA_DV), F32),
        ],
        compiler_params=_cparams("parallel", "arbitrary"),
        name="gla_prompt",
    )(z, z, z, z, log_a, gla_norm)


def _gla_sample_body(q_ref, k_ref, v_ref, go_ref, la_ref, gn_ref, s_ref, o_ref, so_ref, *, bb):
    for h in range(GLA_HEADS):
        kl = pl.ds(h * GLA_DK, GLA_DK)
        vl = pl.ds(h * GLA_DV, GLA_DV)
        aT = jnp.exp(la_ref[:, kl]).T
        kT = k_ref[:, kl].T
        qT = (q_ref[:, kl] * (GLA_DK ** -0.5)).T
        outs = []
        for b in range(bb):
            s_new = aT[:, b:b + 1] * s_ref[b, h] + kT[:, b:b + 1] * v_ref[b:b + 1, vl]
            so_ref[b, h] = s_new
            outs.append(jnp.sum(qT[:, b:b + 1] * s_new, axis=0, keepdims=True))
        o = jnp.concatenate(outs, axis=0)
        g = go_ref[:, vl]
        o_ref[:, vl] = (_rms(o, gn_ref[...]) * (g * jax.nn.sigmoid(g))).astype(o_ref.dtype)


def gla_sample(z, log_a, gla_norm, state, bb=SUBLANES):
    Bs = z.shape[0]
    assert Bs % bb == 0
    q_blk = 3 * ATTN_WIDTH // GLA_KW
    v_blk = (3 * ATTN_WIDTH + 2 * GLA_KW) // GLA_WIDTH
    st_spec = pl.BlockSpec((bb, GLA_HEADS, GLA_DK, GLA_DV), lambda i: (i, 0, 0, 0))
    return pl.pallas_call(
        functools.partial(_gla_sample_body, bb=bb),
        grid=(Bs // bb,),
        in_specs=[
            pl.BlockSpec((bb, GLA_KW), lambda i: (i, q_blk)),
            pl.BlockSpec((bb, GLA_KW), lambda i: (i, q_blk + 1)),
            pl.BlockSpec((bb, GLA_WIDTH), lambda i: (i, v_blk)),
            pl.BlockSpec((bb, GLA_WIDTH), lambda i: (i, v_blk + 1)),
            pl.BlockSpec((bb, GLA_KW), lambda i: (i, 0)),
            pl.BlockSpec((1, GLA_DV), lambda i: (0, 0)),
            st_spec,
        ],
        out_specs=[pl.BlockSpec((bb, GLA_WIDTH), lambda i: (i, 0)), st_spec],
        out_shape=[
            jax.ShapeDtypeStruct((Bs, GLA_WIDTH), BF16),
            jax.ShapeDtypeStruct(state.shape, F32),
        ],
        compiler_params=_cparams("parallel"),
        name="gla_sample",
    )(z, z, z, z, log_a, gla_norm, state)


def _proj_residual_body(a_ref, b_ref, wa_ref, wb_ref, res_ref, o_ref):
    acc = jnp.dot(a_ref[...], wa_ref[...], preferred_element_type=F32)
    acc += jnp.dot(b_ref[...], wb_ref[...], preferred_element_type=F32)
    o_ref[...] = res_ref[...] + acc


def proj_residual(a, b, wa, wb, res, tm, tn=512):
    T, Ka = a.shape
    Kb = b.shape[1]
    N = wa.shape[1]
    assert T % tm == 0 and N % tn == 0
    return pl.pallas_call(
        _proj_residual_body,
        grid=(T // tm, N // tn),
        in_specs=[
            pl.BlockSpec((tm, Ka), lambda i, j: (i, 0)),
            pl.BlockSpec((tm, Kb), lambda i, j: (i, 0)),
            pl.BlockSpec((Ka, tn), lambda i, j: (0, j)),
            pl.BlockSpec((Kb, tn), lambda i, j: (0, j)),
            pl.BlockSpec((tm, tn), lambda i, j: (i, j)),
        ],
        out_specs=pl.BlockSpec((tm, tn), lambda i, j: (i, j)),
        out_shape=jax.ShapeDtypeStruct((T, N), F32),
        compiler_params=_cparams("parallel", "arbitrary"),
        name="proj_residual",
    )(a, b, wa, wb, res)


def _peer_query_body(h_ref, g_ref, w_ref, q_ref, xT_ref, xn_ref):
    @pl.when(pl.program_id(1) == 0)
    def _():
        xn = _rms(h_ref[...], g_ref[...])
        xn_ref[...] = xn.astype(BF16)
        xT_ref[...] = xn.T.astype(BF16)

    q_ref[...] = jnp.dot(xn_ref[...], w_ref[...], preferred_element_type=F32)


def peer_query(h, g, w, tm, tn=512):
    T, D = h.shape
    N = w.shape[1]
    assert T % tm == 0 and N % tn == 0
    return pl.pallas_call(
        _peer_query_body,
        grid=(T // tm, N // tn),
        in_specs=[
            pl.BlockSpec((tm, D), lambda i, j: (i, 0)),
            pl.BlockSpec((1, D), lambda i, j: (0, 0)),
            pl.BlockSpec((D, tn), lambda i, j: (0, j)),
        ],
        out_specs=[
            pl.BlockSpec((tm, tn), lambda i, j: (i, j)),
            pl.BlockSpec((D, tm), lambda i, j: (0, i)),
        ],
        out_shape=[
            jax.ShapeDtypeStruct((T, N), F32),
            jax.ShapeDtypeStruct((D, T), BF16),
        ],
        scratch_shapes=[pltpu.VMEM((tm, D), BF16)],
        compiler_params=_cparams("parallel", "arbitrary"),
        name="peer_query",
    )(h, g, w)


def _peer_route_body(q_ref, k1_ref, k2_ref, s1_ref, s2_ref, thr_ref, *, tm):
    K = PEER_TOPK
    KR = -(-(K + 1) // SUBLANES) * SUBLANES
    rowk = lax.broadcasted_iota(jnp.int32, (KR, tm), 0)
    row8 = lax.broadcasted_iota(jnp.int32, (SUBLANES, tm), 0)
    rowh = lax.broadcasted_iota(jnp.int32, (PEER_HEADS, tm), 0)
    thr_all = jnp.zeros((PEER_HEADS, tm), F32)

    def top_values(s):
        vals = jnp.full((KR, tm), -jnp.inf, F32)
        for it in range(K + 1):
            mx = jnp.max(s, axis=0, keepdims=True)
            vals = jnp.where(rowk == it, mx, vals)
            s = jnp.where(s == mx, -jnp.inf, s)
        return vals

    for h in range(PEER_HEADS):
        q1 = q_ref[:, h * PEER_QDIM:h * PEER_QDIM + PEER_HALF]
        q2 = q_ref[:, h * PEER_QDIM + PEER_HALF:(h + 1) * PEER_QDIM]
        s1 = lax.dot_general(k1_ref[h], q1, NT, precision=HIGHEST, preferred_element_type=F32)
        s2 = lax.dot_general(k2_ref[h], q2, NT, precision=HIGHEST, preferred_element_type=F32)
        v1 = top_values(s1)
        v2 = top_values(s2)
        pieces = [v1[0:1] + v2[r:r + SUBLANES] for r in range(0, KR, SUBLANES)]
        for a in range(1, SUBLANES):
            pieces.append(jnp.where(row8 < (K + 1) // (a + 1), v1[a:a + 1] + v2[0:SUBLANES], -jnp.inf))
        pieces += [v1[r:r + SUBLANES] + v2[0:1] for r in range(SUBLANES, KR, SUBLANES)]
        cand = jnp.concatenate(pieces, axis=0)
        m = v1[0:1] + v2[0:1]
        z = jnp.zeros((1, tm), F32)
        kth = m
        for it in range(K):
            kth = jnp.max(cand, axis=0, keepdims=True)
            z = z + jnp.exp(kth - m)
            cand = jnp.where(cand == kth, -jnp.inf, cand)
        nxt = jnp.max(cand, axis=0, keepdims=True)
        c = m + jnp.log(z)
        s1_ref[h] = (s1 - c) * LOG2E - 1.0
        s2_ref[h] = s2 * LOG2E
        thr_all = jnp.where(rowh == h, (0.5 * (kth + nxt) - c) * LOG2E - 1.0, thr_all)
    thr_ref[...] = thr_all


def peer_route(q, keys1, keys2, tm):
    T = q.shape[0]
    assert T % tm == 0 and tm % LANES == 0 and (PEER_TOPK + 1) // 2 <= SUBLANES
    tile = pl.BlockSpec((PEER_HEADS, PEER_NKEYS, tm), lambda i: (0, 0, i))
    tile_shape = jax.ShapeDtypeStruct((PEER_HEADS, PEER_NKEYS, T), F32)
    kspec = pl.BlockSpec((PEER_HEADS, PEER_NKEYS, PEER_HALF), lambda i: (0, 0, 0))
    return pl.pallas_call(
        functools.partial(_peer_route_body, tm=tm),
        grid=(T // tm,),
        in_specs=[pl.BlockSpec((tm, PEER_HEADS * PEER_QDIM), lambda i: (i, 0)), kspec, kspec],
        out_specs=[tile, tile, pl.BlockSpec((PEER_HEADS, tm), lambda i: (0, i))],
        out_shape=[tile_shape, tile_shape, jax.ShapeDtypeStruct((PEER_HEADS, T), F32)],
        compiler_params=_cparams("parallel"),
        name="peer_route",
    )(q, keys1, keys2)


def _peer_dense_body(xT_ref, u_ref, v_ref, s1_ref, s2_ref, thr_ref, y_ref, *, te, tm):
    e = pl.program_id(1)

    @pl.when(e == 0)
    def _():
        y_ref[...] = jnp.zeros_like(y_ref)

    rows = []
    for c in range(te // PEER_NKEYS):
        tiles = []
        for tc in range(tm // LANES):
            lanes = pl.ds(tc * LANES, LANES)
            gate = jnp.zeros((PEER_NKEYS, LANES), F32)
            for h in range(PEER_HEADS):
                ssum = s2_ref[h, :, lanes] + s1_ref[h, c:c + 1, lanes]
                gate += jnp.where(ssum >= thr_ref[pl.ds(h, 1), lanes], jnp.exp2(ssum), 0.0)
            tiles.append(gate)
        rows.append(jnp.concatenate(tiles, axis=1))
    half_gate = jnp.concatenate(rows, axis=0)
    hT = jnp.dot(u_ref[...], xT_ref[...], preferred_element_type=F32)
    act = (hT + hT * lax.erf(hT * (2.0 ** -0.5))) * half_gate
    y_ref[...] += jnp.dot(act.T.astype(BF16), v_ref[...], preferred_element_type=F32)


def peer_dense(xT, u, v, s1, s2, thr, tm):
    D, T = xT.shape
    E = u.shape[0]
    te = SUBLANES * PEER_NKEYS
    assert T % tm == 0 and E % te == 0 and tm % LANES == 0
    return pl.pallas_call(
        functools.partial(_peer_dense_body, te=te, tm=tm),
        grid=(T // tm, E // te),
        in_specs=[
            pl.BlockSpec((D, tm), lambda i, e: (0, i)),
            pl.BlockSpec((te, D), lambda i, e: (e, 0)),
            pl.BlockSpec((te, D), lambda i, e: (e, 0)),
            pl.BlockSpec((PEER_HEADS, SUBLANES, tm), lambda i, e: (0, e, i)),
            pl.BlockSpec((PEER_HEADS, PEER_NKEYS, tm), lambda i, e: (0, 0, i)),
            pl.BlockSpec((PEER_HEADS, tm), lambda i, e: (0, i)),
        ],
        out_specs=pl.BlockSpec((tm, D), lambda i, e: (i, 0)),
        out_shape=jax.ShapeDtypeStruct((T, D), F32),
        compiler_params=_cparams("parallel", "arbitrary"),
        name="peer_dense",
    )(xT, u, v, s1, s2, thr)


def _final_norm_body(h_ref, y_ref, g_ref, o_ref):
    o_ref[...] = _rms(h_ref[...] + y_ref[...], g_ref[...])


def final_norm(h, y, g, tm):
    T, D = h.shape
    return pl.pallas_call(
        _final_norm_body,
        grid=(T // tm,),
        in_specs=[
            pl.BlockSpec((tm, D), lambda i: (i, 0)),
            pl.BlockSpec((tm, D), lambda i: (i, 0)),
            pl.BlockSpec((1, D), lambda i: (0, 0)),
        ],
        out_specs=pl.BlockSpec((tm, D), lambda i: (i, 0)),
        out_shape=jax.ShapeDtypeStruct((T, D), F32),
        compiler_params=_cparams("parallel"),
        name="final_norm",
    )(h, y, g)


def _post_mix(x, o_attn, o_gla, w, tm, tm_proj):
    h = proj_residual(o_attn, o_gla, w["w_o_a"], w["w_o_b"], x, tm_proj)
    qp, xT = peer_query(h, w["norm_ffn"], w["peer_query"], tm)
    s1, s2, thr = peer_route(qp, w["peer_keys_1"], w["peer_keys_2"], tm)
    y = peer_dense(xT, w["expert_u"], w["expert_v"], s1, s2, thr, tm)
    return final_norm(h, y, w["final_norm"], tm)


def _prompt_group(x_prompt, w):
    B, S, _ = x_prompt.shape
    x = x_prompt.reshape(B * S, D_MODEL)
    tm, tm_proj = 512, 1024
    z, log_a = in_proj(x, w, jnp.arange(S), tm_proj)
    o_attn = attn_prompt(z, B, S)
    o_gla, state = gla_prompt(z, log_a, w["gla_norm"], B, S)
    out = _post_mix(x, o_attn, o_gla, w, tm, tm_proj)
    return out, z, state


def _sample_group(x_sample, cache_k, cache_v, state, w):
    Bs, Ls, _ = x_sample.shape
    assert Ls == 1
    x = x_sample.reshape(Bs, D_MODEL)
    tm = Bs
    z, log_a = in_proj(x, w, jnp.full((tm,), PAST_LEN, jnp.int32), tm)
    o_attn = attn_sample(z, cache_k, cache_v)
    o_gla, new_state = gla_sample(z, log_a, w["gla_norm"], state)
    out = _post_mix(x, o_attn, o_gla, w, tm, tm)
    return out, z, new_state


def _prepare(norm_attn, w_in, w_gate_up, b_gate, gla_norm, w_o, norm_ffn, peer_query_w, peer_keys_1,
             peer_keys_2, expert_u, expert_v, final_norm_w):
    assert w_in.shape[0] == 1
    l = 0
    w_in_l = w_in[l]
    row = lambda v: v.reshape(1, -1).astype(F32)
    return {
        "norm_attn": row(norm_attn[l]),
        "w_in_main": w_in_l[:, :MAIN_COLS].astype(BF16),
        "w_in_lr": jnp.pad(w_in_l[:, MAIN_COLS:], ((0, 0), (0, LANES - GLA_LOWRANK))).astype(BF16),
        "w_gate_up": jnp.pad(w_gate_up[l], ((0, LANES - GLA_LOWRANK), (0, 0))),
        "b_gate": row(b_gate[l]),
        "gla_norm": row(gla_norm[l]),
        "w_o_a": w_o[l][:ATTN_WIDTH].astype(BF16),
        "w_o_b": w_o[l][ATTN_WIDTH:].astype(BF16),
        "norm_ffn": row(norm_ffn[l]),
        "peer_query": peer_query_w[l].astype(BF16),
        "peer_keys_1": peer_keys_1[l],
        "peer_keys_2": peer_keys_2[l],
        "expert_u": expert_u[l].astype(BF16),
        "expert_v": expert_v[l].astype(BF16),
        "final_norm": row(final_norm_w),
    }


def kernel(x_prompt, x_sample, cache_attn_k, cache_attn_v, state_gla, norm_attn, w_in, w_gate_up,
           b_gate, gla_norm, w_o, norm_ffn, peer_query, peer_keys_1, peer_keys_2, expert_u,
           expert_v, final_norm):
    Bp, Lp, _ = x_prompt.shape
    Bs, Ls, _ = x_sample.shape
    w = _prepare(norm_attn, w_in, w_gate_up, b_gate, gla_norm, w_o, norm_ffn, peer_query, peer_keys_1,
                 peer_keys_2, expert_u, expert_v, final_norm)
    win_p = min(WINDOW_MAX, Lp)
    yp, zp, sp = _prompt_group(x_prompt, w)
    ys, zs, ss = _sample_group(x_sample, cache_attn_k[0], cache_attn_v[0], state_gla[0], w)

    def heads(z, lo, B, L):
        return z[:, lo:lo + ATTN_WIDTH].reshape(B, L, ATTN_HEADS, HEAD_DIM)

    kp = heads(zp, ATTN_WIDTH, Bp, Lp)[:, Lp - win_p:]
    vp = heads(zp, 2 * ATTN_WIDTH, Bp, Lp)[:, Lp - win_p:]
    return (yp.reshape(Bp, Lp, D_MODEL), ys.reshape(Bs, Ls, D_MODEL),
            kp[None], vp[None], sp[None],
            heads(zs, ATTN_WIDTH, Bs, Ls)[None], heads(zs, 2 * ATTN_WIDTH, Bs, Ls)[None], ss[None])


def sample_group(inp):
    names = ("norm_attn", "w_in", "w_gate_up", "b_gate", "gla_norm", "w_o", "norm_ffn", "peer_query",
             "peer_keys_1", "peer_keys_2", "expert_u", "expert_v", "final_norm")
    w = _prepare(*[inp[n] for n in names])
    out, z, st = _sample_group(inp["x_sample"], inp["cache_attn_k"][0], inp["cache_attn_v"][0],
                               inp["state_gla"][0], w)
    Bs = out.shape[0]
    return (out, z[:, ATTN_WIDTH:2 * ATTN_WIDTH].reshape(Bs, 1, ATTN_HEADS, HEAD_DIM),
            z[:, 2 * ATTN_WIDTH:3 * ATTN_WIDTH].reshape(Bs, 1, ATTN_HEADS, HEAD_DIM), st)
```

```python
import functools
import math

import jax
import jax.numpy as jnp
import numpy as np
from jax import lax
from jax.experimental import pallas as pl
from jax.experimental.pallas import tpu as pltpu

F32 = jnp.float32
BF16 = jnp.bfloat16

D_MODEL = 2048
PAST_LEN = 2048
HEAD_DIM = 128
ATTN_HEADS = 8
ATTN_WIDTH = ATTN_HEADS * HEAD_DIM
DILATIONS = ((128, 1), (512, 4), (2048, 16))
WINDOW_MAX = 2048
QUERY_BLOCK = 128
ATTN_SCALE = HEAD_DIM ** -0.5
ROPE_THETA = 500000.0
ROT_DIM = HEAD_DIM // 4
ROT_HALF = ROT_DIM // 2
GLA_HEADS = 4
GLA_WIDTH = D_MODEL - ATTN_WIDTH
GLA_DV = GLA_WIDTH // GLA_HEADS
GLA_DK = GLA_DV // 2
GLA_KW = GLA_HEADS * GLA_DK
GLA_LOWRANK = 16
GLA_TAU = 16.0
GLA_SUB = 16
PEER_HEADS = 8
PEER_NKEYS = 128
PEER_EXPERTS = PEER_NKEYS * PEER_NKEYS
PEER_QDIM = 256
PEER_HALF = PEER_QDIM // 2
PEER_TOPK = 16
NORM_EPS = 1e-6
MAIN_COLS = 3 * ATTN_WIDTH + 2 * GLA_KW + 2 * GLA_WIDTH

LANES = 128
SUBLANES = 8
VMEM_LIMIT = 56 * 1024 * 1024
NEG = -1e30
LOG2E = 1.0 / math.log(2.0)
HIGHEST = lax.Precision.HIGHEST
NT = (((1,), (1,)), ((), ()))
TN = (((0,), (0,)), ((), ()))


def _cparams(*sem):
    return pltpu.CompilerParams(dimension_semantics=sem, vmem_limit_bytes=VMEM_LIMIT)


def _rms(x, g):
    return x * lax.rsqrt(jnp.mean(x * x, axis=-1, keepdims=True) + NORM_EPS) * g


def _in_proj_body(x_ref, g_ref, w_ref, wlr_ref, wup_ref, bg_ref, cos_ref, sin_ref,
                  z_ref, la_ref, xn_ref, *, tn):
    j = pl.program_id(1)

    @pl.when(j == 0)
    def _():
        xn = _rms(x_ref[...], g_ref[...]).astype(BF16)
        xn_ref[...] = xn
        lr = jnp.dot(xn, wlr_ref[...], preferred_element_type=F32)
        zg = jnp.dot(lr, wup_ref[...], precision=HIGHEST, preferred_element_type=F32) + bg_ref[...]
        la_ref[...] = (jnp.minimum(zg, 0.0) - jnp.log1p(jnp.exp(-jnp.abs(zg)))) * (1.0 / GLA_TAU)

    acc = jnp.dot(xn_ref[...], w_ref[...], preferred_element_type=F32)

    @pl.when(j < 2 * ATTN_WIDTH // tn)
    def _():
        lane = lax.broadcasted_iota(jnp.int32, (acc.shape[0], HEAD_DIM), 1)
        cos = cos_ref[...]
        sin = sin_ref[...]
        for c in range(tn // HEAD_DIM):
            a = acc[:, c * HEAD_DIM:(c + 1) * HEAD_DIM]
            partner = jnp.where(lane < ROT_HALF, pltpu.roll(a, HEAD_DIM - ROT_HALF, 1),
                                pltpu.roll(a, ROT_HALF, 1))
            z_ref[:, c * HEAD_DIM:(c + 1) * HEAD_DIM] = a * cos + partner * sin

    @pl.when(j >= 2 * ATTN_WIDTH // tn)
    def _():
        z_ref[...] = acc


def _rotary_tables(pos):
    inv = jnp.exp(-math.log(ROPE_THETA) * jnp.arange(ROT_HALF, dtype=F32) * (2.0 / ROT_DIM))
    ang = pos.astype(F32)[:, None] * inv[None, :]
    n = pos.shape[0]
    cos = jnp.concatenate([jnp.cos(ang), jnp.cos(ang), jnp.ones((n, HEAD_DIM - ROT_DIM), F32)], axis=1)
    sin = jnp.concatenate([-jnp.sin(ang), jnp.sin(ang), jnp.zeros((n, HEAD_DIM - ROT_DIM), F32)], axis=1)
    return cos, sin


def in_proj(x, w, pos, tm, tn=512):
    T, D = x.shape
    P = pos.shape[0]
    assert T % tm == 0 and MAIN_COLS % tn == 0 and P % tm == 0 and (2 * ATTN_WIDTH) % tn == 0
    cos, sin = _rotary_tables(pos)
    const = lambda i, j: (0, 0)
    return pl.pallas_call(
        functools.partial(_in_proj_body, tn=tn),
        grid=(T // tm, MAIN_COLS // tn),
        in_specs=[
            pl.BlockSpec((tm, D), lambda i, j: (i, 0)),
            pl.BlockSpec((1, D), const),
            pl.BlockSpec((D, tn), lambda i, j: (0, j)),
            pl.BlockSpec((D, LANES), const),
            pl.BlockSpec((LANES, GLA_KW), const),
            pl.BlockSpec((1, GLA_KW), const),
            pl.BlockSpec((tm, HEAD_DIM), lambda i, j: (i % (P // tm), 0)),
            pl.BlockSpec((tm, HEAD_DIM), lambda i, j: (i % (P // tm), 0)),
        ],
        out_specs=[
            pl.BlockSpec((tm, tn), lambda i, j: (i, j)),
            pl.BlockSpec((tm, GLA_KW), lambda i, j: (i, 0)),
        ],
        out_shape=[
            jax.ShapeDtypeStruct((T, MAIN_COLS), F32),
            jax.ShapeDtypeStruct((T, GLA_KW), F32),
        ],
        scratch_shapes=[pltpu.VMEM((tm, D), BF16)],
        compiler_params=_cparams("parallel", "arbitrary"),
        name="in_proj",
    )(x, w["norm_attn"], w["w_in_main"], w["w_in_lr"], w["w_gate_up"], w["b_gate"], cos, sin)


def _attn_prompt_body(q_ref, k_ref, v_ref, o_ref, ob_ref, lse_ref, s_ref, p_ref, m_ref, *, S):
    QB = QUERY_BLOCK
    row = lax.broadcasted_iota(jnp.int32, (QB, QB), 0)
    col = lax.broadcasted_iota(jnp.int32, (QB, QB), 1)
    bias = jnp.concatenate([jnp.where(col <= row, 0.0, NEG), jnp.where(col >= row, 0.0, NEG)], axis=1)
    lane2 = lax.broadcasted_iota(jnp.int32, (QB, 2 * QB), 1)
    ones = jnp.ones((QB, HEAD_DIM), BF16)

    for bi, (window, d) in enumerate(DILATIONS):
        assert window == d * QB
        nblk = S // d // QB
        assert d * nblk == s_ref.shape[0]

        def windows(idx, d=d, nblk=nblk):
            r = idx // nblk
            i = idx % nblk
            start = r + i * (d * QB)
            pstart = jnp.maximum(start - d * QB, r)
            if d == 1:
                return i, pl.ds(pl.multiple_of(start, QB), QB), pl.ds(pl.multiple_of(pstart, QB), QB)
            return i, pl.ds(start, QB, stride=d), pl.ds(pstart, QB, stride=d)

        def scores(idx, carry):
            i, rows, prows = windows(idx)
            qs = (q_ref[rows, :] * ATTN_SCALE).astype(BF16)
            kk = jnp.concatenate([k_ref[rows, :], k_ref[prows, :]], axis=0).astype(BF16)
            s = lax.dot_general(qs, kk, NT, preferred_element_type=F32) + bias
            s_ref[idx] = jnp.where(jnp.logical_or(lane2 < QB, i > 0), s, NEG)
            return carry

        def softmax(idx, carry):
            s = s_ref[idx]
            m = jnp.max(jnp.maximum(s[:, :QB], s[:, QB:]), axis=1, keepdims=True)
            p_ref[idx] = jnp.exp(s - m).astype(BF16)
            m_ref[idx] = jnp.broadcast_to(m, (QB, HEAD_DIM))
            return carry

        def values(idx, carry, bi=bi):
            i, rows, prows = windows(idx)
            vv = jnp.concatenate([
                jnp.concatenate([v_ref[rows, :].astype(BF16), ones], axis=1),
                jnp.concatenate([v_ref[prows, :].astype(BF16), ones], axis=1)], axis=0)
            o = jnp.dot(p_ref[idx], vv, preferred_element_type=F32)
            den = o[:, HEAD_DIM:]
            ob_ref[bi, rows, :] = o[:, :HEAD_DIM] / den
            lse_ref[bi, rows, :] = m_ref[idx] + jnp.log(den)
            return carry

        for stage in (scores, softmax, values):
            lax.fori_loop(0, d * nblk, stage, 0, unroll=4)

    CH = 256

    def combine(c, carry):
        rows = pl.ds(pl.multiple_of(c * CH, CH), CH)
        l0, l1, l2 = lse_ref[0, rows, :], lse_ref[1, rows, :], lse_ref[2, rows, :]
        mx = jnp.maximum(jnp.maximum(l0, l1), l2)
        w0, w1, w2 = jnp.exp(l0 - mx), jnp.exp(l1 - mx), jnp.exp(l2 - mx)
        num = w0 * ob_ref[0, rows, :] + w1 * ob_ref[1, rows, :] + w2 * ob_ref[2, rows, :]
        o_ref[rows, :] = (num / (w0 + w1 + w2)).astype(o_ref.dtype)
        return carry

    lax.fori_loop(0, S // CH, combine, 0)


def attn_prompt(z, B, S):
    H = ATTN_HEADS
    assert S % (DILATIONS[-1][1] * QUERY_BLOCK) == 0
    return pl.pallas_call(
        functools.partial(_attn_prompt_body, S=S),
        grid=(B, H),
        in_specs=[
            pl.BlockSpec((S, HEAD_DIM), lambda b, h: (b, h)),
            pl.BlockSpec((S, HEAD_DIM), lambda b, h: (b, H + h)),
            pl.BlockSpec((S, HEAD_DIM), lambda b, h: (b, 2 * H + h)),
        ],
        out_specs=pl.BlockSpec((S, HEAD_DIM), lambda b, h: (b, h)),
        out_shape=jax.ShapeDtypeStruct((B * S, ATTN_WIDTH), BF16),
        scratch_shapes=[
            pltpu.VMEM((len(DILATIONS), S, HEAD_DIM), F32),
            pltpu.VMEM((len(DILATIONS), S, HEAD_DIM), F32),
            pltpu.VMEM((S // QUERY_BLOCK, QUERY_BLOCK, 2 * QUERY_BLOCK), F32),
            pltpu.VMEM((S // QUERY_BLOCK, QUERY_BLOCK, 2 * QUERY_BLOCK), BF16),
            pltpu.VMEM((S // QUERY_BLOCK, QUERY_BLOCK, HEAD_DIM), F32),
        ],
        compiler_params=_cparams("parallel", "parallel"),
        name="attn_prompt",
    )(z, z, z)


def _attn_sample_body(z_ref, k1_ref, k4_ref, k16_ref, v1_ref, v4_ref, v16_ref, o_ref, *, bb):
    H = ATTN_HEADS
    for b in range(bb):
        q = z_ref[b, 0:H, :] * ATTN_SCALE
        k_new = z_ref[b, H:2 * H, :]
        v_new = z_ref[b, 2 * H:3 * H, :]
        s_new = jnp.sum(k_new * q, axis=-1, keepdims=True)
        s_win = [jnp.sum(kr[b] * q, axis=-1, keepdims=True)
                 for kr in (k1_ref, k4_ref, k16_ref)]
        m = s_new
        for s in s_win:
            m = jnp.maximum(m, jnp.max(s, axis=0))
        p_new = jnp.exp(s_new - m) * float(len(DILATIONS))
        den = p_new
        acc = p_new * v_new
        for s, vr in zip(s_win, (v1_ref, v4_ref, v16_ref)):
            p = jnp.exp(s - m)
            den = den + jnp.sum(p, axis=0)
            acc = acc + jnp.sum(p * vr[b], axis=0)
        o_ref[b] = acc / den


def attn_sample(z, cache_k, cache_v, bb=4):
    Bs = z.shape[0]
    wbuf = cache_k.shape[1]
    QB, H = QUERY_BLOCK, ATTN_HEADS
    assert wbuf == PAST_LEN == WINDOW_MAX and Bs % bb == 0

    def views(c):
        out, specs = [], []
        for window, d in DILATIONS:
            n = wbuf // d
            if d == 1:
                out.append(c)
                specs.append(pl.BlockSpec((bb, QB, H, HEAD_DIM), lambda i, n=n: (i, n // QB - 1, 0, 0)))
            else:
                out.append(c.reshape(Bs, n, d, H, HEAD_DIM))
                specs.append(pl.BlockSpec((bb, QB, None, H, HEAD_DIM),
                                          lambda i, n=n: (i, n // QB - 1, 0, 0, 0)))
        return out, specs

    kv, kspecs = views(cache_k)
    vv, vspecs = views(cache_v)
    z3 = z.reshape(Bs, MAIN_COLS // HEAD_DIM, HEAD_DIM)
    out = pl.pallas_call(
        functools.partial(_attn_sample_body, bb=bb),
        grid=(Bs // bb,),
        in_specs=[pl.BlockSpec((bb, MAIN_COLS // HEAD_DIM, HEAD_DIM), lambda i: (i, 0, 0))] + kspecs + vspecs,
        out_specs=pl.BlockSpec((bb, H, HEAD_DIM), lambda i: (i, 0, 0)),
        out_shape=jax.ShapeDtypeStruct((Bs, H, HEAD_DIM), F32),
        compiler_params=_cparams("parallel"),
        name="attn_sample",
    )(z3, *kv, *vv)
    return out.reshape(Bs, ATTN_WIDTH).astype(BF16)


def _gla_prompt_body(q_ref, k_ref, v_ref, go_ref, la_ref, gn_ref, o_ref, st_ref, sT_ref, b_ref, oi_ref,
                     *, C):
    c = pl.program_id(1)
    nsub = C // GLA_SUB

    @pl.when(c == 0)
    def _():
        sT_ref[...] = jnp.zeros_like(sT_ref)

    tri = (lax.broadcasted_iota(jnp.int32, (C, C), 1)
           <= lax.broadcasted_iota(jnp.int32, (C, C), 0)).astype(F32)
    srow = lax.broadcasted_iota(jnp.int32, (GLA_SUB, 1), 0)

    for h in range(GLA_HEADS):
        kl = pl.ds(h * GLA_DK, GLA_DK)
        vl = pl.ds(h * GLA_DV, GLA_DV)
        b = jnp.dot(tri, la_ref[:, kl], precision=HIGHEST, preferred_element_type=F32)
        b_ref[h] = b
        qh = q_ref[:, kl] * (GLA_DK ** -0.5)
        kh = k_ref[:, kl]
        vh = v_ref[:, vl]
        b_end = b[C - 1:C, :]
        sT = sT_ref[h]
        oi_ref[h] = lax.dot_general((qh * jnp.exp(b)).astype(BF16), sT.astype(BF16), NT,
                                    preferred_element_type=F32)
        k_end = (kh * jnp.exp(b_end - b)).astype(BF16)
        sT_ref[h] = sT * jnp.exp(b_end) + lax.dot_general(vh.astype(BF16), k_end, TN,
                                                          preferred_element_type=F32)
        for I in range(1, nsub):
            r0 = I * GLA_SUB
            cI = b[r0 - 1:r0, :]
            qI = (qh[r0:r0 + GLA_SUB] * jnp.exp(b[r0:r0 + GLA_SUB] - cI)).astype(BF16)
            kI = (kh[:r0] * jnp.exp(cI - b[:r0])).astype(BF16)
            a = lax.dot_general(qI, kI, NT, preferred_element_type=F32)
            oi_ref[h, r0:r0 + GLA_SUB, :] += jnp.dot(a.astype(BF16), vh[:r0].astype(BF16),
                                                     preferred_element_type=F32)

    def diag(I, carry):
        rows = pl.ds(pl.multiple_of(I * GLA_SUB, GLA_SUB), GLA_SUB)
        for h in range(GLA_HEADS):
            bI = b_ref[h, rows, :]
            qI = q_ref[rows, pl.ds(h * GLA_DK, GLA_DK)] * (GLA_DK ** -0.5)
            kI = k_ref[rows, pl.ds(h * GLA_DK, GLA_DK)]
            vI = v_ref[rows, pl.ds(h * GLA_DV, GLA_DV)]
            acc = jnp.zeros((GLA_SUB, GLA_DV), F32)
            for s in range(GLA_SUB):
                e = jnp.exp(jnp.minimum(bI - bI[s:s + 1, :], 0.0))
                a_col = jnp.sum(qI * kI[s:s + 1, :] * e, axis=1, keepdims=True)
                a_col = jnp.where(srow >= s, a_col, 0.0)
                acc += a_col * vI[s:s + 1, :]
            oi_ref[h, rows, :] += acc
        return carry

    lax.fori_loop(0, nsub, diag, 0)

    for h in range(GLA_HEADS):
        vl = pl.ds(h * GLA_DV, GLA_DV)
        g = go_ref[:, vl]
        o_ref[:, vl] = (_rms(oi_ref[h], gn_ref[...]) * (g * jax.nn.sigmoid(g))).astype(o_ref.dtype)

    @pl.when(c == pl.num_programs(1) - 1)
    def _():
        for h in range(GLA_HEADS):
            st_ref[0, h] = sT_ref[h].T


def gla_prompt(z, log_a, gla_norm, B, S, C=128):
    assert S % C == 0 and C % GLA_SUB == 0
    nc = S // C
    q_blk = 3 * ATTN_WIDTH // GLA_KW
    v_blk = (3 * ATTN_WIDTH + 2 * GLA_KW) // GLA_WIDTH
    return pl.pallas_call(
        functools.partial(_gla_prompt_body, C=C),
        grid=(B, nc),
        in_specs=[
            pl.BlockSpec((C, GLA_KW), lambda b, c: (b * nc + c, q_blk)),
            pl.BlockSpec((C, GLA_KW), lambda b, c: (b * nc + c, q_blk + 1)),
            pl.BlockSpec((C, GLA_WIDTH), lambda b, c: (b * nc + c, v_blk)),
            pl.BlockSpec((C, GLA_WIDTH), lambda b, c: (b * nc + c, v_blk + 1)),
            pl.BlockSpec((C, GLA_KW), lambda b, c: (b * nc + c, 0)),
            pl.BlockSpec((1, GLA_DV), lambda b, c: (0, 0)),
        ],
        out_specs=[
            pl.BlockSpec((C, GLA_WIDTH), lambda b, c: (b * nc + c, 0)),
            pl.BlockSpec((1, GLA_HEADS, GLA_DK, GLA_DV), lambda b, c: (b, 0, 0, 0)),
        ],
        out_shape=[
            jax.ShapeDtypeStruct((B * S, GLA_WIDTH), BF16),
            jax.ShapeDtypeStruct((B, GLA_HEADS, GLA_DK, GLA_DV), F32),
        ],
        scratch_shapes=[
            pltpu.VMEM((GLA_HEADS, GLA_DV, GLA_DK), F32),
            pltpu.VMEM((GLA_HEADS, C, GLA_DK), F32),
            pltpu.VMEM((GLA_HEADS, C, GLA_DV), F32),
        ],
        compiler_params=_cparams("parallel", "arbitrary"),
        name="gla_prompt",
    )(z, z, z, z, log_a, gla_norm)


def _gla_sample_body(q_ref, k_ref, v_ref, go_ref, la_ref, gn_ref, s_ref, o_ref, so_ref, *, bb):
    for h in range(GLA_HEADS):
        kl = pl.ds(h * GLA_DK, GLA_DK)
        vl = pl.ds(h * GLA_DV, GLA_DV)
        aT = jnp.exp(la_ref[:, kl]).T
        kT = k_ref[:, kl].T
        qT = (q_ref[:, kl] * (GLA_DK ** -0.5)).T
        outs = []
        for b in range(bb):
            s_new = aT[:, b:b + 1] * s_ref[b, h] + kT[:, b:b + 1] * v_ref[b:b + 1, vl]
            so_ref[b, h] = s_new
            outs.append(jnp.sum(qT[:, b:b + 1] * s_new, axis=0, keepdims=True))
        o = jnp.concatenate(outs, axis=0)
        g = go_ref[:, vl]
        o_ref[:, vl] = (_rms(o, gn_ref[...]) * (g * jax.nn.sigmoid(g))).astype(o_ref.dtype)


def gla_sample(z, log_a, gla_norm, state, bb=SUBLANES):
    Bs = z.shape[0]
    assert Bs % bb == 0
    q_blk = 3 * ATTN_WIDTH // GLA_KW
    v_blk = (3 * ATTN_WIDTH + 2 * GLA_KW) // GLA_WIDTH
    st_spec = pl.BlockSpec((bb, GLA_HEADS, GLA_DK, GLA_DV), lambda i: (i, 0, 0, 0))
    return pl.pallas_call(
        functools.partial(_gla_sample_body, bb=bb),
        grid=(Bs // bb,),
        in_specs=[
            pl.BlockSpec((bb, GLA_KW), lambda i: (i, q_blk)),
            pl.BlockSpec((bb, GLA_KW), lambda i: (i, q_blk + 1)),
            pl.BlockSpec((bb, GLA_WIDTH), lambda i: (i, v_blk)),
            pl.BlockSpec((bb, GLA_WIDTH), lambda i: (i, v_blk + 1)),
            pl.BlockSpec((bb, GLA_KW), lambda i: (i, 0)),
            pl.BlockSpec((1, GLA_DV), lambda i: (0, 0)),
            st_spec,
        ],
        out_specs=[pl.BlockSpec((bb, GLA_WIDTH), lambda i: (i, 0)), st_spec],
        out_shape=[
            jax.ShapeDtypeStruct((Bs, GLA_WIDTH), BF16),
            jax.ShapeDtypeStruct(state.shape, F32),
        ],
        compiler_params=_cparams("parallel"),
        name="gla_sample",
    )(z, z, z, z, log_a, gla_norm, state)


def _proj_residual_body(a_ref, b_ref, wa_ref, wb_ref, res_ref, o_ref):
    acc = jnp.dot(a_ref[...], wa_ref[...], preferred_element_type=F32)
    acc += jnp.dot(b_ref[...], wb_ref[...], preferred_element_type=F32)
    o_ref[...] = res_ref[...] + acc


def proj_residual(a, b, wa, wb, res, tm, tn=512):
    T, Ka = a.shape
    Kb = b.shape[1]
    N = wa.shape[1]
    assert T % tm == 0 and N % tn == 0
    return pl.pallas_call(
        _proj_residual_body,
        grid=(T // tm, N // tn),
        in_specs=[
            pl.BlockSpec((tm, Ka), lambda i, j: (i, 0)),
            pl.BlockSpec((tm, Kb), lambda i, j: (i, 0)),
            pl.BlockSpec((Ka, tn), lambda i, j: (0, j)),
            pl.BlockSpec((Kb, tn), lambda i, j: (0, j)),
            pl.BlockSpec((tm, tn), lambda i, j: (i, j)),
        ],
        out_specs=pl.BlockSpec((tm, tn), lambda i, j: (i, j)),
        out_shape=jax.ShapeDtypeStruct((T, N), F32),
        compiler_params=_cparams("parallel", "arbitrary"),
        name="proj_residual",
    )(a, b, wa, wb, res)


def _peer_query_body(h_ref, g_ref, w_ref, q_ref, xT_ref, xn_ref):
    @pl.when(pl.program_id(1) == 0)
    def _():
        xn = _rms(h_ref[...], g_ref[...])
        xn_ref[...] = xn.astype(BF16)
        xT_ref[...] = xn.T.astype(BF16)

    q_ref[...] = jnp.dot(xn_ref[...], w_ref[...], preferred_element_type=F32)


def peer_query(h, g, w, tm, tn=512):
    T, D = h.shape
    N = w.shape[1]
    assert T % tm == 0 and N % tn == 0
    return pl.pallas_call(
        _peer_query_body,
        grid=(T // tm, N // tn),
        in_specs=[
            pl.BlockSpec((tm, D), lambda i, j: (i, 0)),
            pl.BlockSpec((1, D), lambda i, j: (0, 0)),
            pl.BlockSpec((D, tn), lambda i, j: (0, j)),
        ],
        out_specs=[
            pl.BlockSpec((tm, tn), lambda i, j: (i, j)),
            pl.BlockSpec((D, tm), lambda i, j: (0, i)),
        ],
        out_shape=[
            jax.ShapeDtypeStruct((T, N), F32),
            jax.ShapeDtypeStruct((D, T), BF16),
        ],
        scratch_shapes=[pltpu.VMEM((tm, D), BF16)],
        compiler_params=_cparams("parallel", "arbitrary"),
        name="peer_query",
    )(h, g, w)


def _peer_route_body(q_ref, k1_ref, k2_ref, s1_ref, s2_ref, thr_ref, *, tm):
    K = PEER_TOPK
    KR = -(-(K + 1) // SUBLANES) * SUBLANES
    rowk = lax.broadcasted_iota(jnp.int32, (KR, tm), 0)
    row8 = lax.broadcasted_iota(jnp.int32, (SUBLANES, tm), 0)
    rowh = lax.broadcasted_iota(jnp.int32, (PEER_HEADS, tm), 0)
    thr_all = jnp.zeros((PEER_HEADS, tm), F32)

    def top_values(s):
        vals = jnp.full((KR, tm), -jnp.inf, F32)
        for it in range(K + 1):
            mx = jnp.max(s, axis=0, keepdims=True)
            vals = jnp.where(rowk == it, mx, vals)
            s = jnp.where(s == mx, -jnp.inf, s)
        return vals

    for h in range(PEER_HEADS):
        q1 = q_ref[:, h * PEER_QDIM:h * PEER_QDIM + PEER_HALF]
        q2 = q_ref[:, h * PEER_QDIM + PEER_HALF:(h + 1) * PEER_QDIM]
        s1 = lax.dot_general(k1_ref[h], q1, NT, precision=HIGHEST, preferred_element_type=F32)
        s2 = lax.dot_general(k2_ref[h], q2, NT, precision=HIGHEST, preferred_element_type=F32)
        v1 = top_values(s1)
        v2 = top_values(s2)
        pieces = [v1[0:1] + v2[r:r + SUBLANES] for r in range(0, KR, SUBLANES)]
        for a in range(1, SUBLANES):
            pieces.append(jnp.where(row8 < (K + 1) // (a + 1), v1[a:a + 1] + v2[0:SUBLANES], -jnp.inf))
        pieces += [v1[r:r + SUBLANES] + v2[0:1] for r in range(SUBLANES, KR, SUBLANES)]
        cand = jnp.concatenate(pieces, axis=0)
        m = v1[0:1] + v2[0:1]
        z = jnp.zeros((1, tm), F32)
        kth = m
        for it in range(K):
            kth = jnp.max(cand, axis=0, keepdims=True)
            z = z + jnp.exp(kth - m)
            cand = jnp.where(cand == kth, -jnp.inf, cand)
        nxt = jnp.max(cand, axis=0, keepdims=True)
        c = m + jnp.log(z)
        s1_ref[h] = (s1 - c) * LOG2E - 1.0
        s2_ref[h] = s2 * LOG2E
        thr_all = jnp.where(rowh == h, (0.5 * (kth + nxt) - c) * LOG2E - 1.0, thr_all)
    thr_ref[...] = thr_all


def peer_route(q, keys1, keys2, tm):
    T = q.shape[0]
    assert T % tm == 0 and tm % LANES == 0 and (PEER_TOPK + 1) // 2 <= SUBLANES
    tile = pl.BlockSpec((PEER_HEADS, PEER_NKEYS, tm), lambda i: (0, 0, i))
    tile_shape = jax.ShapeDtypeStruct((PEER_HEADS, PEER_NKEYS, T), F32)
    kspec = pl.BlockSpec((PEER_HEADS, PEER_NKEYS, PEER_HALF), lambda i: (0, 0, 0))
    return pl.pallas_call(
        functools.partial(_peer_route_body, tm=tm),
        grid=(T // tm,),
        in_specs=[pl.BlockSpec((tm, PEER_HEADS * PEER_QDIM), lambda i: (i, 0)), kspec, kspec],
        out_specs=[tile, tile, pl.BlockSpec((PEER_HEADS, tm), lambda i: (0, i))],
        out_shape=[tile_shape, tile_shape, jax.ShapeDtypeStruct((PEER_HEADS, T), F32)],
        compiler_params=_cparams("parallel"),
        name="peer_route",
    )(q, keys1, keys2)


def _peer_dense_body(xT_ref, u_ref, v_ref, s1_ref, s2_ref, thr_ref, y_ref, *, te, tm):
    e = pl.program_id(1)

    @pl.when(e == 0)
    def _():
        y_ref[...] = jnp.zeros_like(y_ref)

    rows = []
    for c in range(te // PEER_NKEYS):
        tiles = []
        for tc in range(tm // LANES):
            lanes = pl.ds(tc * LANES, LANES)
            gate = jnp.zeros((PEER_NKEYS, LANES), F32)
            for h in range(PEER_HEADS):
                ssum = s2_ref[h, :, lanes] + s1_ref[h, c:c + 1, lanes]
                gate += jnp.where(ssum >= thr_ref[pl.ds(h, 1), lanes], jnp.exp2(ssum), 0.0)
            tiles.append(gate)
        rows.append(jnp.concatenate(tiles, axis=1))
    half_gate = jnp.concatenate(rows, axis=0)
    hT = jnp.dot(u_ref[...], xT_ref[...], preferred_element_type=F32)
    act = (hT + hT * lax.erf(hT * (2.0 ** -0.5))) * half_gate
    y_ref[...] += jnp.dot(act.T.astype(BF16), v_ref[...], preferred_element_type=F32)


def peer_dense(xT, u, v, s1, s2, thr, tm):
    D, T = xT.shape
    E = u.shape[0]
    te = SUBLANES * PEER_NKEYS
    assert T % tm == 0 and E % te == 0 and tm % LANES == 0
    return pl.pallas_call(
        functools.partial(_peer_dense_body, te=te, tm=tm),
        grid=(T // tm, E // te),
        in_specs=[
            pl.BlockSpec((D, tm), lambda i, e: (0, i)),
            pl.BlockSpec((te, D), lambda i, e: (e, 0)),
            pl.BlockSpec((te, D), lambda i, e: (e, 0)),
            pl.BlockSpec((PEER_HEADS, SUBLANES, tm), lambda i, e: (0, e, i)),
            pl.BlockSpec((PEER_HEADS, PEER_NKEYS, tm), lambda i, e: (0, 0, i)),
            pl.BlockSpec((PEER_HEADS, tm), lambda i, e: (0, i)),
        ],
        out_specs=pl.BlockSpec((tm, D), lambda i, e: (i, 0)),
        out_shape=jax.ShapeDtypeStruct((T, D), F32),
        compiler_params=_cparams("parallel", "arbitrary"),
        name="peer_dense",
    )(xT, u, v, s1, s2, thr)


def _final_norm_body(h_ref, y_ref, g_ref, o_ref):
    o_ref[...] = _rms(h_ref[...] + y_ref[...], g_ref[...])


def final_norm(h, y, g, tm):
    T, D = h.shape
    return pl.pallas_call(
        _final_norm_body,
        grid=(T // tm,),
        in_specs=[
            pl.BlockSpec((tm, D), lambda i: (i, 0)),
            pl.BlockSpec((tm, D), lambda i: (i, 0)),
            pl.BlockSpec((1, D), lambda i: (0, 0)),
        ],
        out_specs=pl.BlockSpec((tm, D), lambda i: (i, 0)),
        out_shape=jax.ShapeDtypeStruct((T, D), F32),
        compiler_params=_cparams("parallel"),
        name="final_norm",
    )(h, y, g)


def _post_mix(x, o_attn, o_gla, w, tm, tm_proj):
    h = proj_residual(o_attn, o_gla, w["w_o_a"], w["w_o_b"], x, tm_proj)
    qp, xT = peer_query(h, w["norm_ffn"], w["peer_query"], tm)
    s1, s2, thr = peer_route(qp, w["peer_keys_1"], w["peer_keys_2"], tm)
    y = peer_dense(xT, w["expert_u"], w["expert_v"], s1, s2, thr, tm)
    return final_norm(h, y, w["final_norm"], tm)


def _prompt_group(x_prompt, w):
    B, S, _ = x_prompt.shape
    x = x_prompt.reshape(B * S, D_MODEL)
    tm, tm_proj = 512, 1024
    z, log_a = in_proj(x, w, jnp.arange(S), tm_proj)
    o_attn = attn_prompt(z, B, S)
    o_gla, state = gla_prompt(z, log_a, w["gla_norm"], B, S)
    out = _post_mix(x, o_attn, o_gla, w, tm, tm_proj)
    return out, z, state


def _sample_group(x_sample, cache_k, cache_v, state, w):
    Bs, Ls, _ = x_sample.shape
    assert Ls == 1
    x = x_sample.reshape(Bs, D_MODEL)
    tm = Bs
    z, log_a = in_proj(x, w, jnp.full((tm,), PAST_LEN, jnp.int32), tm)
    o_attn = attn_sample(z, cache_k, cache_v)
    o_gla, new_state = gla_sample(z, log_a, w["gla_norm"], state)
    out = _post_mix(x, o_attn, o_gla, w, tm, tm)
    return out, z, new_state


def _prepare(norm_attn, w_in, w_gate_up, b_gate, gla_norm, w_o, norm_ffn, peer_query_w, peer_keys_1,
             peer_keys_2, expert_u, expert_v, final_norm_w):
    assert w_in.shape[0] == 1
    l = 0
    w_in_l = w_in[l]
    row = lambda v: v.reshape(1, -1).astype(F32)
    return {
        "norm_attn": row(norm_attn[l]),
        "w_in_main": w_in_l[:, :MAIN_COLS].astype(BF16),
        "w_in_lr": jnp.pad(w_in_l[:, MAIN_COLS:], ((0, 0), (0, LANES - GLA_LOWRANK))).astype(BF16),
        "w_gate_up": jnp.pad(w_gate_up[l], ((0, LANES - GLA_LOWRANK), (0, 0))),
        "b_gate": row(b_gate[l]),
        "gla_norm": row(gla_norm[l]),
        "w_o_a": w_o[l][:ATTN_WIDTH].astype(BF16),
        "w_o_b": w_o[l][ATTN_WIDTH:].astype(BF16),
        "norm_ffn": row(norm_ffn[l]),
        "peer_query": peer_query_w[l].astype(BF16),
        "peer_keys_1": peer_keys_1[l],
        "peer_keys_2": peer_keys_2[l],
        "expert_u": expert_u[l].astype(BF16),
        "expert_v": expert_v[l].astype(BF16),
        "final_norm": row(final_norm_w),
    }


def kernel(x_prompt, x_sample, cache_attn_k, cache_attn_v, state_gla, norm_attn, w_in, w_gate_up,
           b_gate, gla_norm, w_o, norm_ffn, peer_query, peer_keys_1, peer_keys_2, expert_u,
           expert_v, final_norm):
    Bp, Lp, _ = x_prompt.shape
    Bs, Ls, _ = x_sample.shape
    w = _prepare(norm_attn, w_in, w_gate_up, b_gate, gla_norm, w_o, norm_ffn, peer_query, peer_keys_1,
                 peer_keys_2, expert_u, expert_v, final_norm)
    win_p = min(WINDOW_MAX, Lp)
    yp, zp, sp = _prompt_group(x_prompt, w)
    ys, zs, ss = _sample_group(x_sample, cache_attn_k[0], cache_attn_v[0], state_gla[0], w)

    def heads(z, lo, B, L):
        return z[:, lo:lo + ATTN_WIDTH].reshape(B, L, ATTN_HEADS, HEAD_DIM)

    kp = heads(zp, ATTN_WIDTH, Bp, Lp)[:, Lp - win_p:]
    vp = heads(zp, 2 * ATTN_WIDTH, Bp, Lp)[:, Lp - win_p:]
    return (yp.reshape(Bp, Lp, D_MODEL), ys.reshape(Bs, Ls, D_MODEL),
            kp[None], vp[None], sp[None],
            heads(zs, ATTN_WIDTH, Bs, Ls)[None], heads(zs, 2 * ATTN_WIDTH, Bs, Ls)[None], ss[None])


def sample_group(inp):
    names = ("norm_attn", "w_in", "w_gate_up", "b_gate", "gla_norm", "w_o", "norm_ffn", "peer_query",
             "peer_keys_1", "peer_keys_2", "expert_u", "expert_v", "final_norm")
    w = _prepare(*[inp[n] for n in names])
    out, z, st = _sample_group(inp["x_sample"], inp["cache_attn_k"][0], inp["cache_attn_v"][0],
                               inp["state_gla"][0], w)
    Bs = out.shape[0]
    return (out, z[:, ATTN_WIDTH:2 * ATTN_WIDTH].reshape(Bs, 1, ATTN_HEADS, HEAD_DIM),
            z[:, 2 * ATTN_WIDTH:3 * ATTN_WIDTH].reshape(Bs, 1, ATTN_HEADS, HEAD_DIM), st)
```

```python
import functools
import math

import jax
import jax.numpy as jnp
import numpy as np
from jax import lax
from jax.experimental import pallas as pl
from jax.experimental.pallas import tpu as pltpu

F32 = jnp.float32
BF16 = jnp.bfloat16

D_MODEL = 2048
PAST_LEN = 2048
HEAD_DIM = 128
ATTN_HEADS = 8
ATTN_WIDTH = ATTN_HEADS * HEAD_DIM
DILATIONS = ((128, 1), (512, 4), (2048, 16))
WINDOW_MAX = 2048
QUERY_BLOCK = 128
ATTN_SCALE = HEAD_DIM ** -0.5
ROPE_THETA = 500000.0
ROT_DIM = HEAD_DIM // 4
ROT_HALF = ROT_DIM // 2
GLA_HEADS = 4
GLA_WIDTH = D_MODEL - ATTN_WIDTH
GLA_DV = GLA_WIDTH // GLA_HEADS
GLA_DK = GLA_DV // 2
GLA_KW = GLA_HEADS * GLA_DK
GLA_LOWRANK = 16
GLA_TAU = 16.0
GLA_SUB = 16
PEER_HEADS = 8
PEER_NKEYS = 128
PEER_EXPERTS = PEER_NKEYS * PEER_NKEYS
PEER_QDIM = 256
PEER_HALF = PEER_QDIM // 2
PEER_TOPK = 16
NORM_EPS = 1e-6
MAIN_COLS = 3 * ATTN_WIDTH + 2 * GLA_KW + 2 * GLA_WIDTH

LANES = 128
SUBLANES = 8
VMEM_LIMIT = 56 * 1024 * 1024
NEG = -1e30
LOG2E = 1.0 / math.log(2.0)
HIGHEST = lax.Precision.HIGHEST
NT = (((1,), (1,)), ((), ()))
TN = (((0,), (0,)), ((), ()))


def _cparams(*sem):
    return pltpu.CompilerParams(dimension_semantics=sem, vmem_limit_bytes=VMEM_LIMIT)


def _rms(x, g):
    return x * lax.rsqrt(jnp.mean(x * x, axis=-1, keepdims=True) + NORM_EPS) * g


def _in_proj_body(x_ref, g_ref, w_ref, wlr_ref, wup_ref, bg_ref, cos_ref, sin_ref,
                  z_ref, la_ref, xn_ref, *, tn):
    j = pl.program_id(1)

    @pl.when(j == 0)
    def _():
        xn = _rms(x_ref[...], g_ref[...]).astype(BF16)
        xn_ref[...] = xn
        lr = jnp.dot(xn, wlr_ref[...], preferred_element_type=F32)
        zg = jnp.dot(lr, wup_ref[...], precision=HIGHEST, preferred_element_type=F32) + bg_ref[...]
        la_ref[...] = (jnp.minimum(zg, 0.0) - jnp.log1p(jnp.exp(-jnp.abs(zg)))) * (1.0 / GLA_TAU)

    acc = jnp.dot(xn_ref[...], w_ref[...], preferred_element_type=F32)

    @pl.when(j < 2 * ATTN_WIDTH // tn)
    def _():
        lane = lax.broadcasted_iota(jnp.int32, (acc.shape[0], HEAD_DIM), 1)
        cos = cos_ref[...]
        sin = sin_ref[...]
        for c in range(tn // HEAD_DIM):
            a = acc[:, c * HEAD_DIM:(c + 1) * HEAD_DIM]
            partner = jnp.where(lane < ROT_HALF, pltpu.roll(a, HEAD_DIM - ROT_HALF, 1),
                                pltpu.roll(a, ROT_HALF, 1))
            z_ref[:, c * HEAD_DIM:(c + 1) * HEAD_DIM] = a * cos + partner * sin

    @pl.when(j >= 2 * ATTN_WIDTH // tn)
    def _():
        z_ref[...] = acc


def _rotary_tables(pos):
    inv = jnp.exp(-math.log(ROPE_THETA) * jnp.arange(ROT_HALF, dtype=F32) * (2.0 / ROT_DIM))
    ang = pos.astype(F32)[:, None] * inv[None, :]
    n = pos.shape[0]
    cos = jnp.concatenate([jnp.cos(ang), jnp.cos(ang), jnp.ones((n, HEAD_DIM - ROT_DIM), F32)], axis=1)
    sin = jnp.concatenate([-jnp.sin(ang), jnp.sin(ang), jnp.zeros((n, HEAD_DIM - ROT_DIM), F32)], axis=1)
    return cos, sin


def in_proj(x, w, pos, tm, tn=512):
    T, D = x.shape
    P = pos.shape[0]
    assert T % tm == 0 and MAIN_COLS % tn == 0 and P % tm == 0 and (2 * ATTN_WIDTH) % tn == 0
    cos, sin = _rotary_tables(pos)
    const = lambda i, j: (0, 0)
    return pl.pallas_call(
        functools.partial(_in_proj_body, tn=tn),
        grid=(T // tm, MAIN_COLS // tn),
        in_specs=[
            pl.BlockSpec((tm, D), lambda i, j: (i, 0)),
            pl.BlockSpec((1, D), const),
            pl.BlockSpec((D, tn), lambda i, j: (0, j)),
            pl.BlockSpec((D, LANES), const),
            pl.BlockSpec((LANES, GLA_KW), const),
            pl.BlockSpec((1, GLA_KW), const),
            pl.BlockSpec((tm, HEAD_DIM), lambda i, j: (i % (P // tm), 0)),
            pl.BlockSpec((tm, HEAD_DIM), lambda i, j: (i % (P // tm), 0)),
        ],
        out_specs=[
            pl.BlockSpec((tm, tn), lambda i, j: (i, j)),
            pl.BlockSpec((tm, GLA_KW), lambda i, j: (i, 0)),
        ],
        out_shape=[
            jax.ShapeDtypeStruct((T, MAIN_COLS), F32),
            jax.ShapeDtypeStruct((T, GLA_KW), F32),
        ],
        scratch_shapes=[pltpu.VMEM((tm, D), BF16)],
        compiler_params=_cparams("parallel", "arbitrary"),
        name="in_proj",
    )(x, w["norm_attn"], w["w_in_main"], w["w_in_lr"], w["w_gate_up"], w["b_gate"], cos, sin)


def _attn_prompt_body(q_ref, k_ref, v_ref, o_ref, ob_ref, lse_ref, s_ref, p_ref, m_ref, *, S):
    QB = QUERY_BLOCK
    row = lax.broadcasted_iota(jnp.int32, (QB, QB), 0)
    col = lax.broadcasted_iota(jnp.int32, (QB, QB), 1)
    bias = jnp.concatenate([jnp.where(col <= row, 0.0, NEG), jnp.where(col >= row, 0.0, NEG)], axis=1)
    lane2 = lax.broadcasted_iota(jnp.int32, (QB, 2 * QB), 1)
    ones = jnp.ones((QB, HEAD_DIM), BF16)

    for bi, (window, d) in enumerate(DILATIONS):
        assert window == d * QB
        nblk = S // d // QB
        assert d * nblk == s_ref.shape[0]

        def windows(idx, d=d, nblk=nblk):
            r = idx // nblk
            i = idx % nblk
            start = r + i * (d * QB)
            pstart = jnp.maximum(start - d * QB, r)
            if d == 1:
                return i, pl.ds(pl.multiple_of(start, QB), QB), pl.ds(pl.multiple_of(pstart, QB), QB)
            return i, pl.ds(start, QB, stride=d), pl.ds(pstart, QB, stride=d)

        def scores(idx, carry):
            i, rows, prows = windows(idx)
            qs = (q_ref[rows, :] * ATTN_SCALE).astype(BF16)
            kk = jnp.concatenate([k_ref[rows, :], k_ref[prows, :]], axis=0).astype(BF16)
            s = lax.dot_general(qs, kk, NT, preferred_element_type=F32) + bias
            s_ref[idx] = jnp.where(jnp.logical_or(lane2 < QB, i > 0), s, NEG)
            return carry

        def softmax(idx, carry):
            s = s_ref[idx]
            m = jnp.max(jnp.maximum(s[:, :QB], s[:, QB:]), axis=1, keepdims=True)
            p_ref[idx] = jnp.exp(s - m).astype(BF16)
            m_ref[idx] = jnp.broadcast_to(m, (QB, HEAD_DIM))
            return carry

        def values(idx, carry, bi=bi):
            i, rows, prows = windows(idx)
            vv = jnp.concatenate([
                jnp.concatenate([v_ref[rows, :].astype(BF16), ones], axis=1),
                jnp.concatenate([v_ref[prows, :].astype(BF16), ones], axis=1)], axis=0)
            o = jnp.dot(p_ref[idx], vv, preferred_element_type=F32)
            den = o[:, HEAD_DIM:]
            ob_ref[bi, rows, :] = o[:, :HEAD_DIM] / den
            lse_ref[bi, rows, :] = m_ref[idx] + jnp.log(den)
            return carry

        for stage in (scores, softmax, values):
            lax.fori_loop(0, d * nblk, stage, 0, unroll=4)

    CH = 256

    def combine(c, carry):
        rows = pl.ds(pl.multiple_of(c * CH, CH), CH)
        l0, l1, l2 = lse_ref[0, rows, :], lse_ref[1, rows, :], lse_ref[2, rows, :]
        mx = jnp.maximum(jnp.maximum(l0, l1), l2)
        w0, w1, w2 = jnp.exp(l0 - mx), jnp.exp(l1 - mx), jnp.exp(l2 - mx)
        num = w0 * ob_ref[0, rows, :] + w1 * ob_ref[1, rows, :] + w2 * ob_ref[2, rows, :]
        o_ref[rows, :] = (num / (w0 + w1 + w2)).astype(o_ref.dtype)
        return carry

    lax.fori_loop(0, S // CH, combine, 0)


def attn_prompt(z, B, S):
    H = ATTN_HEADS
    assert S % (DILATIONS[-1][1] * QUERY_BLOCK) == 0
    return pl.pallas_call(
        functools.partial(_attn_prompt_body, S=S),
        grid=(B, H),
        in_specs=[
            pl.BlockSpec((S, HEAD_DIM), lambda b, h: (b, h)),
            pl.BlockSpec((S, HEAD_DIM), lambda b, h: (b, H + h)),
            pl.BlockSpec((S, HEAD_DIM), lambda b, h: (b, 2 * H + h)),
        ],
        out_specs=pl.BlockSpec((S, HEAD_DIM), lambda b, h: (b, h)),
        out_shape=jax.ShapeDtypeStruct((B * S, ATTN_WIDTH), BF16),
        scratch_shapes=[
            pltpu.VMEM((len(DILATIONS), S, HEAD_DIM), F32),
            pltpu.VMEM((len(DILATIONS), S, HEAD_DIM), F32),
            pltpu.VMEM((S // QUERY_BLOCK, QUERY_BLOCK, 2 * QUERY_BLOCK), F32),
            pltpu.VMEM((S // QUERY_BLOCK, QUERY_BLOCK, 2 * QUERY_BLOCK), BF16),
            pltpu.VMEM((S // QUERY_BLOCK, QUERY_BLOCK, HEAD_DIM), F32),
        ],
        compiler_params=_cparams("parallel", "parallel"),
        name="attn_prompt",
    )(z, z, z)


def _attn_sample_body(z_ref, k1_ref, k4_ref, k16_ref, v1_ref, v4_ref, v16_ref, o_ref, *, bb):
    H = ATTN_HEADS
    for b in range(bb):
        q = z_ref[b, 0:H, :] * ATTN_SCALE
        k_new = z_ref[b, H:2 * H, :]
        v_new = z_ref[b, 2 * H:3 * H, :]
        s_new = jnp.sum(k_new * q, axis=-1, keepdims=True)
        s_win = [jnp.sum(kr[b] * q, axis=-1, keepdims=True)
                 for kr in (k1_ref, k4_ref, k16_ref)]
        m = s_new
        for s in s_win:
            m = jnp.maximum(m, jnp.max(s, axis=0))
        p_new = jnp.exp(s_new - m) * float(len(DILATIONS))
        den = p_new
        acc = p_new * v_new
        for s, vr in zip(s_win, (v1_ref, v4_ref, v16_ref)):
            p = jnp.exp(s - m)
            den = den + jnp.sum(p, axis=0)
            acc = acc + jnp.sum(p * vr[b], axis=0)
        o_ref[b] = acc / den


def attn_sample(z, cache_k, cache_v, bb=4):
    Bs = z.shape[0]
    wbuf = cache_k.shape[1]
    QB, H = QUERY_BLOCK, ATTN_HEADS
    assert wbuf == PAST_LEN == WINDOW_MAX and Bs % bb == 0

    def views(c):
        out, specs = [], []
        for window, d in DILATIONS:
            n = wbuf // d
            if d == 1:
                out.append(c)
                specs.append(pl.BlockSpec((bb, QB, H, HEAD_DIM), lambda i, n=n: (i, n // QB - 1, 0, 0)))
            else:
                out.append(c.reshape(Bs, n, d, H, HEAD_DIM))
                specs.append(pl.BlockSpec((bb, QB, None, H, HEAD_DIM),
                                          lambda i, n=n: (i, n // QB - 1, 0, 0, 0)))
        return out, specs

    kv, kspecs = views(cache_k)
    vv, vspecs = views(cache_v)
    z3 = z.reshape(Bs, MAIN_COLS // HEAD_DIM, HEAD_DIM)
    out = pl.pallas_call(
        functools.partial(_attn_sample_body, bb=bb),
        grid=(Bs // bb,),
        in_specs=[pl.BlockSpec((bb, MAIN_COLS // HEAD_DIM, HEAD_DIM), lambda i: (i, 0, 0))] + kspecs + vspecs,
        out_specs=pl.BlockSpec((bb, H, HEAD_DIM), lambda i: (i, 0, 0)),
        out_shape=jax.ShapeDtypeStruct((Bs, H, HEAD_DIM), F32),
        compiler_params=_cparams("parallel"),
        name="attn_sample",
    )(z3, *kv, *vv)
    return out.reshape(Bs, ATTN_WIDTH).astype(BF16)


def _gla_prompt_body(q_ref, k_ref, v_ref, go_ref, la_ref, gn_ref, o_ref, st_ref, sT_ref, b_ref, oi_ref,
                     *, C):
    c = pl.program_id(1)
    nsub = C // GLA_SUB

    @pl.when(c == 0)
    def _():
        sT_ref[...] = jnp.zeros_like(sT_ref)

    tri = (lax.broadcasted_iota(jnp.int32, (C, C), 1)
           <= lax.broadcasted_iota(jnp.int32, (C, C), 0)).astype(F32)
    srow = lax.broadcasted_iota(jnp.int32, (GLA_SUB, 1), 0)

    for h in range(GLA_HEADS):
        kl = pl.ds(h * GLA_DK, GLA_DK)
        vl = pl.ds(h * GLA_DV, GLA_DV)
        b = jnp.dot(tri, la_ref[:, kl], precision=HIGHEST, preferred_element_type=F32)
        b_ref[h] = b
        qh = q_ref[:, kl] * (GLA_DK ** -0.5)
        kh = k_ref[:, kl]
        vh = v_ref[:, vl]
        b_end = b[C - 1:C, :]
        sT = sT_ref[h]
        oi_ref[h] = lax.dot_general((qh * jnp.exp(b)).astype(BF16), sT.astype(BF16), NT,
                                    preferred_element_type=F32)
        k_end = (kh * jnp.exp(b_end - b)).astype(BF16)
        sT_ref[h] = sT * jnp.exp(b_end) + lax.dot_general(vh.astype(BF16), k_end, TN,
                                                          preferred_element_type=F32)
        for I in range(1, nsub):
            r0 = I * GLA_SUB
            cI = b[r0 - 1:r0, :]
            qI = (qh[r0:r0 + GLA_SUB] * jnp.exp(b[r0:r0 + GLA_SUB] - cI)).astype(BF16)
            kI = (kh[:r0] * jnp.exp(cI - b[:r0])).astype(BF16)
            a = lax.dot_general(qI, kI, NT, preferred_element_type=F32)
            oi_ref[h, r0:r0 + GLA_SUB, :] += jnp.dot(a.astype(BF16), vh[:r0].astype(BF16),
                                                     preferred_element_type=F32)

    def diag(I, carry):
        rows = pl.ds(pl.multiple_of(I * GLA_SUB, GLA_SUB), GLA_SUB)
        for h in range(GLA_HEADS):
            bI = b_ref[h, rows, :]
            qI = q_ref[rows, pl.ds(h * GLA_DK, GLA_DK)] * (GLA_DK ** -0.5)
            kI = k_ref[rows, pl.ds(h * GLA_DK, GLA_DK)]
            vI = v_ref[rows, pl.ds(h * GLA_DV, GLA_DV)]
            acc = jnp.zeros((GLA_SUB, GLA_DV), F32)
            for s in range(GLA_SUB):
                e = jnp.exp(jnp.minimum(bI - bI[s:s + 1, :], 0.0))
                a_col = jnp.sum(qI * kI[s:s + 1, :] * e, axis=1, keepdims=True)
                a_col = jnp.where(srow >= s, a_col, 0.0)
                acc += a_col * vI[s:s + 1, :]
            oi_ref[h, rows, :] += acc
        return carry

    lax.fori_loop(0, nsub, diag, 0)

    for h in range(GLA_HEADS):
        vl = pl.ds(h * GLA_DV, GLA_DV)
        g = go_ref[:, vl]
        o_ref[:, vl] = (_rms(oi_ref[h], gn_ref[...]) * (g * jax.nn.sigmoid(g))).astype(o_ref.dtype)

    @pl.when(c == pl.num_programs(1) - 1)
    def _():
        for h in range(GLA_HEADS):
            st_ref[0, h] = sT_ref[h].T


def gla_prompt(z, log_a, gla_norm, B, S, C=128):
    assert S % C == 0 and C % GLA_SUB == 0
    nc = S // C
    q_blk = 3 * ATTN_WIDTH // GLA_KW
    v_blk = (3 * ATTN_WIDTH + 2 * GLA_KW) // GLA_WIDTH
    return pl.pallas_call(
        functools.partial(_gla_prompt_body, C=C),
        grid=(B, nc),
        in_specs=[
            pl.BlockSpec((C, GLA_KW), lambda b, c: (b * nc + c, q_blk)),
            pl.BlockSpec((C, GLA_KW), lambda b, c: (b * nc + c, q_blk + 1)),
            pl.BlockSpec((C, GLA_WIDTH), lambda b, c: (b * nc + c, v_blk)),
            pl.BlockSpec((C, GLA_WIDTH), lambda b, c: (b * nc + c, v_blk + 1)),
            pl.BlockSpec((C, GLA_KW), lambda b, c: (b * nc + c, 0)),
            pl.BlockSpec((1, GLA_DV), lambda b, c: (0, 0)),
        ],
        out_specs=[
            pl.BlockSpec((C, GLA_WIDTH), lambda b, c: (b * nc + c, 0)),
            pl.BlockSpec((1, GLA_HEADS, GLA_DK, GLA_DV), lambda b, c: (b, 0, 0, 0)),
        ],
        out_shape=[
            jax.ShapeDtypeStruct((B * S, GLA_WIDTH), BF16),
            jax.ShapeDtypeStruct((B, GLA_HEADS, GLA_DK, GLA_DV), F32),
        ],
        scratch_shapes=[
            pltpu.VMEM((GLA_HEADS, GLA_DV, GLA_DK), F32),
            pltpu.VMEM((GLA_HEADS, C, GLA_DK), F32),
            pltpu.VMEM((GLA_HEADS, C, GLA_DV), F32),
        ],
        compiler_params=_cparams("parallel", "arbitrary"),
        name="gla_prompt",
    )(z, z, z, z, log_a, gla_norm)


def _gla_sample_body(q_ref, k_ref, v_ref, go_ref, la_ref, gn_ref, s_ref, o_ref, so_ref, *, bb):
    for h in range(GLA_HEADS):
        kl = pl.ds(h * GLA_DK, GLA_DK)
        vl = pl.ds(h * GLA_DV, GLA_DV)
        aT = jnp.exp(la_ref[:, kl]).T
        kT = k_ref[:, kl].T
        qT = (q_ref[:, kl] * (GLA_DK ** -0.5)).T
        outs = []
        for b in range(bb):
            s_new = aT[:, b:b + 1] * s_ref[b, h] + kT[:, b:b + 1] * v_ref[b:b + 1, vl]
            so_ref[b, h] = s_new
            outs.append(jnp.sum(qT[:, b:b + 1] * s_new, axis=0, keepdims=True))
        o = jnp.concatenate(outs, axis=0)
        g = go_ref[:, vl]
        o_ref[:, vl] = (_rms(o, gn_ref[...]) * (g * jax.nn.sigmoid(g))).astype(o_ref.dtype)


def gla_sample(z, log_a, gla_norm, state, bb=SUBLANES):
    Bs = z.shape[0]
    assert Bs % bb == 0
    q_blk = 3 * ATTN_WIDTH // GLA_KW
    v_blk = (3 * ATTN_WIDTH + 2 * GLA_KW) // GLA_WIDTH
    st_spec = pl.BlockSpec((bb, GLA_HEADS, GLA_DK, GLA_DV), lambda i: (i, 0, 0, 0))
    return pl.pallas_call(
        functools.partial(_gla_sample_body, bb=bb),
        grid=(Bs // bb,),
        in_specs=[
            pl.BlockSpec((bb, GLA_KW), lambda i: (i, q_blk)),
            pl.BlockSpec((bb, GLA_KW), lambda i: (i, q_blk + 1)),
            pl.BlockSpec((bb, GLA_WIDTH), lambda i: (i, v_blk)),
            pl.BlockSpec((bb, GLA_WIDTH), lambda i: (i, v_blk + 1)),
            pl.BlockSpec((bb, GLA_KW), lambda i: (i, 0)),
            pl.BlockSpec((1, GLA_DV), lambda i: (0, 0)),
            st_spec,
        ],
        out_specs=[pl.BlockSpec((bb, GLA_WIDTH), lambda i: (i, 0)), st_spec],
        out_shape=[
            jax.ShapeDtypeStruct((Bs, GLA_WIDTH), BF16),
            jax.ShapeDtypeStruct(state.shape, F32),
        ],
        compiler_params=_cparams("parallel"),
        name="gla_sample",
    )(z, z, z, z, log_a, gla_norm, state)


def _proj_residual_body(a_ref, b_ref, wa_ref, wb_ref, res_ref, o_ref):
    acc = jnp.dot(a_ref[...], wa_ref[...], preferred_element_type=F32)
    acc += jnp.dot(b_ref[...], wb_ref[...], preferred_element_type=F32)
    o_ref[...] = res_ref[...] + acc


def proj_residual(a, b, wa, wb, res, tm, tn=512):
    T, Ka = a.shape
    Kb = b.shape[1]
    N = wa.shape[1]
    assert T % tm == 0 and N % tn == 0
    return pl.pallas_call(
        _proj_residual_body,
        grid=(T // tm, N // tn),
        in_specs=[
            pl.BlockSpec((tm, Ka), lambda i, j: (i, 0)),
            pl.BlockSpec((tm, Kb), lambda i, j: (i, 0)),
            pl.BlockSpec((Ka, tn), lambda i, j: (0, j)),
            pl.BlockSpec((Kb, tn), lambda i, j: (0, j)),
            pl.BlockSpec((tm, tn), lambda i, j: (i, j)),
        ],
        out_specs=pl.BlockSpec((tm, tn), lambda i, j: (i, j)),
        out_shape=jax.ShapeDtypeStruct((T, N), F32),
        compiler_params=_cparams("parallel", "arbitrary"),
        name="proj_residual",
    )(a, b, wa, wb, res)


def _peer_query_body(h_ref, g_ref, w_ref, q_ref, xT_ref, xn_ref):
    @pl.when(pl.program_id(1) == 0)
    def _():
        xn = _rms(h_ref[...], g_ref[...])
        xn_ref[...] = xn.astype(BF16)
        xT_ref[...] = xn.T.astype(BF16)

    q_ref[...] = jnp.dot(xn_ref[...], w_ref[...], preferred_element_type=F32)


def peer_query(h, g, w, tm, tn=512):
    T, D = h.shape
    N = w.shape[1]
    assert T % tm == 0 and N % tn == 0
    return pl.pallas_call(
        _peer_query_body,
        grid=(T // tm, N // tn),
        in_specs=[
            pl.BlockSpec((tm, D), lambda i, j: (i, 0)),
            pl.BlockSpec((1, D), lambda i, j: (0, 0)),
            pl.BlockSpec((D, tn), lambda i, j: (0, j)),
        ],
        out_specs=[
            pl.BlockSpec((tm, tn), lambda i, j: (i, j)),
            pl.BlockSpec((D, tm), lambda i, j: (0, i)),
        ],
        out_shape=[
            jax.ShapeDtypeStruct((T, N), F32),
            jax.ShapeDtypeStruct((D, T), BF16),
        ],
        scratch_shapes=[pltpu.VMEM((tm, D), BF16)],
        compiler_params=_cparams("parallel", "arbitrary"),
        name="peer_query",
    )(h, g, w)


def _sorting_network(n):
    assert n & (n - 1) == 0
    pairs = []
    p = 1
    while p < n:
        k = p
        while k >= 1:
            for j in range(k % p, n - k, 2 * k):
                for i in range(min(k, n - j - k)):
                    if (i + j) // (2 * p) == (i + j + k) // (2 * p):
                        pairs.append((i + j, i + j + k))
            k //= 2
        p *= 2
    return pairs


def _peer_route_body(q_ref, k1_ref, k2_ref, s1_ref, s2_ref, thr_ref, *, tm):
    K = PEER_TOPK
    KR = -(-(K + 1) // SUBLANES) * SUBLANES
    rowk = lax.broadcasted_iota(jnp.int32, (KR, tm), 0)
    row8 = lax.broadcasted_iota(jnp.int32, (SUBLANES, tm), 0)
    rowh = lax.broadcasted_iota(jnp.int32, (PEER_HEADS, tm), 0)
    thr_all = jnp.zeros((PEER_HEADS, tm), F32)

    def top_values(s):
        n = PEER_NKEYS // SUBLANES
        lists = [s[r * SUBLANES:(r + 1) * SUBLANES] for r in range(n)]
        for a, b in _sorting_network(n):
            lists[a], lists[b] = jnp.maximum(lists[a], lists[b]), jnp.minimum(lists[a], lists[b])
        vals = jnp.full((KR, tm), -jnp.inf, F32)
        for it in range(K + 1):
            mx = jnp.max(lists[0], axis=0, keepdims=True)
            vals = jnp.where(rowk == it, mx, vals)
            hit = lists[0] == mx
            for r in range(min(n, K - it)):
                below = lists[r + 1] if r + 1 < n else -jnp.inf
                lists[r] = jnp.where(hit, below, lists[r])
        return vals

    for h in range(PEER_HEADS):
        q1 = q_ref[:, h * PEER_QDIM:h * PEER_QDIM + PEER_HALF]
        q2 = q_ref[:, h * PEER_QDIM + PEER_HALF:(h + 1) * PEER_QDIM]
        s1 = lax.dot_general(k1_ref[h], q1, NT, precision=HIGHEST, preferred_element_type=F32)
        s2 = lax.dot_general(k2_ref[h], q2, NT, precision=HIGHEST, preferred_element_type=F32)
        v1 = top_values(s1)
        v2 = top_values(s2)
        pieces = [v1[0:1] + v2[r:r + SUBLANES] for r in range(0, KR, SUBLANES)]
        for a in range(1, SUBLANES):
            pieces.append(jnp.where(row8 < (K + 1) // (a + 1), v1[a:a + 1] + v2[0:SUBLANES], -jnp.inf))
        pieces += [v1[r:r + SUBLANES] + v2[0:1] for r in range(SUBLANES, KR, SUBLANES)]
        cand = jnp.concatenate(pieces, axis=0)
        m = v1[0:1] + v2[0:1]
        z = jnp.zeros((1, tm), F32)
        kth = m
        for it in range(K):
            kth = jnp.max(cand, axis=0, keepdims=True)
            z = z + jnp.exp(kth - m)
            cand = jnp.where(cand == kth, -jnp.inf, cand)
        nxt = jnp.max(cand, axis=0, keepdims=True)
        c = m + jnp.log(z)
        s1_ref[h] = (s1 - c) * LOG2E - 1.0
        s2_ref[h] = s2 * LOG2E
        thr_all = jnp.where(rowh == h, (0.5 * (kth + nxt) - c) * LOG2E - 1.0, thr_all)
    thr_ref[...] = thr_all


def peer_route(q, keys1, keys2, tm):
    T = q.shape[0]
    assert T % tm == 0 and tm % LANES == 0 and (PEER_TOPK + 1) // 2 <= SUBLANES
    tile = pl.BlockSpec((PEER_HEADS, PEER_NKEYS, tm), lambda i: (0, 0, i))
    tile_shape = jax.ShapeDtypeStruct((PEER_HEADS, PEER_NKEYS, T), F32)
    kspec = pl.BlockSpec((PEER_HEADS, PEER_NKEYS, PEER_HALF), lambda i: (0, 0, 0))
    return pl.pallas_call(
        functools.partial(_peer_route_body, tm=tm),
        grid=(T // tm,),
        in_specs=[pl.BlockSpec((tm, PEER_HEADS * PEER_QDIM), lambda i: (i, 0)), kspec, kspec],
        out_specs=[tile, tile, pl.BlockSpec((PEER_HEADS, tm), lambda i: (0, i))],
        out_shape=[tile_shape, tile_shape, jax.ShapeDtypeStruct((PEER_HEADS, T), F32)],
        compiler_params=_cparams("parallel"),
        name="peer_route",
    )(q, keys1, keys2)


def _peer_dense_body(xT_ref, u_ref, v_ref, s1_ref, s2_ref, thr_ref, h_ref, g_ref, y_ref, *, te, tm):
    e = pl.program_id(1)

    @pl.when(e == 0)
    def _():
        y_ref[...] = h_ref[...]

    rows = []
    for c in range(te // PEER_NKEYS):
        tiles = []
        for tc in range(tm // LANES):
            lanes = pl.ds(tc * LANES, LANES)
            gate = jnp.zeros((PEER_NKEYS, LANES), F32)
            for h in range(PEER_HEADS):
                ssum = s2_ref[h, :, lanes] + s1_ref[h, c:c + 1, lanes]
                gate += jnp.where(ssum >= thr_ref[pl.ds(h, 1), lanes], jnp.exp2(ssum), 0.0)
            tiles.append(gate)
        rows.append(jnp.concatenate(tiles, axis=1))
    half_gate = jnp.concatenate(rows, axis=0)
    hT = jnp.dot(u_ref[...], xT_ref[...], preferred_element_type=F32)
    act = (hT + hT * lax.erf(hT * (2.0 ** -0.5))) * half_gate
    y_ref[...] += jnp.dot(act.T.astype(BF16), v_ref[...], preferred_element_type=F32)

    @pl.when(e == pl.num_programs(1) - 1)
    def _():
        y_ref[...] = _rms(y_ref[...], g_ref[...])


def peer_dense(xT, u, v, s1, s2, thr, h, g, tm):
    D, T = xT.shape
    E = u.shape[0]
    te = SUBLANES * PEER_NKEYS
    assert T % tm == 0 and E % te == 0 and tm % LANES == 0
    return pl.pallas_call(
        functools.partial(_peer_dense_body, te=te, tm=tm),
        grid=(T // tm, E // te),
        in_specs=[
            pl.BlockSpec((D, tm), lambda i, e: (0, i)),
            pl.BlockSpec((te, D), lambda i, e: (e, 0)),
            pl.BlockSpec((te, D), lambda i, e: (e, 0)),
            pl.BlockSpec((PEER_HEADS, SUBLANES, tm), lambda i, e: (0, e, i)),
            pl.BlockSpec((PEER_HEADS, PEER_NKEYS, tm), lambda i, e: (0, 0, i)),
            pl.BlockSpec((PEER_HEADS, tm), lambda i, e: (0, i)),
            pl.BlockSpec((tm, D), lambda i, e: (i, 0)),
            pl.BlockSpec((1, D), lambda i, e: (0, 0)),
        ],
        out_specs=pl.BlockSpec((tm, D), lambda i, e: (i, 0)),
        out_shape=jax.ShapeDtypeStruct((T, D), F32),
        compiler_params=_cparams("parallel", "arbitrary"),
        name="peer_dense",
    )(xT, u, v, s1, s2, thr, h, g)


def _post_mix(x, o_attn, o_gla, w, tm, tm_proj):
    h = proj_residual(o_attn, o_gla, w["w_o_a"], w["w_o_b"], x, tm_proj)
    qp, xT = peer_query(h, w["norm_ffn"], w["peer_query"], tm)
    s1, s2, thr = peer_route(qp, w["peer_keys_1"], w["peer_keys_2"], tm)
    return peer_dense(xT, w["expert_u"], w["expert_v"], s1, s2, thr, h, w["final_norm"], tm)


def _prompt_group(x_prompt, w):
    B, S, _ = x_prompt.shape
    x = x_prompt.reshape(B * S, D_MODEL)
    tm, tm_proj = 512, 1024
    z, log_a = in_proj(x, w, jnp.arange(S), tm_proj)
    o_attn = attn_prompt(z, B, S)
    o_gla, state = gla_prompt(z, log_a, w["gla_norm"], B, S)
    out = _post_mix(x, o_attn, o_gla, w, tm, tm_proj)
    return out, z, state


def _sample_group(x_sample, cache_k, cache_v, state, w):
    Bs, Ls, _ = x_sample.shape
    assert Ls == 1
    x = x_sample.reshape(Bs, D_MODEL)
    tm = Bs
    z, log_a = in_proj(x, w, jnp.full((tm,), PAST_LEN, jnp.int32), tm)
    o_attn = attn_sample(z, cache_k, cache_v)
    o_gla, new_state = gla_sample(z, log_a, w["gla_norm"], state)
    out = _post_mix(x, o_attn, o_gla, w, tm, tm)
    return out, z, new_state


def _prepare(norm_attn, w_in, w_gate_up, b_gate, gla_norm, w_o, norm_ffn, peer_query_w, peer_keys_1,
             peer_keys_2, expert_u, expert_v, final_norm_w):
    assert w_in.shape[0] == 1
    l = 0
    w_in_l = w_in[l]
    row = lambda v: v.reshape(1, -1).astype(F32)
    return {
        "norm_attn": row(norm_attn[l]),
        "w_in_main": w_in_l[:, :MAIN_COLS].astype(BF16),
        "w_in_lr": jnp.pad(w_in_l[:, MAIN_COLS:], ((0, 0), (0, LANES - GLA_LOWRANK))).astype(BF16),
        "w_gate_up": jnp.pad(w_gate_up[l], ((0, LANES - GLA_LOWRANK), (0, 0))),
        "b_gate": row(b_gate[l]),
        "gla_norm": row(gla_norm[l]),
        "w_o_a": w_o[l][:ATTN_WIDTH].astype(BF16),
        "w_o_b": w_o[l][ATTN_WIDTH:].astype(BF16),
        "norm_ffn": row(norm_ffn[l]),
        "peer_query": peer_query_w[l].astype(BF16),
        "peer_keys_1": peer_keys_1[l],
        "peer_keys_2": peer_keys_2[l],
        "expert_u": expert_u[l].astype(BF16),
        "expert_v": expert_v[l].astype(BF16),
        "final_norm": row(final_norm_w),
    }


def kernel(x_prompt, x_sample, cache_attn_k, cache_attn_v, state_gla, norm_attn, w_in, w_gate_up,
           b_gate, gla_norm, w_o, norm_ffn, peer_query, peer_keys_1, peer_keys_2, expert_u,
           expert_v, final_norm):
    Bp, Lp, _ = x_prompt.shape
    Bs, Ls, _ = x_sample.shape
    w = _prepare(norm_attn, w_in, w_gate_up, b_gate, gla_norm, w_o, norm_ffn, peer_query, peer_keys_1,
                 peer_keys_2, expert_u, expert_v, final_norm)
    win_p = min(WINDOW_MAX, Lp)
    yp, zp, sp = _prompt_group(x_prompt, w)
    ys, zs, ss = _sample_group(x_sample, cache_attn_k[0], cache_attn_v[0], state_gla[0], w)

    def heads(z, lo, B, L):
        return z[:, lo:lo + ATTN_WIDTH].reshape(B, L, ATTN_HEADS, HEAD_DIM)

    kp = heads(zp, ATTN_WIDTH, Bp, Lp)[:, Lp - win_p:]
    vp = heads(zp, 2 * ATTN_WIDTH, Bp, Lp)[:, Lp - win_p:]
    return (yp.reshape(Bp, Lp, D_MODEL), ys.reshape(Bs, Ls, D_MODEL),
            kp[None], vp[None], sp[None],
            heads(zs, ATTN_WIDTH, Bs, Ls)[None], heads(zs, 2 * ATTN_WIDTH, Bs, Ls)[None], ss[None])


def sample_group(inp):
    names = ("norm_attn", "w_in", "w_gate_up", "b_gate", "gla_norm", "w_o", "norm_ffn", "peer_query",
             "peer_keys_1", "peer_keys_2", "expert_u", "expert_v", "final_norm")
    w = _prepare(*[inp[n] for n in names])
    out, z, st = _sample_group(inp["x_sample"], inp["cache_attn_k"][0], inp["cache_attn_v"][0],
                               inp["state_gla"][0], w)
    Bs = out.shape[0]
    return (out, z[:, ATTN_WIDTH:2 * ATTN_WIDTH].reshape(Bs, 1, ATTN_HEADS, HEAD_DIM),
            z[:, 2 * ATTN_WIDTH:3 * ATTN_WIDTH].reshape(Bs, 1, ATTN_HEADS, HEAD_DIM), st)
```

```python
import functools
import math

import jax
import jax.numpy as jnp
import numpy as np
from jax import lax
from jax.experimental import pallas as pl
from jax.experimental.pallas import tpu as pltpu

F32 = jnp.float32
BF16 = jnp.bfloat16

D_MODEL = 2048
PAST_LEN = 2048
HEAD_DIM = 128
ATTN_HEADS = 8
ATTN_WIDTH = ATTN_HEADS * HEAD_DIM
DILATIONS = ((128, 1), (512, 4), (2048, 16))
WINDOW_MAX = 2048
QUERY_BLOCK = 128
ATTN_SCALE = HEAD_DIM ** -0.5
ROPE_THETA = 500000.0
ROT_DIM = HEAD_DIM // 4
ROT_HALF = ROT_DIM // 2
GLA_HEADS = 4
GLA_WIDTH = D_MODEL - ATTN_WIDTH
GLA_DV = GLA_WIDTH // GLA_HEADS
GLA_DK = GLA_DV // 2
GLA_KW = GLA_HEADS * GLA_DK
GLA_LOWRANK = 16
GLA_TAU = 16.0
GLA_SUB = 16
PEER_HEADS = 8
PEER_NKEYS = 128
PEER_EXPERTS = PEER_NKEYS * PEER_NKEYS
PEER_QDIM = 256
PEER_HALF = PEER_QDIM // 2
PEER_TOPK = 16
NORM_EPS = 1e-6
MAIN_COLS = 3 * ATTN_WIDTH + 2 * GLA_KW + 2 * GLA_WIDTH

LANES = 128
SUBLANES = 8
VMEM_LIMIT = 56 * 1024 * 1024
NEG = -1e30
LOG2E = 1.0 / math.log(2.0)
HIGHEST = lax.Precision.HIGHEST
NT = (((1,), (1,)), ((), ()))
TN = (((0,), (0,)), ((), ()))


def _cparams(*sem):
    return pltpu.CompilerParams(dimension_semantics=sem, vmem_limit_bytes=VMEM_LIMIT)


def _rms(x, g):
    return x * lax.rsqrt(jnp.mean(x * x, axis=-1, keepdims=True) + NORM_EPS) * g


def _in_proj_body(x_ref, g_ref, w_ref, wlr_ref, wup_ref, bg_ref, cos_ref, sin_ref,
                  z_ref, la_ref, kwin_ref, vwin_ref, xn_ref, *, tn, nb, wb):
    i = pl.program_id(0)
    j = pl.program_id(1)
    in_win = i % nb >= nb - wb
    kv_tiles = ATTN_WIDTH // tn

    @pl.when(j == 0)
    def _():
        xn = _rms(x_ref[...], g_ref[...]).astype(BF16)
        xn_ref[...] = xn
        lr = jnp.dot(xn, wlr_ref[...], preferred_element_type=F32)
        zg = jnp.dot(lr, wup_ref[...], precision=HIGHEST, preferred_element_type=F32) + bg_ref[...]
        la_ref[...] = (jnp.minimum(zg, 0.0) - jnp.log1p(jnp.exp(-jnp.abs(zg)))) * (1.0 / GLA_TAU)

    acc = jnp.dot(xn_ref[...], w_ref[...], preferred_element_type=F32)

    @pl.when(j < 2 * ATTN_WIDTH // tn)
    def _():
        lane = lax.broadcasted_iota(jnp.int32, (acc.shape[0], HEAD_DIM), 1)
        cos = cos_ref[...]
        sin = sin_ref[...]
        for c in range(tn // HEAD_DIM):
            a = acc[:, c * HEAD_DIM:(c + 1) * HEAD_DIM]
            partner = jnp.where(lane < ROT_HALF, pltpu.roll(a, HEAD_DIM - ROT_HALF, 1),
                                pltpu.roll(a, ROT_HALF, 1))
            z_ref[:, c * HEAD_DIM:(c + 1) * HEAD_DIM] = a * cos + partner * sin

        @pl.when(jnp.logical_and(in_win, j >= kv_tiles))
        def _():
            kwin_ref[...] = z_ref[...]

    @pl.when(j >= 2 * ATTN_WIDTH // tn)
    def _():
        z_ref[...] = acc

        @pl.when(jnp.logical_and(in_win, j < 3 * kv_tiles))
        def _():
            vwin_ref[...] = acc


def _rotary_tables(pos):
    inv = jnp.exp(-math.log(ROPE_THETA) * jnp.arange(ROT_HALF, dtype=F32) * (2.0 / ROT_DIM))
    ang = pos.astype(F32)[:, None] * inv[None, :]
    n = pos.shape[0]
    cos = jnp.concatenate([jnp.cos(ang), jnp.cos(ang), jnp.ones((n, HEAD_DIM - ROT_DIM), F32)], axis=1)
    sin = jnp.concatenate([-jnp.sin(ang), jnp.sin(ang), jnp.zeros((n, HEAD_DIM - ROT_DIM), F32)], axis=1)
    return cos, sin


def in_proj(x, w, pos, tm, win, tn=512):
    T, D = x.shape
    P = pos.shape[0]
    assert T % tm == 0 and MAIN_COLS % tn == 0 and P % tm == 0 and ATTN_WIDTH % tn == 0 and win % tm == 0
    nb, wb = P // tm, win // tm
    kv_tiles = ATTN_WIDTH // tn
    cos, sin = _rotary_tables(pos)
    const = lambda i, j: (0, 0)

    def win_spec(first_tile):
        def index(i, j):
            ib = i % nb
            row = (i // nb) * wb + jnp.maximum(ib - (nb - wb), 0)
            col = jnp.where(ib >= nb - wb, jnp.clip(j - first_tile, 0, kv_tiles - 1), 0)
            return row, col
        return pl.BlockSpec((tm, tn), index)

    win_shape = jax.ShapeDtypeStruct((T // P * win, ATTN_WIDTH), F32)
    return pl.pallas_call(
        functools.partial(_in_proj_body, tn=tn, nb=nb, wb=wb),
        grid=(T // tm, MAIN_COLS // tn),
        in_specs=[
            pl.BlockSpec((tm, D), lambda i, j: (i, 0)),
            pl.BlockSpec((1, D), const),
            pl.BlockSpec((D, tn), lambda i, j: (0, j)),
            pl.BlockSpec((D, LANES), const),
            pl.BlockSpec((LANES, GLA_KW), const),
            pl.BlockSpec((1, GLA_KW), const),
            pl.BlockSpec((tm, HEAD_DIM), lambda i, j: (i % (P // tm), 0)),
            pl.BlockSpec((tm, HEAD_DIM), lambda i, j: (i % (P // tm), 0)),
        ],
        out_specs=[
            pl.BlockSpec((tm, tn), lambda i, j: (i, j)),
            pl.BlockSpec((tm, GLA_KW), lambda i, j: (i, 0)),
            win_spec(kv_tiles),
            win_spec(2 * kv_tiles),
        ],
        out_shape=[
            jax.ShapeDtypeStruct((T, MAIN_COLS), F32),
            jax.ShapeDtypeStruct((T, GLA_KW), F32),
            win_shape,
            win_shape,
        ],
        scratch_shapes=[pltpu.VMEM((tm, D), BF16)],
        compiler_params=_cparams("arbitrary", "arbitrary"),
        name="in_proj",
    )(x, w["norm_attn"], w["w_in_main"], w["w_in_lr"], w["w_gate_up"], w["b_gate"], cos, sin)


def _attn_prompt_body(q_ref, k_ref, v_ref, o_ref, ob_ref, lse_ref, s_ref, p_ref, m_ref, *, S):
    QB = QUERY_BLOCK
    row = lax.broadcasted_iota(jnp.int32, (QB, QB), 0)
    col = lax.broadcasted_iota(jnp.int32, (QB, QB), 1)
    bias = jnp.concatenate([jnp.where(col <= row, 0.0, NEG), jnp.where(col >= row, 0.0, NEG)], axis=1)
    lane2 = lax.broadcasted_iota(jnp.int32, (QB, 2 * QB), 1)
    ones = jnp.ones((QB, HEAD_DIM), BF16)

    for bi, (window, d) in enumerate(DILATIONS):
        assert window == d * QB
        nblk = S // d // QB
        assert d * nblk == s_ref.shape[0]

        def windows(idx, d=d, nblk=nblk):
            r = idx // nblk
            i = idx % nblk
            start = r + i * (d * QB)
            if d == 1:
                return i, pl.ds(pl.multiple_of(start, QB), QB)
            return i, pl.ds(start, QB, stride=d)

        def scores(idx, k_prev):
            i, rows = windows(idx)
            qs = (q_ref[rows, :] * ATTN_SCALE).astype(BF16)
            k_cur = k_ref[rows, :].astype(BF16)
            kk = jnp.concatenate([k_cur, k_prev], axis=0)
            s = lax.dot_general(qs, kk, NT, preferred_element_type=F32) + bias
            s_ref[idx] = jnp.where(jnp.logical_or(lane2 < QB, i > 0), s, NEG)
            return k_cur

        def softmax(idx, carry):
            s = s_ref[idx]
            m = jnp.max(jnp.maximum(s[:, :QB], s[:, QB:]), axis=1, keepdims=True)
            p_ref[idx] = jnp.exp(s - m).astype(BF16)
            m_ref[idx] = jnp.broadcast_to(m, (QB, HEAD_DIM))
            return carry

        def values(idx, v_prev, bi=bi):
            i, rows = windows(idx)
            v_cur = jnp.concatenate([v_ref[rows, :].astype(BF16), ones], axis=1)
            vv = jnp.concatenate([v_cur, v_prev], axis=0)
            o = jnp.dot(p_ref[idx], vv, preferred_element_type=F32)
            den = o[:, HEAD_DIM:]
            ob_ref[bi, rows, :] = o[:, :HEAD_DIM] / den
            lse_ref[bi, rows, :] = m_ref[idx] + jnp.log(den)
            return v_cur

        lax.fori_loop(0, d * nblk, scores, jnp.zeros((QB, HEAD_DIM), BF16), unroll=4)
        lax.fori_loop(0, d * nblk, softmax, 0, unroll=4)
        lax.fori_loop(0, d * nblk, values, jnp.zeros((QB, 2 * HEAD_DIM), BF16), unroll=4)

    CH = 256

    def combine(c, carry):
        rows = pl.ds(pl.multiple_of(c * CH, CH), CH)
        l0, l1, l2 = lse_ref[0, rows, :], lse_ref[1, rows, :], lse_ref[2, rows, :]
        mx = jnp.maximum(jnp.maximum(l0, l1), l2)
        w0, w1, w2 = jnp.exp(l0 - mx), jnp.exp(l1 - mx), jnp.exp(l2 - mx)
        num = w0 * ob_ref[0, rows, :] + w1 * ob_ref[1, rows, :] + w2 * ob_ref[2, rows, :]
        o_ref[rows, :] = (num / (w0 + w1 + w2)).astype(o_ref.dtype)
        return carry

    lax.fori_loop(0, S // CH, combine, 0)


def attn_prompt(z, B, S):
    H = ATTN_HEADS
    assert S % (DILATIONS[-1][1] * QUERY_BLOCK) == 0
    return pl.pallas_call(
        functools.partial(_attn_prompt_body, S=S),
        grid=(B, H),
        in_specs=[
            pl.BlockSpec((S, HEAD_DIM), lambda b, h: (b, h)),
            pl.BlockSpec((S, HEAD_DIM), lambda b, h: (b, H + h)),
            pl.BlockSpec((S, HEAD_DIM), lambda b, h: (b, 2 * H + h)),
        ],
        out_specs=pl.BlockSpec((S, HEAD_DIM), lambda b, h: (b, h)),
        out_shape=jax.ShapeDtypeStruct((B * S, ATTN_WIDTH), BF16),
        scratch_shapes=[
            pltpu.VMEM((len(DILATIONS), S, HEAD_DIM), F32),
            pltpu.VMEM((len(DILATIONS), S, HEAD_DIM), F32),
            pltpu.VMEM((S // QUERY_BLOCK, QUERY_BLOCK, 2 * QUERY_BLOCK), F32),
            pltpu.VMEM((S // QUERY_BLOCK, QUERY_BLOCK, 2 * QUERY_BLOCK), BF16),
            pltpu.VMEM((S // QUERY_BLOCK, QUERY_BLOCK, HEAD_DIM), F32),
        ],
        compiler_params=_cparams("parallel", "parallel"),
        name="attn_prompt",
    )(z, z, z)


def _attn_sample_body(z_ref, k1_ref, k4_ref, k16_ref, v1_ref, v4_ref, v16_ref, o_ref, *, bb):
    H = ATTN_HEADS
    for b in range(bb):
        q = z_ref[b, 0:H, :] * ATTN_SCALE
        k_new = z_ref[b, H:2 * H, :]
        v_new = z_ref[b, 2 * H:3 * H, :]
        s_new = jnp.sum(k_new * q, axis=-1, keepdims=True)
        s_win = [jnp.sum(kr[b] * q, axis=-1, keepdims=True)
                 for kr in (k1_ref, k4_ref, k16_ref)]
        m = s_new
        for s in s_win:
            m = jnp.maximum(m, jnp.max(s, axis=0))
        p_new = jnp.exp(s_new - m) * float(len(DILATIONS))
        den = p_new
        acc = p_new * v_new
        for s, vr in zip(s_win, (v1_ref, v4_ref, v16_ref)):
            p = jnp.exp(s - m)
            den = den + jnp.sum(p, axis=0)
            acc = acc + jnp.sum(p * vr[b], axis=0)
        o_ref[b] = acc / den


def attn_sample(z, cache_k, cache_v, bb=4):
    Bs = z.shape[0]
    wbuf = cache_k.shape[1]
    QB, H = QUERY_BLOCK, ATTN_HEADS
    assert wbuf == PAST_LEN == WINDOW_MAX and Bs % bb == 0

    def views(c):
        out, specs = [], []
        for window, d in DILATIONS:
            n = wbuf // d
            if d == 1:
                out.append(c)
                specs.append(pl.BlockSpec((bb, QB, H, HEAD_DIM), lambda i, n=n: (i, n // QB - 1, 0, 0)))
            else:
                out.append(c.reshape(Bs, n, d, H, HEAD_DIM))
                specs.append(pl.BlockSpec((bb, QB, None, H, HEAD_DIM),
                                          lambda i, n=n: (i, n // QB - 1, 0, 0, 0)))
        return out, specs

    kv, kspecs = views(cache_k)
    vv, vspecs = views(cache_v)
    z3 = z.reshape(Bs, MAIN_COLS // HEAD_DIM, HEAD_DIM)
    out = pl.pallas_call(
        functools.partial(_attn_sample_body, bb=bb),
        grid=(Bs // bb,),
        in_specs=[pl.BlockSpec((bb, MAIN_COLS // HEAD_DIM, HEAD_DIM), lambda i: (i, 0, 0))] + kspecs + vspecs,
        out_specs=pl.BlockSpec((bb, H, HEAD_DIM), lambda i: (i, 0, 0)),
        out_shape=jax.ShapeDtypeStruct((Bs, H, HEAD_DIM), F32),
        compiler_params=_cparams("parallel"),
        name="attn_sample",
    )(z3, *kv, *vv)
    return out.reshape(Bs, ATTN_WIDTH).astype(BF16)


def _gla_prompt_body(q_ref, k_ref, v_ref, go_ref, la_ref, gn_ref, o_ref, st_ref, sT_ref, b_ref, oi_ref,
                     *, C):
    c = pl.program_id(1)
    nsub = C // GLA_SUB

    @pl.when(c == 0)
    def _():
        sT_ref[...] = jnp.zeros_like(sT_ref)

    tri = (lax.broadcasted_iota(jnp.int32, (C, C), 1)
           <= lax.broadcasted_iota(jnp.int32, (C, C), 0)).astype(F32)
    srow = lax.broadcasted_iota(jnp.int32, (GLA_SUB, 1), 0)

    for h in range(GLA_HEADS):
        kl = pl.ds(h * GLA_DK, GLA_DK)
        vl = pl.ds(h * GLA_DV, GLA_DV)
        b = jnp.dot(tri, la_ref[:, kl], precision=HIGHEST, preferred_element_type=F32)
        b_ref[h] = b
        qh = q_ref[:, kl] * (GLA_DK ** -0.5)
        kh = k_ref[:, kl]
        vh = v_ref[:, vl]
        b_end = b[C - 1:C, :]
        sT = sT_ref[h]
        oi_ref[h] = lax.dot_general((qh * jnp.exp(b)).astype(BF16), sT.astype(BF16), NT,
                                    preferred_element_type=F32)
        k_end = (kh * jnp.exp(b_end - b)).astype(BF16)
        sT_ref[h] = sT * jnp.exp(b_end) + lax.dot_general(vh.astype(BF16), k_end, TN,
                                                          preferred_element_type=F32)
        for I in range(1, nsub):
            r0 = I * GLA_SUB
            cI = b[r0 - 1:r0, :]
            qI = (qh[r0:r0 + GLA_SUB] * jnp.exp(b[r0:r0 + GLA_SUB] - cI)).astype(BF16)
            kI = (kh[:r0] * jnp.exp(cI - b[:r0])).astype(BF16)
            a = lax.dot_general(qI, kI, NT, preferred_element_type=F32)
            oi_ref[h, r0:r0 + GLA_SUB, :] += jnp.dot(a.astype(BF16), vh[:r0].astype(BF16),
                                                     preferred_element_type=F32)

    def diag(I, carry):
        rows = pl.ds(pl.multiple_of(I * GLA_SUB, GLA_SUB), GLA_SUB)
        for h in range(GLA_HEADS):
            bI = b_ref[h, rows, :]
            qI = q_ref[rows, pl.ds(h * GLA_DK, GLA_DK)] * (GLA_DK ** -0.5)
            kI = k_ref[rows, pl.ds(h * GLA_DK, GLA_DK)]
            vI = v_ref[rows, pl.ds(h * GLA_DV, GLA_DV)]
            acc = jnp.zeros((GLA_SUB, GLA_DV), F32)
            for s in range(GLA_SUB):
                e = jnp.exp(jnp.minimum(bI - bI[s:s + 1, :], 0.0))
                a_col = jnp.sum(qI * kI[s:s + 1, :] * e, axis=1, keepdims=True)
                a_col = jnp.where(srow >= s, a_col, 0.0)
                acc += a_col * vI[s:s + 1, :]
            oi_ref[h, rows, :] += acc
        return carry

    lax.fori_loop(0, nsub, diag, 0)

    for h in range(GLA_HEADS):
        vl = pl.ds(h * GLA_DV, GLA_DV)
        g = go_ref[:, vl]
        o_ref[:, vl] = (_rms(oi_ref[h], gn_ref[...]) * (g * jax.nn.sigmoid(g))).astype(o_ref.dtype)

    @pl.when(c == pl.num_programs(1) - 1)
    def _():
        for h in range(GLA_HEADS):
            st_ref[0, h] = sT_ref[h].T


def gla_prompt(z, log_a, gla_norm, B, S, C=128):
    assert S % C == 0 and C % GLA_SUB == 0
    nc = S // C
    q_blk = 3 * ATTN_WIDTH // GLA_KW
    v_blk = (3 * ATTN_WIDTH + 2 * GLA_KW) // GLA_WIDTH
    return pl.pallas_call(
        functools.partial(_gla_prompt_body, C=C),
        grid=(B, nc),
        in_specs=[
            pl.BlockSpec((C, GLA_KW), lambda b, c: (b * nc + c, q_blk)),
            pl.BlockSpec((C, GLA_KW), lambda b, c: (b * nc + c, q_blk + 1)),
            pl.BlockSpec((C, GLA_WIDTH), lambda b, c: (b * nc + c, v_blk)),
            pl.BlockSpec((C, GLA_WIDTH), lambda b, c: (b * nc + c, v_blk + 1)),
            pl.BlockSpec((C, GLA_KW), lambda b, c: (b * nc + c, 0)),
            pl.BlockSpec((1, GLA_DV), lambda b, c: (0, 0)),
        ],
        out_specs=[
            pl.BlockSpec((C, GLA_WIDTH), lambda b, c: (b * nc + c, 0)),
            pl.BlockSpec((1, GLA_HEADS, GLA_DK, GLA_DV), lambda b, c: (b, 0, 0, 0)),
        ],
        out_shape=[
            jax.ShapeDtypeStruct((B * S, GLA_WIDTH), BF16),
            jax.ShapeDtypeStruct((B, GLA_HEADS, GLA_DK, GLA_DV), F32),
        ],
        scratch_shapes=[
            pltpu.VMEM((GLA_HEADS, GLA_DV, GLA_DK), F32),
            pltpu.VMEM((GLA_HEADS, C, GLA_DK), F32),
            pltpu.VMEM((GLA_HEADS, C, GLA_DV), F32),
        ],
        compiler_params=_cparams("parallel", "arbitrary"),
        name="gla_prompt",
    )(z, z, z, z, log_a, gla_norm)


def _gla_sample_body(q_ref, k_ref, v_ref, go_ref, la_ref, gn_ref, s_ref, o_ref, so_ref, *, bb):
    for h in range(GLA_HEADS):
        kl = pl.ds(h * GLA_DK, GLA_DK)
        vl = pl.ds(h * GLA_DV, GLA_DV)
        aT = jnp.exp(la_ref[:, kl]).T
        kT = k_ref[:, kl].T
        qT = (q_ref[:, kl] * (GLA_DK ** -0.5)).T
        outs = []
        for b in range(bb):
            s_new = aT[:, b:b + 1] * s_ref[b, h] + kT[:, b:b + 1] * v_ref[b:b + 1, vl]
            so_ref[b, h] = s_new
            outs.append(jnp.sum(qT[:, b:b + 1] * s_new, axis=0, keepdims=True))
        o = jnp.concatenate(outs, axis=0)
        g = go_ref[:, vl]
        o_ref[:, vl] = (_rms(o, gn_ref[...]) * (g * jax.nn.sigmoid(g))).astype(o_ref.dtype)


def gla_sample(z, log_a, gla_norm, state, bb=SUBLANES):
    Bs = z.shape[0]
    assert Bs % bb == 0
    q_blk = 3 * ATTN_WIDTH // GLA_KW
    v_blk = (3 * ATTN_WIDTH + 2 * GLA_KW) // GLA_WIDTH
    st_spec = pl.BlockSpec((bb, GLA_HEADS, GLA_DK, GLA_DV), lambda i: (i, 0, 0, 0))
    return pl.pallas_call(
        functools.partial(_gla_sample_body, bb=bb),
        grid=(Bs // bb,),
        in_specs=[
            pl.BlockSpec((bb, GLA_KW), lambda i: (i, q_blk)),
            pl.BlockSpec((bb, GLA_KW), lambda i: (i, q_blk + 1)),
            pl.BlockSpec((bb, GLA_WIDTH), lambda i: (i, v_blk)),
            pl.BlockSpec((bb, GLA_WIDTH), lambda i: (i, v_blk + 1)),
            pl.BlockSpec((bb, GLA_KW), lambda i: (i, 0)),
            pl.BlockSpec((1, GLA_DV), lambda i: (0, 0)),
            st_spec,
        ],
        out_specs=[pl.BlockSpec((bb, GLA_WIDTH), lambda i: (i, 0)), st_spec],
        out_shape=[
            jax.ShapeDtypeStruct((Bs, GLA_WIDTH), BF16),
            jax.ShapeDtypeStruct(state.shape, F32),
        ],
        compiler_params=_cparams("parallel"),
        name="gla_sample",
    )(z, z, z, z, log_a, gla_norm, state)


def _proj_residual_body(a_ref, b_ref, wa_ref, wb_ref, res_ref, o_ref):
    acc = jnp.dot(a_ref[...], wa_ref[...], preferred_element_type=F32)
    acc += jnp.dot(b_ref[...], wb_ref[...], preferred_element_type=F32)
    o_ref[...] = res_ref[...] + acc


def proj_residual(a, b, wa, wb, res, tm, tn=512):
    T, Ka = a.shape
    Kb = b.shape[1]
    N = wa.shape[1]
    assert T % tm == 0 and N % tn == 0
    return pl.pallas_call(
        _proj_residual_body,
        grid=(T // tm, N // tn),
        in_specs=[
            pl.BlockSpec((tm, Ka), lambda i, j: (i, 0)),
            pl.BlockSpec((tm, Kb), lambda i, j: (i, 0)),
            pl.BlockSpec((Ka, tn), lambda i, j: (0, j)),
            pl.BlockSpec((Kb, tn), lambda i, j: (0, j)),
            pl.BlockSpec((tm, tn), lambda i, j: (i, j)),
        ],
        out_specs=pl.BlockSpec((tm, tn), lambda i, j: (i, j)),
        out_shape=jax.ShapeDtypeStruct((T, N), F32),
        compiler_params=_cparams("parallel", "arbitrary"),
        name="proj_residual",
    )(a, b, wa, wb, res)


def _peer_query_body(h_ref, g_ref, w_ref, q_ref, xT_ref, xn_ref):
    @pl.when(pl.program_id(1) == 0)
    def _():
        xn = _rms(h_ref[...], g_ref[...])
        xn_ref[...] = xn.astype(BF16)
        xT_ref[...] = xn.T.astype(BF16)

    q_ref[...] = jnp.dot(xn_ref[...], w_ref[...], preferred_element_type=F32)


def peer_query(h, g, w, tm, tn=512):
    T, D = h.shape
    N = w.shape[1]
    assert T % tm == 0 and N % tn == 0
    return pl.pallas_call(
        _peer_query_body,
        grid=(T // tm, N // tn),
        in_specs=[
            pl.BlockSpec((tm, D), lambda i, j: (i, 0)),
            pl.BlockSpec((1, D), lambda i, j: (0, 0)),
            pl.BlockSpec((D, tn), lambda i, j: (0, j)),
        ],
        out_specs=[
            pl.BlockSpec((tm, tn), lambda i, j: (i, j)),
            pl.BlockSpec((D, tm), lambda i, j: (0, i)),
        ],
        out_shape=[
            jax.ShapeDtypeStruct((T, N), F32),
            jax.ShapeDtypeStruct((D, T), BF16),
        ],
        scratch_shapes=[pltpu.VMEM((tm, D), BF16)],
        compiler_params=_cparams("parallel", "arbitrary"),
        name="peer_query",
    )(h, g, w)


def _sorting_network(n):
    assert n & (n - 1) == 0
    pairs = []
    p = 1
    while p < n:
        k = p
        while k >= 1:
            for j in range(k % p, n - k, 2 * k):
                for i in range(min(k, n - j - k)):
                    if (i + j) // (2 * p) == (i + j + k) // (2 * p):
                        pairs.append((i + j, i + j + k))
            k //= 2
        p *= 2
    return pairs


def _peer_route_body(q_ref, k1_ref, k2_ref, s1_ref, s2_ref, thr_ref, *, tm):
    K = PEER_TOPK
    KR = -(-(K + 1) // SUBLANES) * SUBLANES
    rowk = lax.broadcasted_iota(jnp.int32, (KR, tm), 0)
    row8 = lax.broadcasted_iota(jnp.int32, (SUBLANES, tm), 0)
    rowh = lax.broadcasted_iota(jnp.int32, (PEER_HEADS, tm), 0)
    thr_all = jnp.zeros((PEER_HEADS, tm), F32)

    def top_values(s):
        n = PEER_NKEYS // SUBLANES
        lists = [s[r * SUBLANES:(r + 1) * SUBLANES] for r in range(n)]
        for a, b in _sorting_network(n):
            lists[a], lists[b] = jnp.maximum(lists[a], lists[b]), jnp.minimum(lists[a], lists[b])
        vals = jnp.full((KR, tm), -jnp.inf, F32)
        for it in range(K + 1):
            mx = jnp.max(lists[0], axis=0, keepdims=True)
            vals = jnp.where(rowk == it, mx, vals)
            hit = lists[0] == mx
            for r in range(min(n, K - it)):
                below = lists[r + 1] if r + 1 < n else -jnp.inf
                lists[r] = jnp.where(hit, below, lists[r])
        return vals

    for h in range(PEER_HEADS):
        q1 = q_ref[:, h * PEER_QDIM:h * PEER_QDIM + PEER_HALF]
        q2 = q_ref[:, h * PEER_QDIM + PEER_HALF:(h + 1) * PEER_QDIM]
        s1 = lax.dot_general(k1_ref[h], q1, NT, precision=HIGHEST, preferred_element_type=F32)
        s2 = lax.dot_general(k2_ref[h], q2, NT, precision=HIGHEST, preferred_element_type=F32)
        v1 = top_values(s1)
        v2 = top_values(s2)
        pieces = [v1[0:1] + v2[r:r + SUBLANES] for r in range(0, KR, SUBLANES)]
        for a in range(1, SUBLANES):
            pieces.append(jnp.where(row8 < (K + 1) // (a + 1), v1[a:a + 1] + v2[0:SUBLANES], -jnp.inf))
        pieces += [v1[r:r + SUBLANES] + v2[0:1] for r in range(SUBLANES, KR, SUBLANES)]
        cand = jnp.concatenate(pieces, axis=0)
        m = v1[0:1] + v2[0:1]
        z = jnp.zeros((1, tm), F32)
        kth = m
        for it in range(K):
            kth = jnp.max(cand, axis=0, keepdims=True)
            z = z + jnp.exp(kth - m)
            cand = jnp.where(cand == kth, -jnp.inf, cand)
        nxt = jnp.max(cand, axis=0, keepdims=True)
        c = m + jnp.log(z)
        s1_ref[h] = (s1 - c) * LOG2E - 1.0
        s2_ref[h] = s2 * LOG2E
        thr_all = jnp.where(rowh == h, (0.5 * (kth + nxt) - c) * LOG2E - 1.0, thr_all)
    thr_ref[...] = thr_all


def peer_route(q, keys1, keys2, tm):
    T = q.shape[0]
    assert T % tm == 0 and tm % LANES == 0 and (PEER_TOPK + 1) // 2 <= SUBLANES
    tile = pl.BlockSpec((PEER_HEADS, PEER_NKEYS, tm), lambda i: (0, 0, i))
    tile_shape = jax.ShapeDtypeStruct((PEER_HEADS, PEER_NKEYS, T), F32)
    kspec = pl.BlockSpec((PEER_HEADS, PEER_NKEYS, PEER_HALF), lambda i: (0, 0, 0))
    return pl.pallas_call(
        functools.partial(_peer_route_body, tm=tm),
        grid=(T // tm,),
        in_specs=[pl.BlockSpec((tm, PEER_HEADS * PEER_QDIM), lambda i: (i, 0)), kspec, kspec],
        out_specs=[tile, tile, pl.BlockSpec((PEER_HEADS, tm), lambda i: (0, i))],
        out_shape=[tile_shape, tile_shape, jax.ShapeDtypeStruct((PEER_HEADS, T), F32)],
        compiler_params=_cparams("parallel"),
        name="peer_route",
    )(q, keys1, keys2)


def _peer_dense_body(xT_ref, u_ref, v_ref, s1_ref, s2_ref, thr_ref, h_ref, g_ref, y_ref, *, te, tm):
    e = pl.program_id(1)

    @pl.when(e == 0)
    def _():
        y_ref[...] = h_ref[...]

    rows = []
    for c in range(te // PEER_NKEYS):
        tiles = []
        for tc in range(tm // LANES):
            lanes = pl.ds(tc * LANES, LANES)
            gate = jnp.zeros((PEER_NKEYS, LANES), F32)
            for h in range(PEER_HEADS):
                ssum = s2_ref[h, :, lanes] + s1_ref[h, c:c + 1, lanes]
                gate += jnp.where(ssum >= thr_ref[pl.ds(h, 1), lanes], jnp.exp2(ssum), 0.0)
            tiles.append(gate)
        rows.append(jnp.concatenate(tiles, axis=1))
    half_gate = jnp.concatenate(rows, axis=0)
    hT = jnp.dot(u_ref[...], xT_ref[...], preferred_element_type=F32)
    act = (hT + hT * lax.erf(hT * (2.0 ** -0.5))) * half_gate
    y_ref[...] += jnp.dot(act.T.astype(BF16), v_ref[...], preferred_element_type=F32)

    @pl.when(e == pl.num_programs(1) - 1)
    def _():
        y_ref[...] = _rms(y_ref[...], g_ref[...])


def peer_dense(xT, u, v, s1, s2, thr, h, g, tm):
    D, T = xT.shape
    E = u.shape[0]
    te = SUBLANES * PEER_NKEYS
    assert T % tm == 0 and E % te == 0 and tm % LANES == 0
    return pl.pallas_call(
        functools.partial(_peer_dense_body, te=te, tm=tm),
        grid=(T // tm, E // te),
        in_specs=[
            pl.BlockSpec((D, tm), lambda i, e: (0, i)),
            pl.BlockSpec((te, D), lambda i, e: (e, 0)),
            pl.BlockSpec((te, D), lambda i, e: (e, 0)),
            pl.BlockSpec((PEER_HEADS, SUBLANES, tm), lambda i, e: (0, e, i)),
            pl.BlockSpec((PEER_HEADS, PEER_NKEYS, tm), lambda i, e: (0, 0, i)),
            pl.BlockSpec((PEER_HEADS, tm), lambda i, e: (0, i)),
            pl.BlockSpec((tm, D), lambda i, e: (i, 0)),
            pl.BlockSpec((1, D), lambda i, e: (0, 0)),
        ],
        out_specs=pl.BlockSpec((tm, D), lambda i, e: (i, 0)),
        out_shape=jax.ShapeDtypeStruct((T, D), F32),
        compiler_params=_cparams("parallel", "arbitrary"),
        name="peer_dense",
    )(xT, u, v, s1, s2, thr, h, g)


def _post_mix(x, o_attn, o_gla, w, tm, tm_proj):
    h = proj_residual(o_attn, o_gla, w["w_o_a"], w["w_o_b"], x, tm_proj)
    qp, xT = peer_query(h, w["norm_ffn"], w["peer_query"], tm)
    s1, s2, thr = peer_route(qp, w["peer_keys_1"], w["peer_keys_2"], tm)
    return peer_dense(xT, w["expert_u"], w["expert_v"], s1, s2, thr, h, w["final_norm"], tm)


def _prompt_group(x_prompt, w):
    B, S, _ = x_prompt.shape
    x = x_prompt.reshape(B * S, D_MODEL)
    tm, tm_proj = 512, 1024
    z, log_a, k_win, v_win = in_proj(x, w, jnp.arange(S), tm_proj, min(WINDOW_MAX, S))
    o_attn = attn_prompt(z, B, S)
    o_gla, state = gla_prompt(z, log_a, w["gla_norm"], B, S)
    out = _post_mix(x, o_attn, o_gla, w, tm, tm_proj)
    return out, k_win, v_win, state


def _sample_group(x_sample, cache_k, cache_v, state, w):
    Bs, Ls, _ = x_sample.shape
    assert Ls == 1
    x = x_sample.reshape(Bs, D_MODEL)
    tm = Bs
    z, log_a, k_new, v_new = in_proj(x, w, jnp.full((tm,), PAST_LEN, jnp.int32), tm, tm)
    o_attn = attn_sample(z, cache_k, cache_v)
    o_gla, new_state = gla_sample(z, log_a, w["gla_norm"], state)
    out = _post_mix(x, o_attn, o_gla, w, tm, tm)
    return out, k_new, v_new, new_state


def _prepare(norm_attn, w_in, w_gate_up, b_gate, gla_norm, w_o, norm_ffn, peer_query_w, peer_keys_1,
             peer_keys_2, expert_u, expert_v, final_norm_w):
    assert w_in.shape[0] == 1
    l = 0
    w_in_l = w_in[l]
    row = lambda v: v.reshape(1, -1).astype(F32)
    return {
        "norm_attn": row(norm_attn[l]),
        "w_in_main": w_in_l[:, :MAIN_COLS].astype(BF16),
        "w_in_lr": jnp.pad(w_in_l[:, MAIN_COLS:], ((0, 0), (0, LANES - GLA_LOWRANK))).astype(BF16),
        "w_gate_up": jnp.pad(w_gate_up[l], ((0, LANES - GLA_LOWRANK), (0, 0))),
        "b_gate": row(b_gate[l]),
        "gla_norm": row(gla_norm[l]),
        "w_o_a": w_o[l][:ATTN_WIDTH].astype(BF16),
        "w_o_b": w_o[l][ATTN_WIDTH:].astype(BF16),
        "norm_ffn": row(norm_ffn[l]),
        "peer_query": peer_query_w[l].astype(BF16),
        "peer_keys_1": peer_keys_1[l],
        "peer_keys_2": peer_keys_2[l],
        "expert_u": expert_u[l].astype(BF16),
        "expert_v": expert_v[l].astype(BF16),
        "final_norm": row(final_norm_w),
    }


def kernel(x_prompt, x_sample, cache_attn_k, cache_attn_v, state_gla, norm_attn, w_in, w_gate_up,
           b_gate, gla_norm, w_o, norm_ffn, peer_query, peer_keys_1, peer_keys_2, expert_u,
           expert_v, final_norm):
    Bp, Lp, _ = x_prompt.shape
    Bs, Ls, _ = x_sample.shape
    w = _prepare(norm_attn, w_in, w_gate_up, b_gate, gla_norm, w_o, norm_ffn, peer_query, peer_keys_1,
                 peer_keys_2, expert_u, expert_v, final_norm)
    win_p = min(WINDOW_MAX, Lp)
    yp, kp, vp, sp = _prompt_group(x_prompt, w)
    ys, ks, vs, ss = _sample_group(x_sample, cache_attn_k[0], cache_attn_v[0], state_gla[0], w)
    heads = lambda a, B, L: a.reshape(1, B, L, ATTN_HEADS, HEAD_DIM)
    return (yp.reshape(Bp, Lp, D_MODEL), ys.reshape(Bs, Ls, D_MODEL),
            heads(kp, Bp, win_p), heads(vp, Bp, win_p), sp[None],
            heads(ks, Bs, Ls), heads(vs, Bs, Ls), ss[None])


def sample_group(inp):
    names = ("norm_attn", "w_in", "w_gate_up", "b_gate", "gla_norm", "w_o", "norm_ffn", "peer_query",
             "peer_keys_1", "peer_keys_2", "expert_u", "expert_v", "final_norm")
    w = _prepare(*[inp[n] for n in names])
    out, k, v, st = _sample_group(inp["x_sample"], inp["cache_attn_k"][0], inp["cache_attn_v"][0],
                                  inp["state_gla"][0], w)
    Bs = out.shape[0]
    return (out, k.reshape(Bs, 1, ATTN_HEADS, HEAD_DIM), v.reshape(Bs, 1, ATTN_HEADS, HEAD_DIM), st)
```

```python
import functools
import math

import jax
import jax.numpy as jnp
import numpy as np
from jax import lax
from jax.experimental import pallas as pl
from jax.experimental.pallas import tpu as pltpu

F32 = jnp.float32
BF16 = jnp.bfloat16

D_MODEL = 2048
PAST_LEN = 2048
HEAD_DIM = 128
ATTN_HEADS = 8
ATTN_WIDTH = ATTN_HEADS * HEAD_DIM
DILATIONS = ((128, 1), (512, 4), (2048, 16))
WINDOW_MAX = 2048
QUERY_BLOCK = 128
ATTN_SCALE = HEAD_DIM ** -0.5
ROPE_THETA = 500000.0
ROT_DIM = HEAD_DIM // 4
ROT_HALF = ROT_DIM // 2
GLA_HEADS = 4
GLA_WIDTH = D_MODEL - ATTN_WIDTH
GLA_DV = GLA_WIDTH // GLA_HEADS
GLA_DK = GLA_DV // 2
GLA_KW = GLA_HEADS * GLA_DK
GLA_LOWRANK = 16
GLA_TAU = 16.0
GLA_SUB = 16
PEER_HEADS = 8
PEER_NKEYS = 128
PEER_EXPERTS = PEER_NKEYS * PEER_NKEYS
PEER_QDIM = 256
PEER_HALF = PEER_QDIM // 2
PEER_TOPK = 16
NORM_EPS = 1e-6
MAIN_COLS = 3 * ATTN_WIDTH + 2 * GLA_KW + 2 * GLA_WIDTH

LANES = 128
SUBLANES = 8
VMEM_LIMIT = 56 * 1024 * 1024
NEG = -1e30
LOG2E = 1.0 / math.log(2.0)
HIGHEST = lax.Precision.HIGHEST
NT = (((1,), (1,)), ((), ()))
TN = (((0,), (0,)), ((), ()))


def _cparams(*sem):
    return pltpu.CompilerParams(dimension_semantics=sem, vmem_limit_bytes=VMEM_LIMIT)


def _rms(x, g):
    return x * lax.rsqrt(jnp.mean(x * x, axis=-1, keepdims=True) + NORM_EPS) * g


def _in_proj_body(x_ref, g_ref, w_ref, wlr_ref, wup_ref, bg_ref, cos_ref, sin_ref,
                  z_ref, la_ref, kwin_ref, vwin_ref, xn_ref, *, tn, nb, wb):
    i = pl.program_id(0)
    j = pl.program_id(1)
    in_win = i % nb >= nb - wb
    kv_tiles = ATTN_WIDTH // tn

    @pl.when(j == 0)
    def _():
        xn = _rms(x_ref[...], g_ref[...]).astype(BF16)
        xn_ref[...] = xn
        lr = jnp.dot(xn, wlr_ref[...], preferred_element_type=F32)
        zg = jnp.dot(lr, wup_ref[...], precision=HIGHEST, preferred_element_type=F32) + bg_ref[...]
        la_ref[...] = (jnp.minimum(zg, 0.0) - jnp.log1p(jnp.exp(-jnp.abs(zg)))) * (1.0 / GLA_TAU)

    acc = jnp.dot(xn_ref[...], w_ref[...], preferred_element_type=F32)

    @pl.when(j < 2 * ATTN_WIDTH // tn)
    def _():
        lane = lax.broadcasted_iota(jnp.int32, (acc.shape[0], HEAD_DIM), 1)
        cos = cos_ref[...]
        sin = sin_ref[...]
        for c in range(tn // HEAD_DIM):
            a = acc[:, c * HEAD_DIM:(c + 1) * HEAD_DIM]
            partner = jnp.where(lane < ROT_HALF, pltpu.roll(a, HEAD_DIM - ROT_HALF, 1),
                                pltpu.roll(a, ROT_HALF, 1))
            z_ref[:, c * HEAD_DIM:(c + 1) * HEAD_DIM] = a * cos + partner * sin

        @pl.when(jnp.logical_and(in_win, j >= kv_tiles))
        def _():
            kwin_ref[...] = z_ref[...]

    @pl.when(j >= 2 * ATTN_WIDTH // tn)
    def _():
        z_ref[...] = acc

        @pl.when(jnp.logical_and(in_win, j < 3 * kv_tiles))
        def _():
            vwin_ref[...] = acc


def _rotary_tables(pos):
    inv = jnp.exp(-math.log(ROPE_THETA) * jnp.arange(ROT_HALF, dtype=F32) * (2.0 / ROT_DIM))
    ang = pos.astype(F32)[:, None] * inv[None, :]
    n = pos.shape[0]
    cos = jnp.concatenate([jnp.cos(ang), jnp.cos(ang), jnp.ones((n, HEAD_DIM - ROT_DIM), F32)], axis=1)
    sin = jnp.concatenate([-jnp.sin(ang), jnp.sin(ang), jnp.zeros((n, HEAD_DIM - ROT_DIM), F32)], axis=1)
    return cos, sin


def in_proj(x, w, pos, tm, win, tn=512):
    T, D = x.shape
    P = pos.shape[0]
    assert T % tm == 0 and MAIN_COLS % tn == 0 and P % tm == 0 and ATTN_WIDTH % tn == 0 and win % tm == 0
    nb, wb = P // tm, win // tm
    kv_tiles = ATTN_WIDTH // tn
    cos, sin = _rotary_tables(pos)
    const = lambda i, j: (0, 0)

    def win_spec(first_tile):
        def index(i, j):
            ib = i % nb
            row = (i // nb) * wb + jnp.maximum(ib - (nb - wb), 0)
            col = jnp.where(ib >= nb - wb, jnp.clip(j - first_tile, 0, kv_tiles - 1), 0)
            return row, col
        return pl.BlockSpec((tm, tn), index)

    win_shape = jax.ShapeDtypeStruct((T // P * win, ATTN_WIDTH), F32)
    return pl.pallas_call(
        functools.partial(_in_proj_body, tn=tn, nb=nb, wb=wb),
        grid=(T // tm, MAIN_COLS // tn),
        in_specs=[
            pl.BlockSpec((tm, D), lambda i, j: (i, 0)),
            pl.BlockSpec((1, D), const),
            pl.BlockSpec((D, tn), lambda i, j: (0, j)),
            pl.BlockSpec((D, LANES), const),
            pl.BlockSpec((LANES, GLA_KW), const),
            pl.BlockSpec((1, GLA_KW), const),
            pl.BlockSpec((tm, HEAD_DIM), lambda i, j: (i % (P // tm), 0)),
            pl.BlockSpec((tm, HEAD_DIM), lambda i, j: (i % (P // tm), 0)),
        ],
        out_specs=[
            pl.BlockSpec((tm, tn), lambda i, j: (i, j)),
            pl.BlockSpec((tm, GLA_KW), lambda i, j: (i, 0)),
            win_spec(kv_tiles),
            win_spec(2 * kv_tiles),
        ],
        out_shape=[
            jax.ShapeDtypeStruct((T, MAIN_COLS), F32),
            jax.ShapeDtypeStruct((T, GLA_KW), F32),
            win_shape,
            win_shape,
        ],
        scratch_shapes=[pltpu.VMEM((tm, D), BF16)],
        compiler_params=_cparams("arbitrary", "arbitrary"),
        name="in_proj",
    )(x, w["norm_attn"], w["w_in"], w["w_in_lr"], w["w_gate_up"], w["b_gate"], cos, sin)


def _attn_prompt_body(q_ref, k_ref, v_ref, o_ref, ob_ref, lse_ref, s_ref, p_ref, m_ref, *, S):
    QB = QUERY_BLOCK
    row = lax.broadcasted_iota(jnp.int32, (QB, QB), 0)
    col = lax.broadcasted_iota(jnp.int32, (QB, QB), 1)
    bias = jnp.concatenate([jnp.where(col <= row, 0.0, NEG), jnp.where(col >= row, 0.0, NEG)], axis=1)
    lane2 = lax.broadcasted_iota(jnp.int32, (QB, 2 * QB), 1)
    ones = jnp.ones((QB, HEAD_DIM), BF16)

    for bi, (window, d) in enumerate(DILATIONS):
        assert window == d * QB
        nblk = S // d // QB
        assert d * nblk == s_ref.shape[0]

        def windows(idx, d=d, nblk=nblk):
            r = idx // nblk
            i = idx % nblk
            start = r + i * (d * QB)
            if d == 1:
                return i, pl.ds(pl.multiple_of(start, QB), QB)
            return i, pl.ds(start, QB, stride=d)

        def scores(idx, k_prev):
            i, rows = windows(idx)
            qs = (q_ref[rows, :] * ATTN_SCALE).astype(BF16)
            k_cur = k_ref[rows, :].astype(BF16)
            kk = jnp.concatenate([k_cur, k_prev], axis=0)
            s = lax.dot_general(qs, kk, NT, preferred_element_type=F32) + bias
            s_ref[idx] = jnp.where(jnp.logical_or(lane2 < QB, i > 0), s, NEG)
            return k_cur

        def softmax(idx, carry):
            s = s_ref[idx]
            m = jnp.max(jnp.maximum(s[:, :QB], s[:, QB:]), axis=1, keepdims=True)
            p_ref[idx] = jnp.exp(s - m).astype(BF16)
            m_ref[idx] = jnp.broadcast_to(m, (QB, HEAD_DIM))
            return carry

        def values(idx, v_prev, bi=bi):
            i, rows = windows(idx)
            v_cur = jnp.concatenate([v_ref[rows, :].astype(BF16), ones], axis=1)
            vv = jnp.concatenate([v_cur, v_prev], axis=0)
            o = jnp.dot(p_ref[idx], vv, preferred_element_type=F32)
            den = o[:, HEAD_DIM:]
            ob_ref[bi, rows, :] = o[:, :HEAD_DIM] / den
            lse_ref[bi, rows, :] = m_ref[idx] + jnp.log(den)
            return v_cur

        lax.fori_loop(0, d * nblk, scores, jnp.zeros((QB, HEAD_DIM), BF16), unroll=4)
        lax.fori_loop(0, d * nblk, softmax, 0, unroll=4)
        lax.fori_loop(0, d * nblk, values, jnp.zeros((QB, 2 * HEAD_DIM), BF16), unroll=4)

    CH = 256

    def combine(c, carry):
        rows = pl.ds(pl.multiple_of(c * CH, CH), CH)
        l0, l1, l2 = lse_ref[0, rows, :], lse_ref[1, rows, :], lse_ref[2, rows, :]
        mx = jnp.maximum(jnp.maximum(l0, l1), l2)
        w0, w1, w2 = jnp.exp(l0 - mx), jnp.exp(l1 - mx), jnp.exp(l2 - mx)
        num = w0 * ob_ref[0, rows, :] + w1 * ob_ref[1, rows, :] + w2 * ob_ref[2, rows, :]
        o_ref[rows, :] = (num / (w0 + w1 + w2)).astype(o_ref.dtype)
        return carry

    lax.fori_loop(0, S // CH, combine, 0)


def attn_prompt(z, B, S):
    H = ATTN_HEADS
    assert S % (DILATIONS[-1][1] * QUERY_BLOCK) == 0
    return pl.pallas_call(
        functools.partial(_attn_prompt_body, S=S),
        grid=(B, H),
        in_specs=[
            pl.BlockSpec((S, HEAD_DIM), lambda b, h: (b, h)),
            pl.BlockSpec((S, HEAD_DIM), lambda b, h: (b, H + h)),
            pl.BlockSpec((S, HEAD_DIM), lambda b, h: (b, 2 * H + h)),
        ],
        out_specs=pl.BlockSpec((S, HEAD_DIM), lambda b, h: (b, h)),
        out_shape=jax.ShapeDtypeStruct((B * S, ATTN_WIDTH), BF16),
        scratch_shapes=[
            pltpu.VMEM((len(DILATIONS), S, HEAD_DIM), F32),
            pltpu.VMEM((len(DILATIONS), S, HEAD_DIM), F32),
            pltpu.VMEM((S // QUERY_BLOCK, QUERY_BLOCK, 2 * QUERY_BLOCK), F32),
            pltpu.VMEM((S // QUERY_BLOCK, QUERY_BLOCK, 2 * QUERY_BLOCK), BF16),
            pltpu.VMEM((S // QUERY_BLOCK, QUERY_BLOCK, HEAD_DIM), F32),
        ],
        compiler_params=_cparams("parallel", "parallel"),
        name="attn_prompt",
    )(z, z, z)


def _attn_sample_body(z_ref, k1_ref, k4_ref, k16_ref, v1_ref, v4_ref, v16_ref, o_ref, *, bb):
    H = ATTN_HEADS
    for b in range(bb):
        q = z_ref[b, 0:H, :] * ATTN_SCALE
        k_new = z_ref[b, H:2 * H, :]
        v_new = z_ref[b, 2 * H:3 * H, :]
        s_new = jnp.sum(k_new * q, axis=-1, keepdims=True)
        s_win = [jnp.sum(kr[b] * q, axis=-1, keepdims=True)
                 for kr in (k1_ref, k4_ref, k16_ref)]
        m = s_new
        for s in s_win:
            m = jnp.maximum(m, jnp.max(s, axis=0))
        p_new = jnp.exp(s_new - m) * float(len(DILATIONS))
        den = p_new
        acc = p_new * v_new
        for s, vr in zip(s_win, (v1_ref, v4_ref, v16_ref)):
            p = jnp.exp(s - m)
            den = den + jnp.sum(p, axis=0)
            acc = acc + jnp.sum(p * vr[b], axis=0)
        o_ref[b] = acc / den


def attn_sample(z, cache_k, cache_v, bb=4):
    Bs = z.shape[0]
    wbuf = cache_k.shape[1]
    QB, H = QUERY_BLOCK, ATTN_HEADS
    assert wbuf == PAST_LEN == WINDOW_MAX and Bs % bb == 0

    def views(c):
        out, specs = [], []
        for window, d in DILATIONS:
            n = wbuf // d
            if d == 1:
                out.append(c)
                specs.append(pl.BlockSpec((bb, QB, H, HEAD_DIM), lambda i, n=n: (i, n // QB - 1, 0, 0)))
            else:
                out.append(c.reshape(Bs, n, d, H, HEAD_DIM))
                specs.append(pl.BlockSpec((bb, QB, None, H, HEAD_DIM),
                                          lambda i, n=n: (i, n // QB - 1, 0, 0, 0)))
        return out, specs

    kv, kspecs = views(cache_k)
    vv, vspecs = views(cache_v)
    z3 = z.reshape(Bs, MAIN_COLS // HEAD_DIM, HEAD_DIM)
    out = pl.pallas_call(
        functools.partial(_attn_sample_body, bb=bb),
        grid=(Bs // bb,),
        in_specs=[pl.BlockSpec((bb, MAIN_COLS // HEAD_DIM, HEAD_DIM), lambda i: (i, 0, 0))] + kspecs + vspecs,
        out_specs=pl.BlockSpec((bb, H, HEAD_DIM), lambda i: (i, 0, 0)),
        out_shape=jax.ShapeDtypeStruct((Bs, H, HEAD_DIM), F32),
        compiler_params=_cparams("parallel"),
        name="attn_sample",
    )(z3, *kv, *vv)
    return out.reshape(Bs, ATTN_WIDTH).astype(BF16)


def _gla_prompt_body(q_ref, k_ref, v_ref, go_ref, la_ref, gn_ref, o_ref, st_ref, sT_ref, b_ref, oi_ref,
                     *, C):
    c = pl.program_id(1)
    nsub = C // GLA_SUB

    @pl.when(c == 0)
    def _():
        sT_ref[...] = jnp.zeros_like(sT_ref)

    tri = (lax.broadcasted_iota(jnp.int32, (C, C), 1)
           <= lax.broadcasted_iota(jnp.int32, (C, C), 0)).astype(F32)
    srow = lax.broadcasted_iota(jnp.int32, (GLA_SUB, 1), 0)

    for h in range(GLA_HEADS):
        kl = pl.ds(h * GLA_DK, GLA_DK)
        vl = pl.ds(h * GLA_DV, GLA_DV)
        b = jnp.dot(tri, la_ref[:, kl], precision=HIGHEST, preferred_element_type=F32)
        b_ref[h] = b
        qh = q_ref[:, kl] * (GLA_DK ** -0.5)
        kh = k_ref[:, kl]
        vh = v_ref[:, vl]
        b_end = b[C - 1:C, :]
        sT = sT_ref[h]
        oi_ref[h] = lax.dot_general((qh * jnp.exp(b)).astype(BF16), sT.astype(BF16), NT,
                                    preferred_element_type=F32)
        k_end = (kh * jnp.exp(b_end - b)).astype(BF16)
        sT_ref[h] = sT * jnp.exp(b_end) + lax.dot_general(vh.astype(BF16), k_end, TN,
                                                          preferred_element_type=F32)
        for I in range(1, nsub):
            r0 = I * GLA_SUB
            cI = b[r0 - 1:r0, :]
            qI = (qh[r0:r0 + GLA_SUB] * jnp.exp(b[r0:r0 + GLA_SUB] - cI)).astype(BF16)
            kI = (kh[:r0] * jnp.exp(cI - b[:r0])).astype(BF16)
            a = lax.dot_general(qI, kI, NT, preferred_element_type=F32)
            oi_ref[h, r0:r0 + GLA_SUB, :] += jnp.dot(a.astype(BF16), vh[:r0].astype(BF16),
                                                     preferred_element_type=F32)

    def diag(I, carry):
        rows = pl.ds(pl.multiple_of(I * GLA_SUB, GLA_SUB), GLA_SUB)
        for h in range(GLA_HEADS):
            bI = b_ref[h, rows, :]
            qI = q_ref[rows, pl.ds(h * GLA_DK, GLA_DK)] * (GLA_DK ** -0.5)
            kI = k_ref[rows, pl.ds(h * GLA_DK, GLA_DK)]
            vI = v_ref[rows, pl.ds(h * GLA_DV, GLA_DV)]
            acc = jnp.zeros((GLA_SUB, GLA_DV), F32)
            for s in range(GLA_SUB):
                e = jnp.exp(jnp.minimum(bI - bI[s:s + 1, :], 0.0))
                a_col = jnp.sum(qI * kI[s:s + 1, :] * e, axis=1, keepdims=True)
                a_col = jnp.where(srow >= s, a_col, 0.0)
                acc += a_col * vI[s:s + 1, :]
            oi_ref[h, rows, :] += acc
        return carry

    lax.fori_loop(0, nsub, diag, 0)

    for h in range(GLA_HEADS):
        vl = pl.ds(h * GLA_DV, GLA_DV)
        g = go_ref[:, vl]
        o_ref[:, vl] = (_rms(oi_ref[h], gn_ref[...]) * (g * jax.nn.sigmoid(g))).astype(o_ref.dtype)

    @pl.when(c == pl.num_programs(1) - 1)
    def _():
        for h in range(GLA_HEADS):
            st_ref[0, h] = sT_ref[h].T


def gla_prompt(z, log_a, gla_norm, B, S, C=128):
    assert S % C == 0 and C % GLA_SUB == 0
    nc = S // C
    q_blk = 3 * ATTN_WIDTH // GLA_KW
    v_blk = (3 * ATTN_WIDTH + 2 * GLA_KW) // GLA_WIDTH
    return pl.pallas_call(
        functools.partial(_gla_prompt_body, C=C),
        grid=(B, nc),
        in_specs=[
            pl.BlockSpec((C, GLA_KW), lambda b, c: (b * nc + c, q_blk)),
            pl.BlockSpec((C, GLA_KW), lambda b, c: (b * nc + c, q_blk + 1)),
            pl.BlockSpec((C, GLA_WIDTH), lambda b, c: (b * nc + c, v_blk)),
            pl.BlockSpec((C, GLA_WIDTH), lambda b, c: (b * nc + c, v_blk + 1)),
            pl.BlockSpec((C, GLA_KW), lambda b, c: (b * nc + c, 0)),
            pl.BlockSpec((1, GLA_DV), lambda b, c: (0, 0)),
        ],
        out_specs=[
            pl.BlockSpec((C, GLA_WIDTH), lambda b, c: (b * nc + c, 0)),
            pl.BlockSpec((1, GLA_HEADS, GLA_DK, GLA_DV), lambda b, c: (b, 0, 0, 0)),
        ],
        out_shape=[
            jax.ShapeDtypeStruct((B * S, GLA_WIDTH), BF16),
            jax.ShapeDtypeStruct((B, GLA_HEADS, GLA_DK, GLA_DV), F32),
        ],
        scratch_shapes=[
            pltpu.VMEM((GLA_HEADS, GLA_DV, GLA_DK), F32),
            pltpu.VMEM((GLA_HEADS, C, GLA_DK), F32),
            pltpu.VMEM((GLA_HEADS, C, GLA_DV), F32),
        ],
        compiler_params=_cparams("parallel", "arbitrary"),
        name="gla_prompt",
    )(z, z, z, z, log_a, gla_norm)


def _gla_sample_body(q_ref, k_ref, v_ref, go_ref, la_ref, gn_ref, s_ref, o_ref, so_ref, *, bb):
    for h in range(GLA_HEADS):
        kl = pl.ds(h * GLA_DK, GLA_DK)
        vl = pl.ds(h * GLA_DV, GLA_DV)
        aT = jnp.exp(la_ref[:, kl]).T
        kT = k_ref[:, kl].T
        qT = (q_ref[:, kl] * (GLA_DK ** -0.5)).T
        outs = []
        for b in range(bb):
            s_new = aT[:, b:b + 1] * s_ref[b, h] + kT[:, b:b + 1] * v_ref[b:b + 1, vl]
            so_ref[b, h] = s_new
            outs.append(jnp.sum(qT[:, b:b + 1] * s_new, axis=0, keepdims=True))
        o = jnp.concatenate(outs, axis=0)
        g = go_ref[:, vl]
        o_ref[:, vl] = (_rms(o, gn_ref[...]) * (g * jax.nn.sigmoid(g))).astype(o_ref.dtype)


def gla_sample(z, log_a, gla_norm, state, bb=SUBLANES):
    Bs = z.shape[0]
    assert Bs % bb == 0
    q_blk = 3 * ATTN_WIDTH // GLA_KW
    v_blk = (3 * ATTN_WIDTH + 2 * GLA_KW) // GLA_WIDTH
    st_spec = pl.BlockSpec((bb, GLA_HEADS, GLA_DK, GLA_DV), lambda i: (i, 0, 0, 0))
    return pl.pallas_call(
        functools.partial(_gla_sample_body, bb=bb),
        grid=(Bs // bb,),
        in_specs=[
            pl.BlockSpec((bb, GLA_KW), lambda i: (i, q_blk)),
            pl.BlockSpec((bb, GLA_KW), lambda i: (i, q_blk + 1)),
            pl.BlockSpec((bb, GLA_WIDTH), lambda i: (i, v_blk)),
            pl.BlockSpec((bb, GLA_WIDTH), lambda i: (i, v_blk + 1)),
            pl.BlockSpec((bb, GLA_KW), lambda i: (i, 0)),
            pl.BlockSpec((1, GLA_DV), lambda i: (0, 0)),
            st_spec,
        ],
        out_specs=[pl.BlockSpec((bb, GLA_WIDTH), lambda i: (i, 0)), st_spec],
        out_shape=[
            jax.ShapeDtypeStruct((Bs, GLA_WIDTH), BF16),
            jax.ShapeDtypeStruct(state.shape, F32),
        ],
        compiler_params=_cparams("parallel"),
        name="gla_sample",
    )(z, z, z, z, log_a, gla_norm, state)


def _proj_residual_body(a_ref, b_ref, wa_ref, wb_ref, res_ref, o_ref):
    acc = jnp.dot(a_ref[...], wa_ref[...], preferred_element_type=F32)
    acc += jnp.dot(b_ref[...], wb_ref[...], preferred_element_type=F32)
    o_ref[...] = res_ref[...] + acc


def proj_residual(a, b, wa, wb, res, tm, tn=512):
    T, Ka = a.shape
    Kb = b.shape[1]
    N = wa.shape[1]
    assert T % tm == 0 and N % tn == 0
    return pl.pallas_call(
        _proj_residual_body,
        grid=(T // tm, N // tn),
        in_specs=[
            pl.BlockSpec((tm, Ka), lambda i, j: (i, 0)),
            pl.BlockSpec((tm, Kb), lambda i, j: (i, 0)),
            pl.BlockSpec((Ka, tn), lambda i, j: (0, j)),
            pl.BlockSpec((Kb, tn), lambda i, j: (0, j)),
            pl.BlockSpec((tm, tn), lambda i, j: (i, j)),
        ],
        out_specs=pl.BlockSpec((tm, tn), lambda i, j: (i, j)),
        out_shape=jax.ShapeDtypeStruct((T, N), F32),
        compiler_params=_cparams("parallel", "arbitrary"),
        name="proj_residual",
    )(a, b, wa, wb, res)


def _sorting_network(n):
    assert n & (n - 1) == 0
    pairs = []
    p = 1
    while p < n:
        k = p
        while k >= 1:
            for j in range(k % p, n - k, 2 * k):
                for i in range(min(k, n - j - k)):
                    if (i + j) // (2 * p) == (i + j + k) // (2 * p):
                        pairs.append((i + j, i + j + k))
            k //= 2
        p *= 2
    return pairs


def _peer_route_body(h_ref, g_ref, w_ref, k1_ref, k2_ref, xT_ref, s1_ref, s2_ref, thr_ref, q_ref, *, tm):
    xn = _rms(h_ref[...], g_ref[...])
    xT_ref[...] = xn.T.astype(BF16)
    q_ref[...] = jnp.dot(xn.astype(BF16), w_ref[...], preferred_element_type=F32)

    K = PEER_TOPK
    KR = -(-(K + 1) // SUBLANES) * SUBLANES
    rowk = lax.broadcasted_iota(jnp.int32, (KR, tm), 0)
    row8 = lax.broadcasted_iota(jnp.int32, (SUBLANES, tm), 0)
    rowh = lax.broadcasted_iota(jnp.int32, (PEER_HEADS, tm), 0)
    thr_all = jnp.zeros((PEER_HEADS, tm), F32)

    def top_values(s):
        n = PEER_NKEYS // SUBLANES
        lists = [s[r * SUBLANES:(r + 1) * SUBLANES] for r in range(n)]
        for a, b in _sorting_network(n):
            lists[a], lists[b] = jnp.maximum(lists[a], lists[b]), jnp.minimum(lists[a], lists[b])
        vals = jnp.full((KR, tm), -jnp.inf, F32)
        for it in range(K + 1):
            mx = jnp.max(lists[0], axis=0, keepdims=True)
            vals = jnp.where(rowk == it, mx, vals)
            hit = lists[0] == mx
            for r in range(min(n, K - it)):
                below = lists[r + 1] if r + 1 < n else -jnp.inf
                lists[r] = jnp.where(hit, below, lists[r])
        return vals

    for h in range(PEER_HEADS):
        q1 = q_ref[:, h * PEER_QDIM:h * PEER_QDIM + PEER_HALF]
        q2 = q_ref[:, h * PEER_QDIM + PEER_HALF:(h + 1) * PEER_QDIM]
        s1 = lax.dot_general(k1_ref[h], q1, NT, precision=HIGHEST, preferred_element_type=F32)
        s2 = lax.dot_general(k2_ref[h], q2, NT, precision=HIGHEST, preferred_element_type=F32)
        v1 = top_values(s1)
        v2 = top_values(s2)
        pieces = [v1[0:1] + v2[r:r + SUBLANES] for r in range(0, KR, SUBLANES)]
        for a in range(1, SUBLANES):
            pieces.append(jnp.where(row8 < (K + 1) // (a + 1), v1[a:a + 1] + v2[0:SUBLANES], -jnp.inf))
        pieces += [v1[r:r + SUBLANES] + v2[0:1] for r in range(SUBLANES, KR, SUBLANES)]
        cand = jnp.concatenate(pieces, axis=0)
        m = v1[0:1] + v2[0:1]
        z = jnp.zeros((1, tm), F32)
        kth = m
        for it in range(K):
            kth = jnp.max(cand, axis=0, keepdims=True)
            z = z + jnp.exp(kth - m)
            cand = jnp.where(cand == kth, -jnp.inf, cand)
        nxt = jnp.max(cand, axis=0, keepdims=True)
        c = m + jnp.log(z)
        s1_ref[h] = (s1 - c) * LOG2E - 1.0
        s2_ref[h] = s2 * LOG2E
        thr_all = jnp.where(rowh == h, (0.5 * (kth + nxt) - c) * LOG2E - 1.0, thr_all)
    thr_ref[...] = thr_all


def peer_route(h, g, w, keys1, keys2, tm):
    T, D = h.shape
    N = w.shape[1]
    assert T % tm == 0 and tm % LANES == 0 and (PEER_TOPK + 1) // 2 <= SUBLANES and N == PEER_HEADS * PEER_QDIM
    tile = pl.BlockSpec((PEER_HEADS, PEER_NKEYS, tm), lambda i: (0, 0, i))
    tile_shape = jax.ShapeDtypeStruct((PEER_HEADS, PEER_NKEYS, T), F32)
    kspec = pl.BlockSpec((PEER_HEADS, PEER_NKEYS, PEER_HALF), lambda i: (0, 0, 0))
    return pl.pallas_call(
        functools.partial(_peer_route_body, tm=tm),
        grid=(T // tm,),
        in_specs=[
            pl.BlockSpec((tm, D), lambda i: (i, 0)),
            pl.BlockSpec((1, D), lambda i: (0, 0)),
            pl.BlockSpec((D, N), lambda i: (0, 0)),
            kspec, kspec,
        ],
        out_specs=[pl.BlockSpec((D, tm), lambda i: (0, i)), tile, tile,
                   pl.BlockSpec((PEER_HEADS, tm), lambda i: (0, i))],
        out_shape=[jax.ShapeDtypeStruct((D, T), BF16), tile_shape, tile_shape,
                   jax.ShapeDtypeStruct((PEER_HEADS, T), F32)],
        scratch_shapes=[pltpu.VMEM((tm, N), F32)],
        compiler_params=_cparams("parallel"),
        name="peer_route",
    )(h, g, w, keys1, keys2)


def _peer_dense_body(xT_ref, u_ref, v_ref, s1_ref, s2_ref, thr_ref, h_ref, g_ref, y_ref, *, te, tm):
    e = pl.program_id(1)

    @pl.when(e == 0)
    def _():
        y_ref[...] = h_ref[...]

    rows = []
    for c in range(te // PEER_NKEYS):
        tiles = []
        for tc in range(tm // LANES):
            lanes = pl.ds(tc * LANES, LANES)
            gate = jnp.zeros((PEER_NKEYS, LANES), F32)
            for h in range(PEER_HEADS):
                ssum = s2_ref[h, :, lanes] + s1_ref[h, c:c + 1, lanes]
                gate += jnp.where(ssum >= thr_ref[pl.ds(h, 1), lanes], jnp.exp2(ssum), 0.0)
            tiles.append(gate)
        rows.append(jnp.concatenate(tiles, axis=1))
    half_gate = jnp.concatenate(rows, axis=0)
    hT = jnp.dot(u_ref[...], xT_ref[...], preferred_element_type=F32)
    act = (hT + hT * lax.erf(hT * (2.0 ** -0.5))) * half_gate
    y_ref[...] += jnp.dot(act.T.astype(BF16), v_ref[...], preferred_element_type=F32)

    @pl.when(e == pl.num_programs(1) - 1)
    def _():
        y_ref[...] = _rms(y_ref[...], g_ref[...])


def peer_dense(xT, u, v, s1, s2, thr, h, g, tm):
    D, T = xT.shape
    E = u.shape[0]
    te = SUBLANES * PEER_NKEYS
    assert T % tm == 0 and E % te == 0 and tm % LANES == 0
    return pl.pallas_call(
        functools.partial(_peer_dense_body, te=te, tm=tm),
        grid=(T // tm, E // te),
        in_specs=[
            pl.BlockSpec((D, tm), lambda i, e: (0, i)),
            pl.BlockSpec((te, D), lambda i, e: (e, 0)),
            pl.BlockSpec((te, D), lambda i, e: (e, 0)),
            pl.BlockSpec((PEER_HEADS, SUBLANES, tm), lambda i, e: (0, e, i)),
            pl.BlockSpec((PEER_HEADS, PEER_NKEYS, tm), lambda i, e: (0, 0, i)),
            pl.BlockSpec((PEER_HEADS, tm), lambda i, e: (0, i)),
            pl.BlockSpec((tm, D), lambda i, e: (i, 0)),
            pl.BlockSpec((1, D), lambda i, e: (0, 0)),
        ],
        out_specs=pl.BlockSpec((tm, D), lambda i, e: (i, 0)),
        out_shape=jax.ShapeDtypeStruct((T, D), F32),
        compiler_params=_cparams("parallel", "arbitrary"),
        name="peer_dense",
    )(xT, u, v, s1, s2, thr, h, g)


def _post_mix(x, o_attn, o_gla, w, tm, tm_proj):
    h = proj_residual(o_attn, o_gla, w["w_o_a"], w["w_o_b"], x, tm_proj)
    xT, s1, s2, thr = peer_route(h, w["norm_ffn"], w["peer_query"], w["peer_keys_1"], w["peer_keys_2"], tm)
    return peer_dense(xT, w["expert_u"], w["expert_v"], s1, s2, thr, h, w["final_norm"], tm)


def _prompt_group(x_prompt, w):
    B, S, _ = x_prompt.shape
    x = x_prompt.reshape(B * S, D_MODEL)
    tm, tm_proj = 512, 1024
    z, log_a, k_win, v_win = in_proj(x, w, jnp.arange(S), tm_proj, min(WINDOW_MAX, S))
    o_attn = attn_prompt(z, B, S)
    o_gla, state = gla_prompt(z, log_a, w["gla_norm"], B, S)
    out = _post_mix(x, o_attn, o_gla, w, tm, tm_proj)
    return out, k_win, v_win, state


def _sample_group(x_sample, cache_k, cache_v, state, w):
    Bs, Ls, _ = x_sample.shape
    assert Ls == 1
    x = x_sample.reshape(Bs, D_MODEL)
    tm = Bs
    z, log_a, k_new, v_new = in_proj(x, w, jnp.full((tm,), PAST_LEN, jnp.int32), tm, tm)
    o_attn = attn_sample(z, cache_k, cache_v)
    o_gla, new_state = gla_sample(z, log_a, w["gla_norm"], state)
    out = _post_mix(x, o_attn, o_gla, w, tm, tm)
    return out, k_new, v_new, new_state


def _prepare(norm_attn, w_in, w_gate_up, b_gate, gla_norm, w_o, norm_ffn, peer_query_w, peer_keys_1,
             peer_keys_2, expert_u, expert_v, final_norm_w):
    assert w_in.shape[0] == 1
    l = 0
    w_in_l = w_in[l]
    row = lambda v: v.reshape(1, -1).astype(F32)
    return {
        "norm_attn": row(norm_attn[l]),
        "w_in": w_in_l.astype(BF16),
        "w_in_lr": jnp.pad(w_in_l[:, MAIN_COLS:], ((0, 0), (0, LANES - GLA_LOWRANK))).astype(BF16),
        "w_gate_up": jnp.pad(w_gate_up[l], ((0, LANES - GLA_LOWRANK), (0, 0))),
        "b_gate": row(b_gate[l]),
        "gla_norm": row(gla_norm[l]),
        "w_o_a": w_o[l][:ATTN_WIDTH].astype(BF16),
        "w_o_b": w_o[l][ATTN_WIDTH:].astype(BF16),
        "norm_ffn": row(norm_ffn[l]),
        "peer_query": peer_query_w[l].astype(BF16),
        "peer_keys_1": peer_keys_1[l],
        "peer_keys_2": peer_keys_2[l],
        "expert_u": expert_u[l].astype(BF16),
        "expert_v": expert_v[l].astype(BF16),
        "final_norm": row(final_norm_w),
    }


def kernel(x_prompt, x_sample, cache_attn_k, cache_attn_v, state_gla, norm_attn, w_in, w_gate_up,
           b_gate, gla_norm, w_o, norm_ffn, peer_query, peer_keys_1, peer_keys_2, expert_u,
           expert_v, final_norm):
    Bp, Lp, _ = x_prompt.shape
    Bs, Ls, _ = x_sample.shape
    w = _prepare(norm_attn, w_in, w_gate_up, b_gate, gla_norm, w_o, norm_ffn, peer_query, peer_keys_1,
                 peer_keys_2, expert_u, expert_v, final_norm)
    win_p = min(WINDOW_MAX, Lp)
    yp, kp, vp, sp = _prompt_group(x_prompt, w)
    ys, ks, vs, ss = _sample_group(x_sample, cache_attn_k[0], cache_attn_v[0], state_gla[0], w)
    heads = lambda a, B, L: a.reshape(1, B, L, ATTN_HEADS, HEAD_DIM)
    return (yp.reshape(Bp, Lp, D_MODEL), ys.reshape(Bs, Ls, D_MODEL),
            heads(kp, Bp, win_p), heads(vp, Bp, win_p), sp[None],
            heads(ks, Bs, Ls), heads(vs, Bs, Ls), ss[None])


def sample_group(inp):
    names = ("norm_attn", "w_in", "w_gate_up", "b_gate", "gla_norm", "w_o", "norm_ffn", "peer_query",
             "peer_keys_1", "peer_keys_2", "expert_u", "expert_v", "final_norm")
    w = _prepare(*[inp[n] for n in names])
    out, k, v, st = _sample_group(inp["x_sample"], inp["cache_attn_k"][0], inp["cache_attn_v"][0],
                                  inp["state_gla"][0], w)
    Bs = out.shape[0]
    return (out, k.reshape(Bs, 1, ATTN_HEADS, HEAD_DIM), v.reshape(Bs, 1, ATTN_HEADS, HEAD_DIM), st)
```

```python
import functools
import math

import jax
import jax.numpy as jnp
import numpy as np
from jax import lax
from jax.experimental import pallas as pl
from jax.experimental.pallas import tpu as pltpu

F32 = jnp.float32
BF16 = jnp.bfloat16

D_MODEL = 2048
PAST_LEN = 2048
HEAD_DIM = 128
ATTN_HEADS = 8
ATTN_WIDTH = ATTN_HEADS * HEAD_DIM
DILATIONS = ((128, 1), (512, 4), (2048, 16))
WINDOW_MAX = 2048
QUERY_BLOCK = 128
ATTN_SCALE = HEAD_DIM ** -0.5
ROPE_THETA = 500000.0
ROT_DIM = HEAD_DIM // 4
ROT_HALF = ROT_DIM // 2
GLA_HEADS = 4
GLA_WIDTH = D_MODEL - ATTN_WIDTH
GLA_DV = GLA_WIDTH // GLA_HEADS
GLA_DK = GLA_DV // 2
GLA_KW = GLA_HEADS * GLA_DK
GLA_LOWRANK = 16
GLA_TAU = 16.0
GLA_SUB = 16
PEER_HEADS = 8
PEER_NKEYS = 128
PEER_EXPERTS = PEER_NKEYS * PEER_NKEYS
PEER_QDIM = 256
PEER_HALF = PEER_QDIM // 2
PEER_TOPK = 16
NORM_EPS = 1e-6
MAIN_COLS = 3 * ATTN_WIDTH + 2 * GLA_KW + 2 * GLA_WIDTH

LANES = 128
SUBLANES = 8
VMEM_LIMIT = 56 * 1024 * 1024
NEG = -1e30
LOG2E = 1.0 / math.log(2.0)
HIGHEST = lax.Precision.HIGHEST
NT = (((1,), (1,)), ((), ()))
TN = (((0,), (0,)), ((), ()))


def _cparams(*sem):
    return pltpu.CompilerParams(dimension_semantics=sem, vmem_limit_bytes=VMEM_LIMIT)


def _rms(x, g):
    return x * lax.rsqrt(jnp.mean(x * x, axis=-1, keepdims=True) + NORM_EPS) * g


def _in_proj_body(x_ref, g_ref, w_ref, wlr_ref, wup_ref, bg_ref, cos_ref, sin_ref,
                  z_ref, la_ref, kwin_ref, vwin_ref, xn_ref, *, tn, nb, wb):
    i = pl.program_id(0)
    j = pl.program_id(1)
    in_win = i % nb >= nb - wb
    kv_tiles = ATTN_WIDTH // tn

    @pl.when(j == 0)
    def _():
        xn = _rms(x_ref[...], g_ref[...]).astype(BF16)
        xn_ref[...] = xn
        lr = jnp.dot(xn, wlr_ref[...], preferred_element_type=F32)
        zg = jnp.dot(lr, wup_ref[...], precision=HIGHEST, preferred_element_type=F32) + bg_ref[...]
        la_ref[...] = (jnp.minimum(zg, 0.0) - jnp.log1p(jnp.exp(-jnp.abs(zg)))) * (1.0 / GLA_TAU)

    acc = jnp.dot(xn_ref[...], w_ref[...], preferred_element_type=F32)

    @pl.when(j < 2 * ATTN_WIDTH // tn)
    def _():
        lane = lax.broadcasted_iota(jnp.int32, (acc.shape[0], HEAD_DIM), 1)
        cos = cos_ref[...]
        sin = sin_ref[...]
        for c in range(tn // HEAD_DIM):
            a = acc[:, c * HEAD_DIM:(c + 1) * HEAD_DIM]
            partner = jnp.where(lane < ROT_HALF, pltpu.roll(a, HEAD_DIM - ROT_HALF, 1),
                                pltpu.roll(a, ROT_HALF, 1))
            z_ref[:, c * HEAD_DIM:(c + 1) * HEAD_DIM] = a * cos + partner * sin

        @pl.when(jnp.logical_and(in_win, j >= kv_tiles))
        def _():
            kwin_ref[...] = z_ref[...]

    @pl.when(j >= 2 * ATTN_WIDTH // tn)
    def _():
        z_ref[...] = acc

        @pl.when(jnp.logical_and(in_win, j < 3 * kv_tiles))
        def _():
            vwin_ref[...] = acc


def _rotary_tables(pos):
    inv = jnp.exp(-math.log(ROPE_THETA) * jnp.arange(ROT_HALF, dtype=F32) * (2.0 / ROT_DIM))
    ang = pos.astype(F32)[:, None] * inv[None, :]
    n = pos.shape[0]
    cos = jnp.concatenate([jnp.cos(ang), jnp.cos(ang), jnp.ones((n, HEAD_DIM - ROT_DIM), F32)], axis=1)
    sin = jnp.concatenate([-jnp.sin(ang), jnp.sin(ang), jnp.zeros((n, HEAD_DIM - ROT_DIM), F32)], axis=1)
    return cos, sin


def in_proj(x, w, pos, tm, win, tn=512):
    T, D = x.shape
    P = pos.shape[0]
    assert T % tm == 0 and MAIN_COLS % tn == 0 and P % tm == 0 and ATTN_WIDTH % tn == 0 and win % tm == 0
    nb, wb = P // tm, win // tm
    kv_tiles = ATTN_WIDTH // tn
    cos, sin = _rotary_tables(pos)
    const = lambda i, j: (0, 0)

    def win_spec(first_tile):
        def index(i, j):
            ib = i % nb
            row = (i // nb) * wb + jnp.maximum(ib - (nb - wb), 0)
            col = jnp.where(ib >= nb - wb, jnp.clip(j - first_tile, 0, kv_tiles - 1), 0)
            return row, col
        return pl.BlockSpec((tm, tn), index)

    win_shape = jax.ShapeDtypeStruct((T // P * win, ATTN_WIDTH), F32)
    return pl.pallas_call(
        functools.partial(_in_proj_body, tn=tn, nb=nb, wb=wb),
        grid=(T // tm, MAIN_COLS // tn),
        in_specs=[
            pl.BlockSpec((tm, D), lambda i, j: (i, 0)),
            pl.BlockSpec((1, D), const),
            pl.BlockSpec((D, tn), lambda i, j: (0, j)),
            pl.BlockSpec((D, LANES), const),
            pl.BlockSpec((LANES, GLA_KW), const),
            pl.BlockSpec((1, GLA_KW), const),
            pl.BlockSpec((tm, HEAD_DIM), lambda i, j: (i % (P // tm), 0)),
            pl.BlockSpec((tm, HEAD_DIM), lambda i, j: (i % (P // tm), 0)),
        ],
        out_specs=[
            pl.BlockSpec((tm, tn), lambda i, j: (i, j)),
            pl.BlockSpec((tm, GLA_KW), lambda i, j: (i, 0)),
            win_spec(kv_tiles),
            win_spec(2 * kv_tiles),
        ],
        out_shape=[
            jax.ShapeDtypeStruct((T, MAIN_COLS), F32),
            jax.ShapeDtypeStruct((T, GLA_KW), F32),
            win_shape,
            win_shape,
        ],
        scratch_shapes=[pltpu.VMEM((tm, D), BF16)],
        compiler_params=_cparams("arbitrary", "arbitrary"),
        name="in_proj",
    )(x, w["norm_attn"], w["w_in"], w["w_in_lr"], w["w_gate_up"], w["b_gate"], cos, sin)


def _attn_prompt_body(q_ref, k_ref, v_ref, o_ref, ob_ref, lse_ref, s_ref, p_ref, m_ref, *, S):
    QB = QUERY_BLOCK
    row = lax.broadcasted_iota(jnp.int32, (QB, QB), 0)
    col = lax.broadcasted_iota(jnp.int32, (QB, QB), 1)
    bias = jnp.concatenate([jnp.where(col <= row, 0.0, NEG), jnp.where(col >= row, 0.0, NEG)], axis=1)
    lane2 = lax.broadcasted_iota(jnp.int32, (QB, 2 * QB), 1)
    ones = jnp.ones((QB, HEAD_DIM), BF16)

    for bi, (window, d) in enumerate(DILATIONS):
        assert window == d * QB
        nblk = S // d // QB
        assert d * nblk == s_ref.shape[0]

        def windows(idx, d=d, nblk=nblk):
            r = idx // nblk
            i = idx % nblk
            start = r + i * (d * QB)
            if d == 1:
                return i, pl.ds(pl.multiple_of(start, QB), QB)
            return i, pl.ds(start, QB, stride=d)

        def scores(idx, k_prev):
            i, rows = windows(idx)
            qs = (q_ref[rows, :] * ATTN_SCALE).astype(BF16)
            k_cur = k_ref[rows, :].astype(BF16)
            kk = jnp.concatenate([k_cur, k_prev], axis=0)
            s = lax.dot_general(qs, kk, NT, preferred_element_type=F32) + bias
            s_ref[idx] = jnp.where(jnp.logical_or(lane2 < QB, i > 0), s, NEG)
            return k_cur

        def softmax(idx, carry):
            s = s_ref[idx]
            m = jnp.max(jnp.maximum(s[:, :QB], s[:, QB:]), axis=1, keepdims=True)
            p_ref[idx] = jnp.exp(s - m).astype(BF16)
            m_ref[idx] = jnp.broadcast_to(m, (QB, HEAD_DIM))
            return carry

        def values(idx, v_prev, bi=bi):
            i, rows = windows(idx)
            v_cur = jnp.concatenate([v_ref[rows, :].astype(BF16), ones], axis=1)
            vv = jnp.concatenate([v_cur, v_prev], axis=0)
            o = jnp.dot(p_ref[idx], vv, preferred_element_type=F32)
            den = o[:, HEAD_DIM:]
            ob_ref[bi, rows, :] = o[:, :HEAD_DIM] / den
            lse_ref[bi, rows, :] = m_ref[idx] + jnp.log(den)
            return v_cur

        lax.fori_loop(0, d * nblk, scores, jnp.zeros((QB, HEAD_DIM), BF16), unroll=4)
        lax.fori_loop(0, d * nblk, softmax, 0, unroll=4)
        lax.fori_loop(0, d * nblk, values, jnp.zeros((QB, 2 * HEAD_DIM), BF16), unroll=4)

    CH = 256

    def combine(c, carry):
        rows = pl.ds(pl.multiple_of(c * CH, CH), CH)
        l0, l1, l2 = lse_ref[0, rows, :], lse_ref[1, rows, :], lse_ref[2, rows, :]
        mx = jnp.maximum(jnp.maximum(l0, l1), l2)
        w0, w1, w2 = jnp.exp(l0 - mx), jnp.exp(l1 - mx), jnp.exp(l2 - mx)
        num = w0 * ob_ref[0, rows, :] + w1 * ob_ref[1, rows, :] + w2 * ob_ref[2, rows, :]
        o_ref[rows, :] = (num / (w0 + w1 + w2)).astype(o_ref.dtype)
        return carry

    lax.fori_loop(0, S // CH, combine, 0)


def attn_prompt(z, B, S):
    H = ATTN_HEADS
    assert S % (DILATIONS[-1][1] * QUERY_BLOCK) == 0
    return pl.pallas_call(
        functools.partial(_attn_prompt_body, S=S),
        grid=(B, H),
        in_specs=[
            pl.BlockSpec((S, HEAD_DIM), lambda b, h: (b, h)),
            pl.BlockSpec((S, HEAD_DIM), lambda b, h: (b, H + h)),
            pl.BlockSpec((S, HEAD_DIM), lambda b, h: (b, 2 * H + h)),
        ],
        out_specs=pl.BlockSpec((S, HEAD_DIM), lambda b, h: (b, h)),
        out_shape=jax.ShapeDtypeStruct((B * S, ATTN_WIDTH), BF16),
        scratch_shapes=[
            pltpu.VMEM((len(DILATIONS), S, HEAD_DIM), F32),
            pltpu.VMEM((len(DILATIONS), S, HEAD_DIM), F32),
            pltpu.VMEM((S // QUERY_BLOCK, QUERY_BLOCK, 2 * QUERY_BLOCK), F32),
            pltpu.VMEM((S // QUERY_BLOCK, QUERY_BLOCK, 2 * QUERY_BLOCK), BF16),
            pltpu.VMEM((S // QUERY_BLOCK, QUERY_BLOCK, HEAD_DIM), F32),
        ],
        compiler_params=_cparams("parallel", "parallel"),
        name="attn_prompt",
    )(z, z, z)


def _attn_sample_body(z_ref, k1_ref, k4_ref, k16_ref, v1_ref, v4_ref, v16_ref, o_ref, *, bb):
    H = ATTN_HEADS
    for b in range(bb):
        q = z_ref[b, 0:H, :] * ATTN_SCALE
        k_new = z_ref[b, H:2 * H, :]
        v_new = z_ref[b, 2 * H:3 * H, :]
        s_new = jnp.sum(k_new * q, axis=-1, keepdims=True)
        s_win = [jnp.sum(kr[b] * q, axis=-1, keepdims=True)
                 for kr in (k1_ref, k4_ref, k16_ref)]
        m = s_new
        for s in s_win:
            m = jnp.maximum(m, jnp.max(s, axis=0))
        p_new = jnp.exp(s_new - m) * float(len(DILATIONS))
        den = p_new
        acc = p_new * v_new
        for s, vr in zip(s_win, (v1_ref, v4_ref, v16_ref)):
            p = jnp.exp(s - m)
            den = den + jnp.sum(p, axis=0)
            acc = acc + jnp.sum(p * vr[b], axis=0)
        o_ref[b] = acc / den


def attn_sample(z, cache_k, cache_v, bb=4):
    Bs = z.shape[0]
    wbuf = cache_k.shape[1]
    QB, H = QUERY_BLOCK, ATTN_HEADS
    assert wbuf == PAST_LEN == WINDOW_MAX and Bs % bb == 0

    def views(c):
        out, specs = [], []
        for window, d in DILATIONS:
            n = wbuf // d
            if d == 1:
                out.append(c)
                specs.append(pl.BlockSpec((bb, QB, H, HEAD_DIM), lambda i, n=n: (i, n // QB - 1, 0, 0)))
            else:
                out.append(c.reshape(Bs, n, d, H, HEAD_DIM))
                specs.append(pl.BlockSpec((bb, QB, None, H, HEAD_DIM),
                                          lambda i, n=n: (i, n // QB - 1, 0, 0, 0)))
        return out, specs

    kv, kspecs = views(cache_k)
    vv, vspecs = views(cache_v)
    z3 = z.reshape(Bs, MAIN_COLS // HEAD_DIM, HEAD_DIM)
    out = pl.pallas_call(
        functools.partial(_attn_sample_body, bb=bb),
        grid=(Bs // bb,),
        in_specs=[pl.BlockSpec((bb, MAIN_COLS // HEAD_DIM, HEAD_DIM), lambda i: (i, 0, 0))] + kspecs + vspecs,
        out_specs=pl.BlockSpec((bb, H, HEAD_DIM), lambda i: (i, 0, 0)),
        out_shape=jax.ShapeDtypeStruct((Bs, H, HEAD_DIM), F32),
        compiler_params=_cparams("parallel"),
        name="attn_sample",
    )(z3, *kv, *vv)
    return out.reshape(Bs, ATTN_WIDTH).astype(BF16)


def _gla_prompt_body(q_ref, k_ref, v_ref, go_ref, la_ref, gn_ref, o_ref, st_ref, sT_ref, b_ref, oi_ref,
                     *, B, C):
    c = pl.program_id(0)
    nsub = C // GLA_SUB

    @pl.when(c == 0)
    def _():
        sT_ref[...] = jnp.zeros_like(sT_ref)

    tri = (lax.broadcasted_iota(jnp.int32, (C, C), 1)
           <= lax.broadcasted_iota(jnp.int32, (C, C), 0)).astype(F32)
    srow = lax.broadcasted_iota(jnp.int32, (GLA_SUB, 1), 0)
    pairs = [(n, h) for n in range(B) for h in range(GLA_HEADS)]

    for n, h in pairs:
        kl = pl.ds(h * GLA_DK, GLA_DK)
        vl = pl.ds(h * GLA_DV, GLA_DV)
        b = jnp.dot(tri, la_ref[n, :, kl], precision=HIGHEST, preferred_element_type=F32)
        b_ref[n, h] = b
        qh = q_ref[n, :, kl] * (GLA_DK ** -0.5)
        kh = k_ref[n, :, kl]
        vh = v_ref[n, :, vl]
        b_end = b[C - 1:C, :]
        sT = sT_ref[n, h]
        oi_ref[n, h] = lax.dot_general((qh * jnp.exp(b)).astype(BF16), sT.astype(BF16), NT,
                                       preferred_element_type=F32)
        k_end = (kh * jnp.exp(b_end - b)).astype(BF16)
        sT_ref[n, h] = sT * jnp.exp(b_end) + lax.dot_general(vh.astype(BF16), k_end, TN,
                                                             preferred_element_type=F32)
        for I in range(1, nsub):
            r0 = I * GLA_SUB
            cI = b[r0 - 1:r0, :]
            qI = (qh[r0:r0 + GLA_SUB] * jnp.exp(b[r0:r0 + GLA_SUB] - cI)).astype(BF16)
            kI = (kh[:r0] * jnp.exp(cI - b[:r0])).astype(BF16)
            a = lax.dot_general(qI, kI, NT, preferred_element_type=F32)
            oi_ref[n, h, r0:r0 + GLA_SUB, :] += jnp.dot(a.astype(BF16), vh[:r0].astype(BF16),
                                                        preferred_element_type=F32)

    def diag(I, carry):
        rows = pl.ds(pl.multiple_of(I * GLA_SUB, GLA_SUB), GLA_SUB)
        for n, h in pairs:
            bI = b_ref[n, h, rows, :]
            qI = q_ref[n, rows, pl.ds(h * GLA_DK, GLA_DK)] * (GLA_DK ** -0.5)
            kI = k_ref[n, rows, pl.ds(h * GLA_DK, GLA_DK)]
            vI = v_ref[n, rows, pl.ds(h * GLA_DV, GLA_DV)]
            acc = jnp.zeros((GLA_SUB, GLA_DV), F32)
            for s in range(GLA_SUB):
                e = jnp.exp(jnp.minimum(bI - bI[s:s + 1, :], 0.0))
                a_col = jnp.sum(qI * kI[s:s + 1, :] * e, axis=1, keepdims=True)
                a_col = jnp.where(srow >= s, a_col, 0.0)
                acc += a_col * vI[s:s + 1, :]
            oi_ref[n, h, rows, :] += acc
        return carry

    lax.fori_loop(0, nsub, diag, 0)

    for n, h in pairs:
        vl = pl.ds(h * GLA_DV, GLA_DV)
        g = go_ref[n, :, vl]
        o_ref[n, :, vl] = (_rms(oi_ref[n, h], gn_ref[...]) * (g * jax.nn.sigmoid(g))).astype(o_ref.dtype)

    @pl.when(c == pl.num_programs(0) - 1)
    def _():
        for n, h in pairs:
            st_ref[n, h] = sT_ref[n, h].T


def gla_prompt(z, log_a, gla_norm, B, S, C=128):
    assert S % C == 0 and C % GLA_SUB == 0
    q_blk = 3 * ATTN_WIDTH // GLA_KW
    v_blk = (3 * ATTN_WIDTH + 2 * GLA_KW) // GLA_WIDTH
    z3 = z.reshape(B, S, z.shape[1])
    la3 = log_a.reshape(B, S, GLA_KW)
    o, state = pl.pallas_call(
        functools.partial(_gla_prompt_body, B=B, C=C),
        grid=(S // C,),
        in_specs=[
            pl.BlockSpec((B, C, GLA_KW), lambda c: (0, c, q_blk)),
            pl.BlockSpec((B, C, GLA_KW), lambda c: (0, c, q_blk + 1)),
            pl.BlockSpec((B, C, GLA_WIDTH), lambda c: (0, c, v_blk)),
            pl.BlockSpec((B, C, GLA_WIDTH), lambda c: (0, c, v_blk + 1)),
            pl.BlockSpec((B, C, GLA_KW), lambda c: (0, c, 0)),
            pl.BlockSpec((1, GLA_DV), lambda c: (0, 0)),
        ],
        out_specs=[
            pl.BlockSpec((B, C, GLA_WIDTH), lambda c: (0, c, 0)),
            pl.BlockSpec((B, GLA_HEADS, GLA_DK, GLA_DV), lambda c: (0, 0, 0, 0)),
        ],
        out_shape=[
            jax.ShapeDtypeStruct((B, S, GLA_WIDTH), BF16),
            jax.ShapeDtypeStruct((B, GLA_HEADS, GLA_DK, GLA_DV), F32),
        ],
        scratch_shapes=[
            pltpu.VMEM((B, GLA_HEADS, GLA_DV, GLA_DK), F32),
            pltpu.VMEM((B, GLA_HEADS, C, GLA_DK), F32),
            pltpu.VMEM((B, GLA_HEADS, C, GLA_DV), F32),
        ],
        compiler_params=_cparams("arbitrary"),
        name="gla_prompt",
    )(z3, z3, z3, z3, la3, gla_norm)
    return o.reshape(B * S, GLA_WIDTH), state


def _gla_sample_body(q_ref, k_ref, v_ref, go_ref, la_ref, gn_ref, s_ref, o_ref, so_ref, *, bb):
    for h in range(GLA_HEADS):
        kl = pl.ds(h * GLA_DK, GLA_DK)
        vl = pl.ds(h * GLA_DV, GLA_DV)
        aT = jnp.exp(la_ref[:, kl]).T
        kT = k_ref[:, kl].T
        qT = (q_ref[:, kl] * (GLA_DK ** -0.5)).T
        outs = []
        for b in range(bb):
            s_new = aT[:, b:b + 1] * s_ref[b, h] + kT[:, b:b + 1] * v_ref[b:b + 1, vl]
            so_ref[b, h] = s_new
            outs.append(jnp.sum(qT[:, b:b + 1] * s_new, axis=0, keepdims=True))
        o = jnp.concatenate(outs, axis=0)
        g = go_ref[:, vl]
        o_ref[:, vl] = (_rms(o, gn_ref[...]) * (g * jax.nn.sigmoid(g))).astype(o_ref.dtype)


def gla_sample(z, log_a, gla_norm, state, bb=SUBLANES):
    Bs = z.shape[0]
    assert Bs % bb == 0
    q_blk = 3 * ATTN_WIDTH // GLA_KW
    v_blk = (3 * ATTN_WIDTH + 2 * GLA_KW) // GLA_WIDTH
    st_spec = pl.BlockSpec((bb, GLA_HEADS, GLA_DK, GLA_DV), lambda i: (i, 0, 0, 0))
    return pl.pallas_call(
        functools.partial(_gla_sample_body, bb=bb),
        grid=(Bs // bb,),
        in_specs=[
            pl.BlockSpec((bb, GLA_KW), lambda i: (i, q_blk)),
            pl.BlockSpec((bb, GLA_KW), lambda i: (i, q_blk + 1)),
            pl.BlockSpec((bb, GLA_WIDTH), lambda i: (i, v_blk)),
            pl.BlockSpec((bb, GLA_WIDTH), lambda i: (i, v_blk + 1)),
            pl.BlockSpec((bb, GLA_KW), lambda i: (i, 0)),
            pl.BlockSpec((1, GLA_DV), lambda i: (0, 0)),
            st_spec,
        ],
        out_specs=[pl.BlockSpec((bb, GLA_WIDTH), lambda i: (i, 0)), st_spec],
        out_shape=[
            jax.ShapeDtypeStruct((Bs, GLA_WIDTH), BF16),
            jax.ShapeDtypeStruct(state.shape, F32),
        ],
        compiler_params=_cparams("parallel"),
        name="gla_sample",
    )(z, z, z, z, log_a, gla_norm, state)


def _proj_residual_body(a_ref, b_ref, wa_ref, wb_ref, res_ref, o_ref):
    acc = jnp.dot(a_ref[...], wa_ref[...], preferred_element_type=F32)
    acc += jnp.dot(b_ref[...], wb_ref[...], preferred_element_type=F32)
    o_ref[...] = res_ref[...] + acc


def proj_residual(a, b, wa, wb, res, tm, tn=512):
    T, Ka = a.shape
    Kb = b.shape[1]
    N = wa.shape[1]
    assert T % tm == 0 and N % tn == 0
    return pl.pallas_call(
        _proj_residual_body,
        grid=(T // tm, N // tn),
        in_specs=[
            pl.BlockSpec((tm, Ka), lambda i, j: (i, 0)),
            pl.BlockSpec((tm, Kb), lambda i, j: (i, 0)),
            pl.BlockSpec((Ka, tn), lambda i, j: (0, j)),
            pl.BlockSpec((Kb, tn), lambda i, j: (0, j)),
            pl.BlockSpec((tm, tn), lambda i, j: (i, j)),
        ],
        out_specs=pl.BlockSpec((tm, tn), lambda i, j: (i, j)),
        out_shape=jax.ShapeDtypeStruct((T, N), F32),
        compiler_params=_cparams("parallel", "arbitrary"),
        name="proj_residual",
    )(a, b, wa, wb, res)


def _sorting_network(n):
    assert n & (n - 1) == 0
    pairs = []
    p = 1
    while p < n:
        k = p
        while k >= 1:
            for j in range(k % p, n - k, 2 * k):
                for i in range(min(k, n - j - k)):
                    if (i + j) // (2 * p) == (i + j + k) // (2 * p):
                        pairs.append((i + j, i + j + k))
            k //= 2
        p *= 2
    return pairs


def _peer_route_body(h_ref, g_ref, w_ref, k1_ref, k2_ref, xn_ref, s1_ref, s2_ref, thr_ref, q_ref, *, tm):
    xn = _rms(h_ref[...], g_ref[...]).astype(BF16)
    xn_ref[...] = xn
    q_ref[...] = jnp.dot(xn, w_ref[...], preferred_element_type=F32)

    K = PEER_TOPK
    KR = -(-(K + 1) // SUBLANES) * SUBLANES
    rowk = lax.broadcasted_iota(jnp.int32, (KR, tm), 0)
    row8 = lax.broadcasted_iota(jnp.int32, (SUBLANES, tm), 0)
    rowh = lax.broadcasted_iota(jnp.int32, (PEER_HEADS, tm), 0)
    thr_all = jnp.zeros((PEER_HEADS, tm), F32)

    def top_values(s):
        n = PEER_NKEYS // SUBLANES
        lists = [s[r * SUBLANES:(r + 1) * SUBLANES] for r in range(n)]
        for a, b in _sorting_network(n):
            lists[a], lists[b] = jnp.maximum(lists[a], lists[b]), jnp.minimum(lists[a], lists[b])
        vals = jnp.full((KR, tm), -jnp.inf, F32)
        for it in range(K + 1):
            mx = jnp.max(lists[0], axis=0, keepdims=True)
            vals = jnp.where(rowk == it, mx, vals)
            hit = lists[0] == mx
            for r in range(min(n, K - it)):
                below = lists[r + 1] if r + 1 < n else -jnp.inf
                lists[r] = jnp.where(hit, below, lists[r])
        return vals

    for h in range(PEER_HEADS):
        q1 = q_ref[:, h * PEER_QDIM:h * PEER_QDIM + PEER_HALF]
        q2 = q_ref[:, h * PEER_QDIM + PEER_HALF:(h + 1) * PEER_QDIM]
        s1 = lax.dot_general(k1_ref[h], q1, NT, precision=HIGHEST, preferred_element_type=F32)
        s2 = lax.dot_general(k2_ref[h], q2, NT, precision=HIGHEST, preferred_element_type=F32)
        v1 = top_values(s1)
        v2 = top_values(s2)
        pieces = [v1[0:1] + v2[r:r + SUBLANES] for r in range(0, KR, SUBLANES)]
        for a in range(1, SUBLANES):
            pieces.append(jnp.where(row8 < (K + 1) // (a + 1), v1[a:a + 1] + v2[0:SUBLANES], -jnp.inf))
        pieces += [v1[r:r + SUBLANES] + v2[0:1] for r in range(SUBLANES, KR, SUBLANES)]
        cand = jnp.concatenate(pieces, axis=0)
        m = v1[0:1] + v2[0:1]
        z = jnp.zeros((1, tm), F32)
        kth = m
        for it in range(K):
            kth = jnp.max(cand, axis=0, keepdims=True)
            z = z + jnp.exp(kth - m)
            cand = jnp.where(cand == kth, -jnp.inf, cand)
        nxt = jnp.max(cand, axis=0, keepdims=True)
        c = m + jnp.log(z)
        s1_ref[h] = (s1 - c) * LOG2E - 1.0
        s2_ref[h] = s2 * LOG2E
        thr_all = jnp.where(rowh == h, (0.5 * (kth + nxt) - c) * LOG2E - 1.0, thr_all)
    thr_ref[...] = thr_all


def peer_route(h, g, w, keys1, keys2, tm):
    T, D = h.shape
    N = w.shape[1]
    assert T % tm == 0 and tm % LANES == 0 and (PEER_TOPK + 1) // 2 <= SUBLANES and N == PEER_HEADS * PEER_QDIM
    tile = pl.BlockSpec((PEER_HEADS, PEER_NKEYS, tm), lambda i: (0, 0, i))
    tile_shape = jax.ShapeDtypeStruct((PEER_HEADS, PEER_NKEYS, T), F32)
    kspec = pl.BlockSpec((PEER_HEADS, PEER_NKEYS, PEER_HALF), lambda i: (0, 0, 0))
    return pl.pallas_call(
        functools.partial(_peer_route_body, tm=tm),
        grid=(T // tm,),
        in_specs=[
            pl.BlockSpec((tm, D), lambda i: (i, 0)),
            pl.BlockSpec((1, D), lambda i: (0, 0)),
            pl.BlockSpec((D, N), lambda i: (0, 0)),
            kspec, kspec,
        ],
        out_specs=[pl.BlockSpec((tm, D), lambda i: (i, 0)), tile, tile,
                   pl.BlockSpec((PEER_HEADS, tm), lambda i: (0, i))],
        out_shape=[jax.ShapeDtypeStruct((T, D), BF16), tile_shape, tile_shape,
                   jax.ShapeDtypeStruct((PEER_HEADS, T), F32)],
        scratch_shapes=[pltpu.VMEM((tm, N), F32)],
        compiler_params=_cparams("parallel"),
        name="peer_route",
    )(h, g, w, keys1, keys2)


def _peer_dense_body(xn_ref, u_ref, v_ref, s1_ref, s2_ref, thr_ref, h_ref, g_ref, y_ref, *, te, tm):
    e = pl.program_id(1)

    @pl.when(e == 0)
    def _():
        y_ref[...] = h_ref[...]

    rows = []
    for c in range(te // PEER_NKEYS):
        tiles = []
        for tc in range(tm // LANES):
            lanes = pl.ds(tc * LANES, LANES)
            gate = jnp.zeros((PEER_NKEYS, LANES), F32)
            for h in range(PEER_HEADS):
                ssum = s2_ref[h, :, lanes] + s1_ref[h, c:c + 1, lanes]
                gate += jnp.where(ssum >= thr_ref[pl.ds(h, 1), lanes], jnp.exp2(ssum), 0.0)
            tiles.append(gate)
        rows.append(jnp.concatenate(tiles, axis=1))
    half_gate = jnp.concatenate(rows, axis=0).T
    hx = lax.dot_general(xn_ref[...], u_ref[...], NT, preferred_element_type=F32)
    act = (hx + hx * lax.erf(hx * (2.0 ** -0.5))) * half_gate
    y_ref[...] += jnp.dot(act.astype(BF16), v_ref[...], preferred_element_type=F32)

    @pl.when(e == pl.num_programs(1) - 1)
    def _():
        y_ref[...] = _rms(y_ref[...], g_ref[...])


def peer_dense(xn, u, v, s1, s2, thr, h, g, tm):
    T, D = xn.shape
    E = u.shape[0]
    te = SUBLANES * PEER_NKEYS
    assert T % tm == 0 and E % te == 0 and tm % LANES == 0
    return pl.pallas_call(
        functools.partial(_peer_dense_body, te=te, tm=tm),
        grid=(T // tm, E // te),
        in_specs=[
            pl.BlockSpec((tm, D), lambda i, e: (i, 0)),
            pl.BlockSpec((te, D), lambda i, e: (e, 0)),
            pl.BlockSpec((te, D), lambda i, e: (e, 0)),
            pl.BlockSpec((PEER_HEADS, SUBLANES, tm), lambda i, e: (0, e, i)),
            pl.BlockSpec((PEER_HEADS, PEER_NKEYS, tm), lambda i, e: (0, 0, i)),
            pl.BlockSpec((PEER_HEADS, tm), lambda i, e: (0, i)),
            pl.BlockSpec((tm, D), lambda i, e: (i, 0)),
            pl.BlockSpec((1, D), lambda i, e: (0, 0)),
        ],
        out_specs=pl.BlockSpec((tm, D), lambda i, e: (i, 0)),
        out_shape=jax.ShapeDtypeStruct((T, D), F32),
        compiler_params=_cparams("parallel", "arbitrary"),
        name="peer_dense",
    )(xn, u, v, s1, s2, thr, h, g)


def _post_mix(x, o_attn, o_gla, w, tm, tm_proj):
    h = proj_residual(o_attn, o_gla, w["w_o_a"], w["w_o_b"], x, tm_proj)
    xn, s1, s2, thr = peer_route(h, w["norm_ffn"], w["peer_query"], w["peer_keys_1"], w["peer_keys_2"], tm)
    return peer_dense(xn, w["expert_u"], w["expert_v"], s1, s2, thr, h, w["final_norm"], tm)


def _prompt_group(x_prompt, w):
    B, S, _ = x_prompt.shape
    x = x_prompt.reshape(B * S, D_MODEL)
    tm, tm_proj = 512, 1024
    z, log_a, k_win, v_win = in_proj(x, w, jnp.arange(S), tm_proj, min(WINDOW_MAX, S))
    o_attn = attn_prompt(z, B, S)
    o_gla, state = gla_prompt(z, log_a, w["gla_norm"], B, S)
    out = _post_mix(x, o_attn, o_gla, w, tm, tm_proj)
    return out, k_win, v_win, state


def _sample_group(x_sample, cache_k, cache_v, state, w):
    Bs, Ls, _ = x_sample.shape
    assert Ls == 1
    x = x_sample.reshape(Bs, D_MODEL)
    tm = Bs
    z, log_a, k_new, v_new = in_proj(x, w, jnp.full((tm,), PAST_LEN, jnp.int32), tm, tm)
    o_attn = attn_sample(z, cache_k, cache_v)
    o_gla, new_state = gla_sample(z, log_a, w["gla_norm"], state)
    out = _post_mix(x, o_attn, o_gla, w, tm, tm)
    return out, k_new, v_new, new_state


def _prepare(norm_attn, w_in, w_gate_up, b_gate, gla_norm, w_o, norm_ffn, peer_query_w, peer_keys_1,
             peer_keys_2, expert_u, expert_v, final_norm_w):
    assert w_in.shape[0] == 1
    l = 0
    w_in_l = w_in[l]
    row = lambda v: v.reshape(1, -1).astype(F32)
    return {
        "norm_attn": row(norm_attn[l]),
        "w_in": w_in_l.astype(BF16),
        "w_in_lr": jnp.pad(w_in_l[:, MAIN_COLS:], ((0, 0), (0, LANES - GLA_LOWRANK))).astype(BF16),
        "w_gate_up": jnp.pad(w_gate_up[l], ((0, LANES - GLA_LOWRANK), (0, 0))),
        "b_gate": row(b_gate[l]),
        "gla_norm": row(gla_norm[l]),
        "w_o_a": w_o[l][:ATTN_WIDTH].astype(BF16),
        "w_o_b": w_o[l][ATTN_WIDTH:].astype(BF16),
        "norm_ffn": row(norm_ffn[l]),
        "peer_query": peer_query_w[l].astype(BF16),
        "peer_keys_1": peer_keys_1[l],
        "peer_keys_2": peer_keys_2[l],
        "expert_u": expert_u[l].astype(BF16),
        "expert_v": expert_v[l].astype(BF16),
        "final_norm": row(final_norm_w),
    }


def kernel(x_prompt, x_sample, cache_attn_k, cache_attn_v, state_gla, norm_attn, w_in, w_gate_up,
           b_gate, gla_norm, w_o, norm_ffn, peer_query, peer_keys_1, peer_keys_2, expert_u,
           expert_v, final_norm):
    Bp, Lp, _ = x_prompt.shape
    Bs, Ls, _ = x_sample.shape
    w = _prepare(norm_attn, w_in, w_gate_up, b_gate, gla_norm, w_o, norm_ffn, peer_query, peer_keys_1,
                 peer_keys_2, expert_u, expert_v, final_norm)
    win_p = min(WINDOW_MAX, Lp)
    yp, kp, vp, sp = _prompt_group(x_prompt, w)
    ys, ks, vs, ss = _sample_group(x_sample, cache_attn_k[0], cache_attn_v[0], state_gla[0], w)
    heads = lambda a, B, L: a.reshape(1, B, L, ATTN_HEADS, HEAD_DIM)
    return (yp.reshape(Bp, Lp, D_MODEL), ys.reshape(Bs, Ls, D_MODEL),
            heads(kp, Bp, win_p), heads(vp, Bp, win_p), sp[None],
            heads(ks, Bs, Ls), heads(vs, Bs, Ls), ss[None])


def sample_group(inp):
    names = ("norm_attn", "w_in", "w_gate_up", "b_gate", "gla_norm", "w_o", "norm_ffn", "peer_query",
             "peer_keys_1", "peer_keys_2", "expert_u", "expert_v", "final_norm")
    w = _prepare(*[inp[n] for n in names])
    out, k, v, st = _sample_group(inp["x_sample"], inp["cache_attn_k"][0], inp["cache_attn_v"][0],
                                  inp["state_gla"][0], w)
    Bs = out.shape[0]
    return (out, k.reshape(Bs, 1, ATTN_HEADS, HEAD_DIM), v.reshape(Bs, 1, ATTN_HEADS, HEAD_DIM), st)
```

```python
import functools
import math

import jax
import jax.numpy as jnp
import numpy as np
from jax import lax
from jax.experimental import pallas as pl
from jax.experimental.pallas import tpu as pltpu

F32 = jnp.float32
BF16 = jnp.bfloat16

D_MODEL = 2048
PAST_LEN = 2048
HEAD_DIM = 128
ATTN_HEADS = 8
ATTN_WIDTH = ATTN_HEADS * HEAD_DIM
DILATIONS = ((128, 1), (512, 4), (2048, 16))
WINDOW_MAX = 2048
QUERY_BLOCK = 128
ATTN_SCALE = HEAD_DIM ** -0.5
ROPE_THETA = 500000.0
ROT_DIM = HEAD_DIM // 4
ROT_HALF = ROT_DIM // 2
GLA_HEADS = 4
GLA_WIDTH = D_MODEL - ATTN_WIDTH
GLA_DV = GLA_WIDTH // GLA_HEADS
GLA_DK = GLA_DV // 2
GLA_KW = GLA_HEADS * GLA_DK
GLA_LOWRANK = 16
GLA_TAU = 16.0
GLA_SUB = 16
PEER_HEADS = 8
PEER_NKEYS = 128
PEER_EXPERTS = PEER_NKEYS * PEER_NKEYS
PEER_QDIM = 256
PEER_HALF = PEER_QDIM // 2
PEER_TOPK = 16
NORM_EPS = 1e-6
MAIN_COLS = 3 * ATTN_WIDTH + 2 * GLA_KW + 2 * GLA_WIDTH

LANES = 128
SUBLANES = 8
VMEM_LIMIT = 56 * 1024 * 1024
NEG = -1e30
LOG2E = 1.0 / math.log(2.0)
HIGHEST = lax.Precision.HIGHEST
NT = (((1,), (1,)), ((), ()))
TN = (((0,), (0,)), ((), ()))


def _cparams(*sem):
    return pltpu.CompilerParams(dimension_semantics=sem, vmem_limit_bytes=VMEM_LIMIT)


def _rms(x, g):
    return x * lax.rsqrt(jnp.mean(x * x, axis=-1, keepdims=True) + NORM_EPS) * g


def _in_proj_body(x_ref, g_ref, w_ref, wlr_ref, wup_ref, bg_ref, cos_ref, sin_ref,
                  z_ref, la_ref, kwin_ref, vwin_ref, xn_ref, *, tn, nb, wb):
    i = pl.program_id(0)
    j = pl.program_id(1)
    in_win = i % nb >= nb - wb
    kv_tiles = ATTN_WIDTH // tn

    @pl.when(j == 0)
    def _():
        xn = _rms(x_ref[...], g_ref[...]).astype(BF16)
        xn_ref[...] = xn
        lr = jnp.dot(xn, wlr_ref[...], preferred_element_type=F32)
        zg = jnp.dot(lr, wup_ref[...], precision=HIGHEST, preferred_element_type=F32) + bg_ref[...]
        la_ref[...] = (jnp.minimum(zg, 0.0) - jnp.log1p(jnp.exp(-jnp.abs(zg)))) * (1.0 / GLA_TAU)

    acc = jnp.dot(xn_ref[...], w_ref[...], preferred_element_type=F32)

    @pl.when(j < 2 * ATTN_WIDTH // tn)
    def _():
        lane = lax.broadcasted_iota(jnp.int32, (acc.shape[0], HEAD_DIM), 1)
        cos = cos_ref[...]
        sin = sin_ref[...]
        for c in range(tn // HEAD_DIM):
            a = acc[:, c * HEAD_DIM:(c + 1) * HEAD_DIM]
            partner = jnp.where(lane < ROT_HALF, pltpu.roll(a, HEAD_DIM - ROT_HALF, 1),
                                pltpu.roll(a, ROT_HALF, 1))
            z_ref[:, c * HEAD_DIM:(c + 1) * HEAD_DIM] = a * cos + partner * sin

        @pl.when(jnp.logical_and(in_win, j >= kv_tiles))
        def _():
            kwin_ref[...] = z_ref[...]

    @pl.when(j >= 2 * ATTN_WIDTH // tn)
    def _():
        z_ref[...] = acc

        @pl.when(jnp.logical_and(in_win, j < 3 * kv_tiles))
        def _():
            vwin_ref[...] = acc


def _rotary_tables(pos):
    inv = jnp.exp(-math.log(ROPE_THETA) * jnp.arange(ROT_HALF, dtype=F32) * (2.0 / ROT_DIM))
    ang = pos.astype(F32)[:, None] * inv[None, :]
    n = pos.shape[0]
    cos = jnp.concatenate([jnp.cos(ang), jnp.cos(ang), jnp.ones((n, HEAD_DIM - ROT_DIM), F32)], axis=1)
    sin = jnp.concatenate([-jnp.sin(ang), jnp.sin(ang), jnp.zeros((n, HEAD_DIM - ROT_DIM), F32)], axis=1)
    return cos, sin


def in_proj(x, w, pos, tm, win, tn=512):
    T, D = x.shape
    P = pos.shape[0]
    assert T % tm == 0 and MAIN_COLS % tn == 0 and P % tm == 0 and ATTN_WIDTH % tn == 0 and win % tm == 0
    nb, wb = P // tm, win // tm
    kv_tiles = ATTN_WIDTH // tn
    cos, sin = _rotary_tables(pos)
    const = lambda i, j: (0, 0)

    def win_spec(first_tile):
        def index(i, j):
            ib = i % nb
            row = (i // nb) * wb + jnp.maximum(ib - (nb - wb), 0)
            col = jnp.where(ib >= nb - wb, jnp.clip(j - first_tile, 0, kv_tiles - 1), 0)
            return row, col
        return pl.BlockSpec((tm, tn), index)

    win_shape = jax.ShapeDtypeStruct((T // P * win, ATTN_WIDTH), F32)
    return pl.pallas_call(
        functools.partial(_in_proj_body, tn=tn, nb=nb, wb=wb),
        grid=(T // tm, MAIN_COLS // tn),
        in_specs=[
            pl.BlockSpec((tm, D), lambda i, j: (i, 0)),
            pl.BlockSpec((1, D), const),
            pl.BlockSpec((D, tn), lambda i, j: (0, j)),
            pl.BlockSpec((D, LANES), const),
            pl.BlockSpec((LANES, GLA_KW), const),
            pl.BlockSpec((1, GLA_KW), const),
            pl.BlockSpec((tm, HEAD_DIM), lambda i, j: (i % (P // tm), 0)),
            pl.BlockSpec((tm, HEAD_DIM), lambda i, j: (i % (P // tm), 0)),
        ],
        out_specs=[
            pl.BlockSpec((tm, tn), lambda i, j: (i, j)),
            pl.BlockSpec((tm, GLA_KW), lambda i, j: (i, 0)),
            win_spec(kv_tiles),
            win_spec(2 * kv_tiles),
        ],
        out_shape=[
            jax.ShapeDtypeStruct((T, MAIN_COLS), F32),
            jax.ShapeDtypeStruct((T, GLA_KW), F32),
            win_shape,
            win_shape,
        ],
        scratch_shapes=[pltpu.VMEM((tm, D), BF16)],
        compiler_params=_cparams("arbitrary", "arbitrary"),
        name="in_proj",
    )(x, w["norm_attn"], w["w_in"], w["w_in_lr"], w["w_gate_up"], w["b_gate"], cos, sin)


def _attn_prompt_body(q_ref, k_ref, v_ref, o_ref, ob_ref, lse_ref, s_ref, p_ref, m_ref, *, S):
    QB = QUERY_BLOCK
    row = lax.broadcasted_iota(jnp.int32, (QB, QB), 0)
    col = lax.broadcasted_iota(jnp.int32, (QB, QB), 1)
    bias = jnp.concatenate([jnp.where(col <= row, 0.0, NEG), jnp.where(col >= row, 0.0, NEG)], axis=1)
    lane2 = lax.broadcasted_iota(jnp.int32, (QB, 2 * QB), 1)
    ones = jnp.ones((QB, HEAD_DIM), BF16)

    for bi, (window, d) in enumerate(DILATIONS):
        assert window == d * QB
        nblk = S // d // QB
        assert d * nblk == s_ref.shape[0]

        def windows(idx, d=d, nblk=nblk):
            r = idx // nblk
            i = idx % nblk
            start = r + i * (d * QB)
            if d == 1:
                return i, pl.ds(pl.multiple_of(start, QB), QB)
            return i, pl.ds(start, QB, stride=d)

        def scores(idx, k_prev):
            i, rows = windows(idx)
            qs = (q_ref[rows, :] * ATTN_SCALE).astype(BF16)
            k_cur = k_ref[rows, :].astype(BF16)
            kk = jnp.concatenate([k_cur, k_prev], axis=0)
            s = lax.dot_general(qs, kk, NT, preferred_element_type=F32) + bias
            s_ref[idx] = jnp.where(jnp.logical_or(lane2 < QB, i > 0), s, NEG)
            return k_cur

        def softmax(idx, carry):
            s = s_ref[idx]
            m = jnp.max(jnp.maximum(s[:, :QB], s[:, QB:]), axis=1, keepdims=True)
            p_ref[idx] = jnp.exp(s - m).astype(BF16)
            m_ref[idx] = jnp.broadcast_to(m, (QB, HEAD_DIM))
            return carry

        def values(idx, v_prev, bi=bi):
            i, rows = windows(idx)
            v_cur = jnp.concatenate([v_ref[rows, :].astype(BF16), ones], axis=1)
            vv = jnp.concatenate([v_cur, v_prev], axis=0)
            o = jnp.dot(p_ref[idx], vv, preferred_element_type=F32)
            den = o[:, HEAD_DIM:]
            ob_ref[bi, rows, :] = o[:, :HEAD_DIM] / den
            lse_ref[bi, rows, :] = m_ref[idx] + jnp.log(den)
            return v_cur

        lax.fori_loop(0, d * nblk, scores, jnp.zeros((QB, HEAD_DIM), BF16), unroll=4)
        lax.fori_loop(0, d * nblk, softmax, 0, unroll=4)
        lax.fori_loop(0, d * nblk, values, jnp.zeros((QB, 2 * HEAD_DIM), BF16), unroll=4)

    CH = 256

    def combine(c, carry):
        rows = pl.ds(pl.multiple_of(c * CH, CH), CH)
        l0, l1, l2 = lse_ref[0, rows, :], lse_ref[1, rows, :], lse_ref[2, rows, :]
        mx = jnp.maximum(jnp.maximum(l0, l1), l2)
        w0, w1, w2 = jnp.exp(l0 - mx), jnp.exp(l1 - mx), jnp.exp(l2 - mx)
        num = w0 * ob_ref[0, rows, :] + w1 * ob_ref[1, rows, :] + w2 * ob_ref[2, rows, :]
        o_ref[rows, :] = (num / (w0 + w1 + w2)).astype(o_ref.dtype)
        return carry

    lax.fori_loop(0, S // CH, combine, 0)


def attn_prompt(z, B, S):
    H = ATTN_HEADS
    assert S % (DILATIONS[-1][1] * QUERY_BLOCK) == 0
    return pl.pallas_call(
        functools.partial(_attn_prompt_body, S=S),
        grid=(B, H),
        in_specs=[
            pl.BlockSpec((S, HEAD_DIM), lambda b, h: (b, h)),
            pl.BlockSpec((S, HEAD_DIM), lambda b, h: (b, H + h)),
            pl.BlockSpec((S, HEAD_DIM), lambda b, h: (b, 2 * H + h)),
        ],
        out_specs=pl.BlockSpec((S, HEAD_DIM), lambda b, h: (b, h)),
        out_shape=jax.ShapeDtypeStruct((B * S, ATTN_WIDTH), BF16),
        scratch_shapes=[
            pltpu.VMEM((len(DILATIONS), S, HEAD_DIM), F32),
            pltpu.VMEM((len(DILATIONS), S, HEAD_DIM), F32),
            pltpu.VMEM((S // QUERY_BLOCK, QUERY_BLOCK, 2 * QUERY_BLOCK), F32),
            pltpu.VMEM((S // QUERY_BLOCK, QUERY_BLOCK, 2 * QUERY_BLOCK), BF16),
            pltpu.VMEM((S // QUERY_BLOCK, QUERY_BLOCK, HEAD_DIM), F32),
        ],
        compiler_params=_cparams("parallel", "parallel"),
        name="attn_prompt",
    )(z, z, z)


def _attn_sample_body(z_ref, k1_ref, k4_ref, k16_ref, v1_ref, v4_ref, v16_ref, o_ref, *, bb):
    H = ATTN_HEADS
    for b in range(bb):
        q = z_ref[b, 0:H, :] * ATTN_SCALE
        k_new = z_ref[b, H:2 * H, :]
        v_new = z_ref[b, 2 * H:3 * H, :]
        s_new = jnp.sum(k_new * q, axis=-1, keepdims=True)
        s_win = [jnp.sum(kr[b] * q, axis=-1, keepdims=True)
                 for kr in (k1_ref, k4_ref, k16_ref)]
        m = s_new
        for s in s_win:
            m = jnp.maximum(m, jnp.max(s, axis=0))
        p_new = jnp.exp(s_new - m) * float(len(DILATIONS))
        den = p_new
        acc = p_new * v_new
        for s, vr in zip(s_win, (v1_ref, v4_ref, v16_ref)):
            p = jnp.exp(s - m)
            den = den + jnp.sum(p, axis=0)
            acc = acc + jnp.sum(p * vr[b], axis=0)
        o_ref[b] = acc / den


def attn_sample(z, cache_k, cache_v, bb=4):
    Bs = z.shape[0]
    wbuf = cache_k.shape[1]
    QB, H = QUERY_BLOCK, ATTN_HEADS
    assert wbuf == PAST_LEN == WINDOW_MAX and Bs % bb == 0

    def views(c):
        out, specs = [], []
        for window, d in DILATIONS:
            n = wbuf // d
            if d == 1:
                out.append(c)
                specs.append(pl.BlockSpec((bb, QB, H, HEAD_DIM), lambda i, n=n: (i, n // QB - 1, 0, 0)))
            else:
                out.append(c.reshape(Bs, n, d, H, HEAD_DIM))
                specs.append(pl.BlockSpec((bb, QB, None, H, HEAD_DIM),
                                          lambda i, n=n: (i, n // QB - 1, 0, 0, 0)))
        return out, specs

    kv, kspecs = views(cache_k)
    vv, vspecs = views(cache_v)
    z3 = z.reshape(Bs, MAIN_COLS // HEAD_DIM, HEAD_DIM)
    out = pl.pallas_call(
        functools.partial(_attn_sample_body, bb=bb),
        grid=(Bs // bb,),
        in_specs=[pl.BlockSpec((bb, MAIN_COLS // HEAD_DIM, HEAD_DIM), lambda i: (i, 0, 0))] + kspecs + vspecs,
        out_specs=pl.BlockSpec((bb, H, HEAD_DIM), lambda i: (i, 0, 0)),
        out_shape=jax.ShapeDtypeStruct((Bs, H, HEAD_DIM), F32),
        compiler_params=_cparams("parallel"),
        name="attn_sample",
    )(z3, *kv, *vv)
    return out.reshape(Bs, ATTN_WIDTH).astype(BF16)


def _gla_prompt_body(q_ref, k_ref, v_ref, go_ref, la_ref, gn_ref, o_ref, st_ref, sT_ref, b_ref, oi_ref,
                     *, B, C):
    c = pl.program_id(0)
    nsub = C // GLA_SUB

    @pl.when(c == 0)
    def _():
        sT_ref[...] = jnp.zeros_like(sT_ref)

    tri = (lax.broadcasted_iota(jnp.int32, (C, C), 1)
           <= lax.broadcasted_iota(jnp.int32, (C, C), 0)).astype(F32)
    srow = lax.broadcasted_iota(jnp.int32, (GLA_SUB, 1), 0)
    pairs = [(n, h) for n in range(B) for h in range(GLA_HEADS)]

    for n, h in pairs:
        kl = pl.ds(h * GLA_DK, GLA_DK)
        vl = pl.ds(h * GLA_DV, GLA_DV)
        b = jnp.dot(tri, la_ref[n, :, kl], precision=HIGHEST, preferred_element_type=F32)
        b_ref[n, h] = b
        qh = q_ref[n, :, kl] * (GLA_DK ** -0.5)
        kh = k_ref[n, :, kl]
        vh = v_ref[n, :, vl]
        b_end = b[C - 1:C, :]
        sT = sT_ref[n, h]
        oi_ref[n, h] = lax.dot_general((qh * jnp.exp(b)).astype(BF16), sT.astype(BF16), NT,
                                       preferred_element_type=F32)
        k_end = (kh * jnp.exp(b_end - b)).astype(BF16)
        sT_ref[n, h] = sT * jnp.exp(b_end) + lax.dot_general(vh.astype(BF16), k_end, TN,
                                                             preferred_element_type=F32)
        for I in range(1, nsub):
            r0 = I * GLA_SUB
            cI = b[r0 - 1:r0, :]
            qI = (qh[r0:r0 + GLA_SUB] * jnp.exp(b[r0:r0 + GLA_SUB] - cI)).astype(BF16)
            kI = (kh[:r0] * jnp.exp(cI - b[:r0])).astype(BF16)
            a = lax.dot_general(qI, kI, NT, preferred_element_type=F32)
            oi_ref[n, h, r0:r0 + GLA_SUB, :] += jnp.dot(a.astype(BF16), vh[:r0].astype(BF16),
                                                        preferred_element_type=F32)

    def diag(I, carry):
        rows = pl.ds(pl.multiple_of(I * GLA_SUB, GLA_SUB), GLA_SUB)
        for n, h in pairs:
            bI = b_ref[n, h, rows, :]
            qI = q_ref[n, rows, pl.ds(h * GLA_DK, GLA_DK)] * (GLA_DK ** -0.5)
            kI = k_ref[n, rows, pl.ds(h * GLA_DK, GLA_DK)]
            vI = v_ref[n, rows, pl.ds(h * GLA_DV, GLA_DV)]
            acc = jnp.zeros((GLA_SUB, GLA_DV), F32)
            for s in range(GLA_SUB):
                e = jnp.exp(jnp.minimum(bI - bI[s:s + 1, :], 0.0))
                a_col = jnp.sum(qI * kI[s:s + 1, :] * e, axis=1, keepdims=True)
                a_col = jnp.where(srow >= s, a_col, 0.0)
                acc += a_col * vI[s:s + 1, :]
            oi_ref[n, h, rows, :] += acc
        return carry

    lax.fori_loop(0, nsub, diag, 0)

    for n, h in pairs:
        vl = pl.ds(h * GLA_DV, GLA_DV)
        g = go_ref[n, :, vl]
        o_ref[n, :, vl] = (_rms(oi_ref[n, h], gn_ref[...]) * (g * jax.nn.sigmoid(g))).astype(o_ref.dtype)

    @pl.when(c == pl.num_programs(0) - 1)
    def _():
        for n, h in pairs:
            st_ref[n, h] = sT_ref[n, h].T


def gla_prompt(z, log_a, gla_norm, B, S, C=128):
    assert S % C == 0 and C % GLA_SUB == 0
    q_blk = 3 * ATTN_WIDTH // GLA_KW
    v_blk = (3 * ATTN_WIDTH + 2 * GLA_KW) // GLA_WIDTH
    z3 = z.reshape(B, S, z.shape[1])
    la3 = log_a.reshape(B, S, GLA_KW)
    o, state = pl.pallas_call(
        functools.partial(_gla_prompt_body, B=B, C=C),
        grid=(S // C,),
        in_specs=[
            pl.BlockSpec((B, C, GLA_KW), lambda c: (0, c, q_blk)),
            pl.BlockSpec((B, C, GLA_KW), lambda c: (0, c, q_blk + 1)),
            pl.BlockSpec((B, C, GLA_WIDTH), lambda c: (0, c, v_blk)),
            pl.BlockSpec((B, C, GLA_WIDTH), lambda c: (0, c, v_blk + 1)),
            pl.BlockSpec((B, C, GLA_KW), lambda c: (0, c, 0)),
            pl.BlockSpec((1, GLA_DV), lambda c: (0, 0)),
        ],
        out_specs=[
            pl.BlockSpec((B, C, GLA_WIDTH), lambda c: (0, c, 0)),
            pl.BlockSpec((B, GLA_HEADS, GLA_DK, GLA_DV), lambda c: (0, 0, 0, 0)),
        ],
        out_shape=[
            jax.ShapeDtypeStruct((B, S, GLA_WIDTH), BF16),
            jax.ShapeDtypeStruct((B, GLA_HEADS, GLA_DK, GLA_DV), F32),
        ],
        scratch_shapes=[
            pltpu.VMEM((B, GLA_HEADS, GLA_DV, GLA_DK), F32),
            pltpu.VMEM((B, GLA_HEADS, C, GLA_DK), F32),
            pltpu.VMEM((B, GLA_HEADS, C, GLA_DV), F32),
        ],
        compiler_params=_cparams("arbitrary"),
        name="gla_prompt",
    )(z3, z3, z3, z3, la3, gla_norm)
    return o.reshape(B * S, GLA_WIDTH), state


def _gla_sample_body(q_ref, k_ref, v_ref, go_ref, la_ref, gn_ref, s_ref, o_ref, so_ref, *, bb):
    for h in range(GLA_HEADS):
        kl = pl.ds(h * GLA_DK, GLA_DK)
        vl = pl.ds(h * GLA_DV, GLA_DV)
        aT = jnp.exp(la_ref[:, kl]).T
        kT = k_ref[:, kl].T
        qT = (q_ref[:, kl] * (GLA_DK ** -0.5)).T
        outs = []
        for b in range(bb):
            s_new = aT[:, b:b + 1] * s_ref[b, h] + kT[:, b:b + 1] * v_ref[b:b + 1, vl]
            so_ref[b, h] = s_new
            outs.append(jnp.sum(qT[:, b:b + 1] * s_new, axis=0, keepdims=True))
        o = jnp.concatenate(outs, axis=0)
        g = go_ref[:, vl]
        o_ref[:, vl] = (_rms(o, gn_ref[...]) * (g * jax.nn.sigmoid(g))).astype(o_ref.dtype)


def gla_sample(z, log_a, gla_norm, state, bb=SUBLANES):
    Bs = z.shape[0]
    assert Bs % bb == 0
    q_blk = 3 * ATTN_WIDTH // GLA_KW
    v_blk = (3 * ATTN_WIDTH + 2 * GLA_KW) // GLA_WIDTH
    st_spec = pl.BlockSpec((bb, GLA_HEADS, GLA_DK, GLA_DV), lambda i: (i, 0, 0, 0))
    return pl.pallas_call(
        functools.partial(_gla_sample_body, bb=bb),
        grid=(Bs // bb,),
        in_specs=[
            pl.BlockSpec((bb, GLA_KW), lambda i: (i, q_blk)),
            pl.BlockSpec((bb, GLA_KW), lambda i: (i, q_blk + 1)),
            pl.BlockSpec((bb, GLA_WIDTH), lambda i: (i, v_blk)),
            pl.BlockSpec((bb, GLA_WIDTH), lambda i: (i, v_blk + 1)),
            pl.BlockSpec((bb, GLA_KW), lambda i: (i, 0)),
            pl.BlockSpec((1, GLA_DV), lambda i: (0, 0)),
            st_spec,
        ],
        out_specs=[pl.BlockSpec((bb, GLA_WIDTH), lambda i: (i, 0)), st_spec],
        out_shape=[
            jax.ShapeDtypeStruct((Bs, GLA_WIDTH), BF16),
            jax.ShapeDtypeStruct(state.shape, F32),
        ],
        compiler_params=_cparams("parallel"),
        name="gla_sample",
    )(z, z, z, z, log_a, gla_norm, state)


def _proj_residual_body(a_ref, b_ref, wa_ref, wb_ref, res_ref, o_ref):
    acc = jnp.dot(a_ref[...], wa_ref[...], preferred_element_type=F32)
    acc += jnp.dot(b_ref[...], wb_ref[...], preferred_element_type=F32)
    o_ref[...] = res_ref[...] + acc


def proj_residual(a, b, wa, wb, res, tm, tn=512):
    T, Ka = a.shape
    Kb = b.shape[1]
    N = wa.shape[1]
    assert T % tm == 0 and N % tn == 0
    return pl.pallas_call(
        _proj_residual_body,
        grid=(T // tm, N // tn),
        in_specs=[
            pl.BlockSpec((tm, Ka), lambda i, j: (i, 0)),
            pl.BlockSpec((tm, Kb), lambda i, j: (i, 0)),
            pl.BlockSpec((Ka, tn), lambda i, j: (0, j)),
            pl.BlockSpec((Kb, tn), lambda i, j: (0, j)),
            pl.BlockSpec((tm, tn), lambda i, j: (i, j)),
        ],
        out_specs=pl.BlockSpec((tm, tn), lambda i, j: (i, j)),
        out_shape=jax.ShapeDtypeStruct((T, N), F32),
        compiler_params=_cparams("parallel", "arbitrary"),
        name="proj_residual",
    )(a, b, wa, wb, res)


def _dot_nt_split(a, b):
    a_hi = a.astype(BF16)
    b_hi = b.astype(BF16)
    a_lo = (a - a_hi.astype(F32)).astype(BF16)
    b_lo = (b - b_hi.astype(F32)).astype(BF16)
    dot = lambda x, y: lax.dot_general(x, y, NT, preferred_element_type=F32)
    return dot(a_hi, b_hi) + (dot(a_hi, b_lo) + dot(a_lo, b_hi))


def _sorting_network(n):
    assert n & (n - 1) == 0
    pairs = []
    p = 1
    while p < n:
        k = p
        while k >= 1:
            for j in range(k % p, n - k, 2 * k):
                for i in range(min(k, n - j - k)):
                    if (i + j) // (2 * p) == (i + j + k) // (2 * p):
                        pairs.append((i + j, i + j + k))
            k //= 2
        p *= 2
    return pairs


def _peer_route_body(h_ref, g_ref, w_ref, k1_ref, k2_ref, xn_ref, s1_ref, s2_ref, thr_ref, q_ref, *, tm):
    xn = _rms(h_ref[...], g_ref[...]).astype(BF16)
    xn_ref[...] = xn
    q_ref[...] = jnp.dot(xn, w_ref[...], preferred_element_type=F32)

    K = PEER_TOPK
    G = -(-(K + 1) // SUBLANES)
    row8 = lax.broadcasted_iota(jnp.int32, (SUBLANES, tm), 0)
    rowh = lax.broadcasted_iota(jnp.int32, (PEER_HEADS, tm), 0)
    thr_all = jnp.zeros((PEER_HEADS, tm), F32)
    neg_inf = jnp.full((SUBLANES, tm), -jnp.inf, F32)

    def sort_groups(groups):
        n = 1 << (len(groups) - 1).bit_length()
        lists = list(groups) + [None] * (n - len(groups))
        for a, b in _sorting_network(n):
            if lists[a] is None:
                lists[a], lists[b] = lists[b], None
            elif lists[b] is not None:
                lists[a], lists[b] = jnp.maximum(lists[a], lists[b]), jnp.minimum(lists[a], lists[b])
        return lists

    def pop(lists, depth):
        mx = jnp.max(lists[0], axis=0, keepdims=True)
        hit = lists[0] == mx
        for r in range(min(len(lists), depth)):
            if lists[r] is None:
                break
            below = lists[r + 1] if r + 1 < len(lists) and lists[r + 1] is not None else neg_inf
            lists[r] = jnp.where(hit, below, lists[r])
        return mx

    def top_values(s):
        lists = sort_groups([s[r * SUBLANES:(r + 1) * SUBLANES] for r in range(PEER_NKEYS // SUBLANES)])
        vals = [neg_inf] * G
        for it in range(K + 1):
            mx = pop(lists, K - it)
            vals[it // SUBLANES] = jnp.where(row8 == it % SUBLANES, mx, vals[it // SUBLANES])
        return vals

    for h in range(PEER_HEADS):
        q1 = q_ref[:, h * PEER_QDIM:h * PEER_QDIM + PEER_HALF]
        q2 = q_ref[:, h * PEER_QDIM + PEER_HALF:(h + 1) * PEER_QDIM]
        s1 = _dot_nt_split(k1_ref[h], q1)
        s2 = _dot_nt_split(k2_ref[h], q2)
        v1 = top_values(s1)
        v2 = top_values(s2)
        top1, top2 = v1[0][0:1], v2[0][0:1]
        cands = [top1 + g for g in v2]
        for a in range(1, SUBLANES):
            cands.append(jnp.where(row8 < (K + 1) // (a + 1), v1[0][a:a + 1] + v2[0], -jnp.inf))
        cands += [g + top2 for g in v1[1:]]
        lists = sort_groups(cands)
        m = top1 + top2
        z = jnp.zeros((1, tm), F32)
        kth = m
        for it in range(K):
            kth = pop(lists, K - it)
            z = z + jnp.exp(kth - m)
        nxt = jnp.max(lists[0], axis=0, keepdims=True)
        c = m + jnp.log(z)
        s1_ref[h] = (s1 - c) * LOG2E - 1.0
        s2_ref[h] = s2 * LOG2E
        thr_all = jnp.where(rowh == h, (0.5 * (kth + nxt) - c) * LOG2E - 1.0, thr_all)
    thr_ref[...] = thr_all


def peer_route(h, g, w, keys1, keys2, tm):
    T, D = h.shape
    N = w.shape[1]
    assert T % tm == 0 and tm % LANES == 0 and (PEER_TOPK + 1) // 2 <= SUBLANES and N == PEER_HEADS * PEER_QDIM
    tile = pl.BlockSpec((PEER_HEADS, PEER_NKEYS, tm), lambda i: (0, 0, i))
    tile_shape = jax.ShapeDtypeStruct((PEER_HEADS, PEER_NKEYS, T), F32)
    kspec = pl.BlockSpec((PEER_HEADS, PEER_NKEYS, PEER_HALF), lambda i: (0, 0, 0))
    return pl.pallas_call(
        functools.partial(_peer_route_body, tm=tm),
        grid=(T // tm,),
        in_specs=[
            pl.BlockSpec((tm, D), lambda i: (i, 0)),
            pl.BlockSpec((1, D), lambda i: (0, 0)),
            pl.BlockSpec((D, N), lambda i: (0, 0)),
            kspec, kspec,
        ],
        out_specs=[pl.BlockSpec((tm, D), lambda i: (i, 0)), tile, tile,
                   pl.BlockSpec((PEER_HEADS, tm), lambda i: (0, i))],
        out_shape=[jax.ShapeDtypeStruct((T, D), BF16), tile_shape, tile_shape,
                   jax.ShapeDtypeStruct((PEER_HEADS, T), F32)],
        scratch_shapes=[pltpu.VMEM((tm, N), F32)],
        compiler_params=_cparams("parallel"),
        name="peer_route",
    )(h, g, w, keys1, keys2)


def _peer_dense_body(xn_ref, u_ref, v_ref, s1_ref, s2_ref, thr_ref, h_ref, g_ref, y_ref, *, te, tm):
    e = pl.program_id(1)

    @pl.when(e == 0)
    def _():
        y_ref[...] = h_ref[...]

    rows = []
    for c in range(te // PEER_NKEYS):
        tiles = []
        for tc in range(tm // LANES):
            lanes = pl.ds(tc * LANES, LANES)
            gate = jnp.zeros((PEER_NKEYS, LANES), F32)
            for h in range(PEER_HEADS):
                ssum = s2_ref[h, :, lanes] + s1_ref[h, c:c + 1, lanes]
                gate += jnp.where(ssum >= thr_ref[pl.ds(h, 1), lanes], jnp.exp2(ssum), 0.0)
            tiles.append(gate)
        rows.append(jnp.concatenate(tiles, axis=1))
    half_gate = jnp.concatenate(rows, axis=0).T
    hx = lax.dot_general(xn_ref[...], u_ref[...], NT, preferred_element_type=F32)
    act = (hx + hx * lax.erf(hx * (2.0 ** -0.5))) * half_gate
    y_ref[...] += jnp.dot(act.astype(BF16), v_ref[...], preferred_element_type=F32)

    @pl.when(e == pl.num_programs(1) - 1)
    def _():
        y_ref[...] = _rms(y_ref[...], g_ref[...])


def peer_dense(xn, u, v, s1, s2, thr, h, g, tm):
    T, D = xn.shape
    E = u.shape[0]
    te = SUBLANES * PEER_NKEYS
    assert T % tm == 0 and E % te == 0 and tm % LANES == 0
    return pl.pallas_call(
        functools.partial(_peer_dense_body, te=te, tm=tm),
        grid=(T // tm, E // te),
        in_specs=[
            pl.BlockSpec((tm, D), lambda i, e: (i, 0)),
            pl.BlockSpec((te, D), lambda i, e: (e, 0)),
            pl.BlockSpec((te, D), lambda i, e: (e, 0)),
            pl.BlockSpec((PEER_HEADS, SUBLANES, tm), lambda i, e: (0, e, i)),
            pl.BlockSpec((PEER_HEADS, PEER_NKEYS, tm), lambda i, e: (0, 0, i)),
            pl.BlockSpec((PEER_HEADS, tm), lambda i, e: (0, i)),
            pl.BlockSpec((tm, D), lambda i, e: (i, 0)),
            pl.BlockSpec((1, D), lambda i, e: (0, 0)),
        ],
        out_specs=pl.BlockSpec((tm, D), lambda i, e: (i, 0)),
        out_shape=jax.ShapeDtypeStruct((T, D), F32),
        compiler_params=_cparams("parallel", "arbitrary"),
        name="peer_dense",
    )(xn, u, v, s1, s2, thr, h, g)


def _post_mix(x, o_attn, o_gla, w, tm, tm_proj):
    h = proj_residual(o_attn, o_gla, w["w_o_a"], w["w_o_b"], x, tm_proj)
    xn, s1, s2, thr = peer_route(h, w["norm_ffn"], w["peer_query"], w["peer_keys_1"], w["peer_keys_2"], tm)
    return peer_dense(xn, w["expert_u"], w["expert_v"], s1, s2, thr, h, w["final_norm"], tm)


def _prompt_group(x_prompt, w):
    B, S, _ = x_prompt.shape
    x = x_prompt.reshape(B * S, D_MODEL)
    tm, tm_proj = 512, 1024
    z, log_a, k_win, v_win = in_proj(x, w, jnp.arange(S), tm_proj, min(WINDOW_MAX, S))
    o_attn = attn_prompt(z, B, S)
    o_gla, state = gla_prompt(z, log_a, w["gla_norm"], B, S)
    out = _post_mix(x, o_attn, o_gla, w, tm, tm_proj)
    return out, k_win, v_win, state


def _sample_group(x_sample, cache_k, cache_v, state, w):
    Bs, Ls, _ = x_sample.shape
    assert Ls == 1
    x = x_sample.reshape(Bs, D_MODEL)
    tm = Bs
    z, log_a, k_new, v_new = in_proj(x, w, jnp.full((tm,), PAST_LEN, jnp.int32), tm, tm)
    o_attn = attn_sample(z, cache_k, cache_v)
    o_gla, new_state = gla_sample(z, log_a, w["gla_norm"], state)
    out = _post_mix(x, o_attn, o_gla, w, tm, tm)
    return out, k_new, v_new, new_state


def _prepare(norm_attn, w_in, w_gate_up, b_gate, gla_norm, w_o, norm_ffn, peer_query_w, peer_keys_1,
             peer_keys_2, expert_u, expert_v, final_norm_w):
    assert w_in.shape[0] == 1
    l = 0
    w_in_l = w_in[l]
    row = lambda v: v.reshape(1, -1).astype(F32)
    return {
        "norm_attn": row(norm_attn[l]),
        "w_in": w_in_l.astype(BF16),
        "w_in_lr": jnp.pad(w_in_l[:, MAIN_COLS:], ((0, 0), (0, LANES - GLA_LOWRANK))).astype(BF16),
        "w_gate_up": jnp.pad(w_gate_up[l], ((0, LANES - GLA_LOWRANK), (0, 0))),
        "b_gate": row(b_gate[l]),
        "gla_norm": row(gla_norm[l]),
        "w_o_a": w_o[l][:ATTN_WIDTH].astype(BF16),
        "w_o_b": w_o[l][ATTN_WIDTH:].astype(BF16),
        "norm_ffn": row(norm_ffn[l]),
        "peer_query": peer_query_w[l].astype(BF16),
        "peer_keys_1": peer_keys_1[l],
        "peer_keys_2": peer_keys_2[l],
        "expert_u": expert_u[l].astype(BF16),
        "expert_v": expert_v[l].astype(BF16),
        "final_norm": row(final_norm_w),
    }


def kernel(x_prompt, x_sample, cache_attn_k, cache_attn_v, state_gla, norm_attn, w_in, w_gate_up,
           b_gate, gla_norm, w_o, norm_ffn, peer_query, peer_keys_1, peer_keys_2, expert_u,
           expert_v, final_norm):
    Bp, Lp, _ = x_prompt.shape
    Bs, Ls, _ = x_sample.shape
    w = _prepare(norm_attn, w_in, w_gate_up, b_gate, gla_norm, w_o, norm_ffn, peer_query, peer_keys_1,
                 peer_keys_2, expert_u, expert_v, final_norm)
    win_p = min(WINDOW_MAX, Lp)
    yp, kp, vp, sp = _prompt_group(x_prompt, w)
    ys, ks, vs, ss = _sample_group(x_sample, cache_attn_k[0], cache_attn_v[0], state_gla[0], w)
    heads = lambda a, B, L: a.reshape(1, B, L, ATTN_HEADS, HEAD_DIM)
    return (yp.reshape(Bp, Lp, D_MODEL), ys.reshape(Bs, Ls, D_MODEL),
            heads(kp, Bp, win_p), heads(vp, Bp, win_p), sp[None],
            heads(ks, Bs, Ls), heads(vs, Bs, Ls), ss[None])


def sample_group(inp):
    names = ("norm_attn", "w_in", "w_gate_up", "b_gate", "gla_norm", "w_o", "norm_ffn", "peer_query",
             "peer_keys_1", "peer_keys_2", "expert_u", "expert_v", "final_norm")
    w = _prepare(*[inp[n] for n in names])
    out, k, v, st = _sample_group(inp["x_sample"], inp["cache_attn_k"][0], inp["cache_attn_v"][0],
                                  inp["state_gla"][0], w)
    Bs = out.shape[0]
    return (out, k.reshape(Bs, 1, ATTN_HEADS, HEAD_DIM), v.reshape(Bs, 1, ATTN_HEADS, HEAD_DIM), st)
```

```python
import functools
import math

import jax
import jax.numpy as jnp
import numpy as np
from jax import lax
from jax.experimental import pallas as pl
from jax.experimental.pallas import tpu as pltpu

F32 = jnp.float32
BF16 = jnp.bfloat16

D_MODEL = 2048
PAST_LEN = 2048
HEAD_DIM = 128
ATTN_HEADS = 8
ATTN_WIDTH = ATTN_HEADS * HEAD_DIM
DILATIONS = ((128, 1), (512, 4), (2048, 16))
WINDOW_MAX = 2048
QUERY_BLOCK = 128
ATTN_SCALE = HEAD_DIM ** -0.5
ROPE_THETA = 500000.0
ROT_DIM = HEAD_DIM // 4
ROT_HALF = ROT_DIM // 2
GLA_HEADS = 4
GLA_WIDTH = D_MODEL - ATTN_WIDTH
GLA_DV = GLA_WIDTH // GLA_HEADS
GLA_DK = GLA_DV // 2
GLA_KW = GLA_HEADS * GLA_DK
GLA_LOWRANK = 16
GLA_TAU = 16.0
GLA_SUB = 16
PEER_HEADS = 8
PEER_NKEYS = 128
PEER_EXPERTS = PEER_NKEYS * PEER_NKEYS
PEER_QDIM = 256
PEER_HALF = PEER_QDIM // 2
PEER_TOPK = 16
NORM_EPS = 1e-6
MAIN_COLS = 3 * ATTN_WIDTH + 2 * GLA_KW + 2 * GLA_WIDTH

LANES = 128
SUBLANES = 8
VMEM_LIMIT = 56 * 1024 * 1024
NEG = -1e30
LOG2E = 1.0 / math.log(2.0)
HIGHEST = lax.Precision.HIGHEST
NT = (((1,), (1,)), ((), ()))
TN = (((0,), (0,)), ((), ()))
NN = (((1,), (0,)), ((), ()))


def _cparams(*sem):
    return pltpu.CompilerParams(dimension_semantics=sem, vmem_limit_bytes=VMEM_LIMIT)


def _rms(x, g):
    return x * lax.rsqrt(jnp.mean(x * x, axis=-1, keepdims=True) + NORM_EPS) * g


def _in_proj_body(x_ref, g_ref, w_ref, wlr_ref, wup_ref, bg_ref, cos_ref, sin_ref,
                  z_ref, la_ref, kwin_ref, vwin_ref, xn_ref, *, tn, nb, wb):
    i = pl.program_id(0)
    j = pl.program_id(1)
    in_win = i % nb >= nb - wb
    kv_tiles = ATTN_WIDTH // tn

    @pl.when(j == 0)
    def _():
        xn = _rms(x_ref[...], g_ref[...]).astype(BF16)
        xn_ref[...] = xn
        lr = jnp.dot(xn, wlr_ref[...], preferred_element_type=F32)
        zg = _dot_split(lr, wup_ref[...], NN) + bg_ref[...]
        la_ref[...] = (jnp.minimum(zg, 0.0) - jnp.log(1.0 + jnp.exp(-jnp.abs(zg)))) * (1.0 / GLA_TAU)

    acc = jnp.dot(xn_ref[...], w_ref[...], preferred_element_type=F32)

    @pl.when(j < 2 * ATTN_WIDTH // tn)
    def _():
        lane = lax.broadcasted_iota(jnp.int32, (acc.shape[0], HEAD_DIM), 1)
        cos = cos_ref[...]
        sin = sin_ref[...]
        for c in range(tn // HEAD_DIM):
            a = acc[:, c * HEAD_DIM:(c + 1) * HEAD_DIM]
            partner = jnp.where(lane < ROT_HALF, pltpu.roll(a, HEAD_DIM - ROT_HALF, 1),
                                pltpu.roll(a, ROT_HALF, 1))
            z_ref[:, c * HEAD_DIM:(c + 1) * HEAD_DIM] = a * cos + partner * sin

        @pl.when(jnp.logical_and(in_win, j >= kv_tiles))
        def _():
            kwin_ref[...] = z_ref[...]

    @pl.when(j >= 2 * ATTN_WIDTH // tn)
    def _():
        z_ref[...] = acc

        @pl.when(jnp.logical_and(in_win, j < 3 * kv_tiles))
        def _():
            vwin_ref[...] = acc


def _rotary_tables(pos):
    inv = jnp.exp(-math.log(ROPE_THETA) * jnp.arange(ROT_HALF, dtype=F32) * (2.0 / ROT_DIM))
    ang = pos.astype(F32)[:, None] * inv[None, :]
    n = pos.shape[0]
    cos = jnp.concatenate([jnp.cos(ang), jnp.cos(ang), jnp.ones((n, HEAD_DIM - ROT_DIM), F32)], axis=1)
    sin = jnp.concatenate([-jnp.sin(ang), jnp.sin(ang), jnp.zeros((n, HEAD_DIM - ROT_DIM), F32)], axis=1)
    return cos, sin


def in_proj(x, w, pos, tm, win, tn=512):
    T, D = x.shape
    P = pos.shape[0]
    assert T % tm == 0 and MAIN_COLS % tn == 0 and P % tm == 0 and ATTN_WIDTH % tn == 0 and win % tm == 0
    nb, wb = P // tm, win // tm
    kv_tiles = ATTN_WIDTH // tn
    cos, sin = _rotary_tables(pos)
    const = lambda i, j: (0, 0)

    def win_spec(first_tile):
        def index(i, j):
            ib = i % nb
            row = (i // nb) * wb + jnp.maximum(ib - (nb - wb), 0)
            col = jnp.where(ib >= nb - wb, jnp.clip(j - first_tile, 0, kv_tiles - 1), 0)
            return row, col
        return pl.BlockSpec((tm, tn), index)

    win_shape = jax.ShapeDtypeStruct((T // P * win, ATTN_WIDTH), F32)
    return pl.pallas_call(
        functools.partial(_in_proj_body, tn=tn, nb=nb, wb=wb),
        grid=(T // tm, MAIN_COLS // tn),
        in_specs=[
            pl.BlockSpec((tm, D), lambda i, j: (i, 0)),
            pl.BlockSpec((1, D), const),
            pl.BlockSpec((D, tn), lambda i, j: (0, j)),
            pl.BlockSpec((D, LANES), const),
            pl.BlockSpec((LANES, GLA_KW), const),
            pl.BlockSpec((1, GLA_KW), const),
            pl.BlockSpec((tm, HEAD_DIM), lambda i, j: (i % (P // tm), 0)),
            pl.BlockSpec((tm, HEAD_DIM), lambda i, j: (i % (P // tm), 0)),
        ],
        out_specs=[
            pl.BlockSpec((tm, tn), lambda i, j: (i, j)),
            pl.BlockSpec((tm, GLA_KW), lambda i, j: (i, 0)),
            win_spec(kv_tiles),
            win_spec(2 * kv_tiles),
        ],
        out_shape=[
            jax.ShapeDtypeStruct((T, MAIN_COLS), F32),
            jax.ShapeDtypeStruct((T, GLA_KW), F32),
            win_shape,
            win_shape,
        ],
        scratch_shapes=[pltpu.VMEM((tm, D), BF16)],
        compiler_params=_cparams("arbitrary", "arbitrary"),
        name="in_proj",
    )(x, w["norm_attn"], w["w_in"], w["w_in_lr"], w["w_gate_up"], w["b_gate"], cos, sin)


def _attn_prompt_body(q_ref, k_ref, v_ref, o_ref, ob_ref, lse_ref, s_ref, p_ref, m_ref, *, S):
    QB = QUERY_BLOCK
    row = lax.broadcasted_iota(jnp.int32, (QB, QB), 0)
    col = lax.broadcasted_iota(jnp.int32, (QB, QB), 1)
    bias = jnp.concatenate([jnp.where(col <= row, 0.0, NEG), jnp.where(col >= row, 0.0, NEG)], axis=1)
    lane2 = lax.broadcasted_iota(jnp.int32, (QB, 2 * QB), 1)
    ones = jnp.ones((QB, HEAD_DIM), BF16)

    for bi, (window, d) in enumerate(DILATIONS):
        assert window == d * QB
        nblk = S // d // QB
        assert d * nblk == s_ref.shape[0]

        def windows(idx, d=d, nblk=nblk):
            r = idx // nblk
            i = idx % nblk
            start = r + i * (d * QB)
            if d == 1:
                return i, pl.ds(pl.multiple_of(start, QB), QB)
            return i, pl.ds(start, QB, stride=d)

        def scores(idx, k_prev):
            i, rows = windows(idx)
            qs = (q_ref[rows, :] * ATTN_SCALE).astype(BF16)
            k_cur = k_ref[rows, :].astype(BF16)
            kk = jnp.concatenate([k_cur, k_prev], axis=0)
            s = lax.dot_general(qs, kk, NT, preferred_element_type=F32) + bias
            s_ref[idx] = jnp.where(jnp.logical_or(lane2 < QB, i > 0), s, NEG)
            return k_cur

        def softmax(idx, carry):
            s = s_ref[idx]
            m = jnp.max(jnp.maximum(s[:, :QB], s[:, QB:]), axis=1, keepdims=True)
            p_ref[idx] = jnp.exp(s - m).astype(BF16)
            m_ref[idx] = jnp.broadcast_to(m, (QB, HEAD_DIM))
            return carry

        def values(idx, v_prev, bi=bi):
            i, rows = windows(idx)
            v_cur = jnp.concatenate([v_ref[rows, :].astype(BF16), ones], axis=1)
            vv = jnp.concatenate([v_cur, v_prev], axis=0)
            o = jnp.dot(p_ref[idx], vv, preferred_element_type=F32)
            den = o[:, HEAD_DIM:]
            ob_ref[bi, rows, :] = o[:, :HEAD_DIM] / den
            lse_ref[bi, rows, :] = m_ref[idx] + jnp.log(den)
            return v_cur

        lax.fori_loop(0, d * nblk, scores, jnp.zeros((QB, HEAD_DIM), BF16), unroll=4)
        lax.fori_loop(0, d * nblk, softmax, 0, unroll=4)
        lax.fori_loop(0, d * nblk, values, jnp.zeros((QB, 2 * HEAD_DIM), BF16), unroll=4)

    CH = 256

    def combine(c, carry):
        rows = pl.ds(pl.multiple_of(c * CH, CH), CH)
        l0, l1, l2 = lse_ref[0, rows, :], lse_ref[1, rows, :], lse_ref[2, rows, :]
        mx = jnp.maximum(jnp.maximum(l0, l1), l2)
        w0, w1, w2 = jnp.exp(l0 - mx), jnp.exp(l1 - mx), jnp.exp(l2 - mx)
        num = w0 * ob_ref[0, rows, :] + w1 * ob_ref[1, rows, :] + w2 * ob_ref[2, rows, :]
        o_ref[rows, :] = (num / (w0 + w1 + w2)).astype(o_ref.dtype)
        return carry

    lax.fori_loop(0, S // CH, combine, 0)


def attn_prompt(z, B, S):
    H = ATTN_HEADS
    assert S % (DILATIONS[-1][1] * QUERY_BLOCK) == 0
    return pl.pallas_call(
        functools.partial(_attn_prompt_body, S=S),
        grid=(B, H),
        in_specs=[
            pl.BlockSpec((S, HEAD_DIM), lambda b, h: (b, h)),
            pl.BlockSpec((S, HEAD_DIM), lambda b, h: (b, H + h)),
            pl.BlockSpec((S, HEAD_DIM), lambda b, h: (b, 2 * H + h)),
        ],
        out_specs=pl.BlockSpec((S, HEAD_DIM), lambda b, h: (b, h)),
        out_shape=jax.ShapeDtypeStruct((B * S, ATTN_WIDTH), BF16),
        scratch_shapes=[
            pltpu.VMEM((len(DILATIONS), S, HEAD_DIM), F32),
            pltpu.VMEM((len(DILATIONS), S, HEAD_DIM), F32),
            pltpu.VMEM((S // QUERY_BLOCK, QUERY_BLOCK, 2 * QUERY_BLOCK), F32),
            pltpu.VMEM((S // QUERY_BLOCK, QUERY_BLOCK, 2 * QUERY_BLOCK), BF16),
            pltpu.VMEM((S // QUERY_BLOCK, QUERY_BLOCK, HEAD_DIM), F32),
        ],
        compiler_params=_cparams("parallel", "parallel"),
        name="attn_prompt",
    )(z, z, z)


def _attn_sample_body(z_ref, k1_ref, k4_ref, k16_ref, v1_ref, v4_ref, v16_ref, o_ref, *, bb):
    H = ATTN_HEADS
    for b in range(bb):
        q = z_ref[b, 0:H, :] * ATTN_SCALE
        k_new = z_ref[b, H:2 * H, :]
        v_new = z_ref[b, 2 * H:3 * H, :]
        s_new = jnp.sum(k_new * q, axis=-1, keepdims=True)
        s_win = [jnp.sum(kr[b] * q, axis=-1, keepdims=True)
                 for kr in (k1_ref, k4_ref, k16_ref)]
        m = s_new
        for s in s_win:
            m = jnp.maximum(m, jnp.max(s, axis=0))
        p_new = jnp.exp(s_new - m) * float(len(DILATIONS))
        den = p_new
        acc = p_new * v_new
        for s, vr in zip(s_win, (v1_ref, v4_ref, v16_ref)):
            p = jnp.exp(s - m)
            den = den + jnp.sum(p, axis=0)
            acc = acc + jnp.sum(p * vr[b], axis=0)
        o_ref[b] = acc / den


def attn_sample(z, cache_k, cache_v, bb=4):
    Bs = z.shape[0]
    wbuf = cache_k.shape[1]
    QB, H = QUERY_BLOCK, ATTN_HEADS
    assert wbuf == PAST_LEN == WINDOW_MAX and Bs % bb == 0

    def views(c):
        out, specs = [], []
        for window, d in DILATIONS:
            n = wbuf // d
            if d == 1:
                out.append(c)
                specs.append(pl.BlockSpec((bb, QB, H, HEAD_DIM), lambda i, n=n: (i, n // QB - 1, 0, 0)))
            else:
                out.append(c.reshape(Bs, n, d, H, HEAD_DIM))
                specs.append(pl.BlockSpec((bb, QB, None, H, HEAD_DIM),
                                          lambda i, n=n: (i, n // QB - 1, 0, 0, 0)))
        return out, specs

    kv, kspecs = views(cache_k)
    vv, vspecs = views(cache_v)
    z3 = z.reshape(Bs, MAIN_COLS // HEAD_DIM, HEAD_DIM)
    out = pl.pallas_call(
        functools.partial(_attn_sample_body, bb=bb),
        grid=(Bs // bb,),
        in_specs=[pl.BlockSpec((bb, MAIN_COLS // HEAD_DIM, HEAD_DIM), lambda i: (i, 0, 0))] + kspecs + vspecs,
        out_specs=pl.BlockSpec((bb, H, HEAD_DIM), lambda i: (i, 0, 0)),
        out_shape=jax.ShapeDtypeStruct((Bs, H, HEAD_DIM), F32),
        compiler_params=_cparams("parallel"),
        name="attn_sample",
    )(z3, *kv, *vv)
    return out.reshape(Bs, ATTN_WIDTH).astype(BF16)


def _gla_prompt_body(q_ref, k_ref, v_ref, go_ref, la_ref, gn_ref, o_ref, st_ref, sT_ref, b_ref, oi_ref,
                     *, B, C):
    c = pl.program_id(0)
    nsub = C // GLA_SUB

    @pl.when(c == 0)
    def _():
        sT_ref[...] = jnp.zeros_like(sT_ref)

    tri = (lax.broadcasted_iota(jnp.int32, (C, C), 1)
           <= lax.broadcasted_iota(jnp.int32, (C, C), 0)).astype(F32)
    srow = lax.broadcasted_iota(jnp.int32, (GLA_SUB, 1), 0)
    pairs = [(n, h) for n in range(B) for h in range(GLA_HEADS)]

    for n, h in pairs:
        kl = pl.ds(h * GLA_DK, GLA_DK)
        vl = pl.ds(h * GLA_DV, GLA_DV)
        b = jnp.dot(tri, la_ref[n, :, kl], precision=HIGHEST, preferred_element_type=F32)
        b_ref[n, h] = b
        qh = q_ref[n, :, kl] * (GLA_DK ** -0.5)
        kh = k_ref[n, :, kl]
        vh = v_ref[n, :, vl]
        b_end = b[C - 1:C, :]
        sT = sT_ref[n, h]
        oi_ref[n, h] = lax.dot_general((qh * jnp.exp(b)).astype(BF16), sT.astype(BF16), NT,
                                       preferred_element_type=F32)
        k_end = (kh * jnp.exp(b_end - b)).astype(BF16)
        sT_ref[n, h] = sT * jnp.exp(b_end) + lax.dot_general(vh.astype(BF16), k_end, TN,
                                                             preferred_element_type=F32)
        for I in range(1, nsub):
            r0 = I * GLA_SUB
            cI = b[r0 - 1:r0, :]
            qI = (qh[r0:r0 + GLA_SUB] * jnp.exp(b[r0:r0 + GLA_SUB] - cI)).astype(BF16)
            kI = (kh[:r0] * jnp.exp(cI - b[:r0])).astype(BF16)
            a = lax.dot_general(qI, kI, NT, preferred_element_type=F32)
            oi_ref[n, h, r0:r0 + GLA_SUB, :] += jnp.dot(a.astype(BF16), vh[:r0].astype(BF16),
                                                        preferred_element_type=F32)

    def diag(I, carry):
        rows = pl.ds(pl.multiple_of(I * GLA_SUB, GLA_SUB), GLA_SUB)
        for n, h in pairs:
            bI = b_ref[n, h, rows, :]
            qI = q_ref[n, rows, pl.ds(h * GLA_DK, GLA_DK)] * (GLA_DK ** -0.5)
            kI = k_ref[n, rows, pl.ds(h * GLA_DK, GLA_DK)]
            vI = v_ref[n, rows, pl.ds(h * GLA_DV, GLA_DV)]
            acc = jnp.zeros((GLA_SUB, GLA_DV), F32)
            for s in range(GLA_SUB):
                e = jnp.exp(jnp.minimum(bI - bI[s:s + 1, :], 0.0))
                a_col = jnp.sum(qI * kI[s:s + 1, :] * e, axis=1, keepdims=True)
                a_col = jnp.where(srow >= s, a_col, 0.0)
                acc += a_col * vI[s:s + 1, :]
            oi_ref[n, h, rows, :] += acc
        return carry

    lax.fori_loop(0, nsub, diag, 0)

    for n, h in pairs:
        vl = pl.ds(h * GLA_DV, GLA_DV)
        g = go_ref[n, :, vl]
        o_ref[n, :, vl] = (_rms(oi_ref[n, h], gn_ref[...]) * (g * jax.nn.sigmoid(g))).astype(o_ref.dtype)

    @pl.when(c == pl.num_programs(0) - 1)
    def _():
        for n, h in pairs:
            st_ref[n, h] = sT_ref[n, h].T


def gla_prompt(z, log_a, gla_norm, B, S, C=128):
    assert S % C == 0 and C % GLA_SUB == 0
    q_blk = 3 * ATTN_WIDTH // GLA_KW
    v_blk = (3 * ATTN_WIDTH + 2 * GLA_KW) // GLA_WIDTH
    z3 = z.reshape(B, S, z.shape[1])
    la3 = log_a.reshape(B, S, GLA_KW)
    o, state = pl.pallas_call(
        functools.partial(_gla_prompt_body, B=B, C=C),
        grid=(S // C,),
        in_specs=[
            pl.BlockSpec((B, C, GLA_KW), lambda c: (0, c, q_blk)),
            pl.BlockSpec((B, C, GLA_KW), lambda c: (0, c, q_blk + 1)),
            pl.BlockSpec((B, C, GLA_WIDTH), lambda c: (0, c, v_blk)),
            pl.BlockSpec((B, C, GLA_WIDTH), lambda c: (0, c, v_blk + 1)),
            pl.BlockSpec((B, C, GLA_KW), lambda c: (0, c, 0)),
            pl.BlockSpec((1, GLA_DV), lambda c: (0, 0)),
        ],
        out_specs=[
            pl.BlockSpec((B, C, GLA_WIDTH), lambda c: (0, c, 0)),
            pl.BlockSpec((B, GLA_HEADS, GLA_DK, GLA_DV), lambda c: (0, 0, 0, 0)),
        ],
        out_shape=[
            jax.ShapeDtypeStruct((B, S, GLA_WIDTH), BF16),
            jax.ShapeDtypeStruct((B, GLA_HEADS, GLA_DK, GLA_DV), F32),
        ],
        scratch_shapes=[
            pltpu.VMEM((B, GLA_HEADS, GLA_DV, GLA_DK), F32),
            pltpu.VMEM((B, GLA_HEADS, C, GLA_DK), F32),
            pltpu.VMEM((B, GLA_HEADS, C, GLA_DV), F32),
        ],
        compiler_params=_cparams("arbitrary"),
        name="gla_prompt",
    )(z3, z3, z3, z3, la3, gla_norm)
    return o.reshape(B * S, GLA_WIDTH), state


def _gla_sample_body(q_ref, k_ref, v_ref, go_ref, la_ref, gn_ref, s_ref, o_ref, so_ref, *, bb):
    for h in range(GLA_HEADS):
        kl = pl.ds(h * GLA_DK, GLA_DK)
        vl = pl.ds(h * GLA_DV, GLA_DV)
        aT = jnp.exp(la_ref[:, kl]).T
        kT = k_ref[:, kl].T
        qT = (q_ref[:, kl] * (GLA_DK ** -0.5)).T
        outs = []
        for b in range(bb):
            s_new = aT[:, b:b + 1] * s_ref[b, h] + kT[:, b:b + 1] * v_ref[b:b + 1, vl]
            so_ref[b, h] = s_new
            outs.append(jnp.sum(qT[:, b:b + 1] * s_new, axis=0, keepdims=True))
        o = jnp.concatenate(outs, axis=0)
        g = go_ref[:, vl]
        o_ref[:, vl] = (_rms(o, gn_ref[...]) * (g * jax.nn.sigmoid(g))).astype(o_ref.dtype)


def gla_sample(z, log_a, gla_norm, state, bb=SUBLANES):
    Bs = z.shape[0]
    assert Bs % bb == 0
    q_blk = 3 * ATTN_WIDTH // GLA_KW
    v_blk = (3 * ATTN_WIDTH + 2 * GLA_KW) // GLA_WIDTH
    st_spec = pl.BlockSpec((bb, GLA_HEADS, GLA_DK, GLA_DV), lambda i: (i, 0, 0, 0))
    return pl.pallas_call(
        functools.partial(_gla_sample_body, bb=bb),
        grid=(Bs // bb,),
        in_specs=[
            pl.BlockSpec((bb, GLA_KW), lambda i: (i, q_blk)),
            pl.BlockSpec((bb, GLA_KW), lambda i: (i, q_blk + 1)),
            pl.BlockSpec((bb, GLA_WIDTH), lambda i: (i, v_blk)),
            pl.BlockSpec((bb, GLA_WIDTH), lambda i: (i, v_blk + 1)),
            pl.BlockSpec((bb, GLA_KW), lambda i: (i, 0)),
            pl.BlockSpec((1, GLA_DV), lambda i: (0, 0)),
            st_spec,
        ],
        out_specs=[pl.BlockSpec((bb, GLA_WIDTH), lambda i: (i, 0)), st_spec],
        out_shape=[
            jax.ShapeDtypeStruct((Bs, GLA_WIDTH), BF16),
            jax.ShapeDtypeStruct(state.shape, F32),
        ],
        compiler_params=_cparams("parallel"),
        name="gla_sample",
    )(z, z, z, z, log_a, gla_norm, state)


def _proj_residual_body(a_ref, b_ref, wa_ref, wb_ref, res_ref, o_ref):
    acc = jnp.dot(a_ref[...], wa_ref[...], preferred_element_type=F32)
    acc += jnp.dot(b_ref[...], wb_ref[...], preferred_element_type=F32)
    o_ref[...] = res_ref[...] + acc


def proj_residual(a, b, wa, wb, res, tm, tn=1024):
    T, Ka = a.shape
    Kb = b.shape[1]
    N = wa.shape[1]
    assert T % tm == 0 and N % tn == 0
    return pl.pallas_call(
        _proj_residual_body,
        grid=(T // tm, N // tn),
        in_specs=[
            pl.BlockSpec((tm, Ka), lambda i, j: (i, 0)),
            pl.BlockSpec((tm, Kb), lambda i, j: (i, 0)),
            pl.BlockSpec((Ka, tn), lambda i, j: (0, j)),
            pl.BlockSpec((Kb, tn), lambda i, j: (0, j)),
            pl.BlockSpec((tm, tn), lambda i, j: (i, j)),
        ],
        out_specs=pl.BlockSpec((tm, tn), lambda i, j: (i, j)),
        out_shape=jax.ShapeDtypeStruct((T, N), F32),
        compiler_params=_cparams("parallel", "arbitrary"),
        name="proj_residual",
    )(a, b, wa, wb, res)


def _dot_split(a, b, dims):
    a_hi = a.astype(BF16)
    b_hi = b.astype(BF16)
    a_lo = (a - a_hi.astype(F32)).astype(BF16)
    b_lo = (b - b_hi.astype(F32)).astype(BF16)
    dot = lambda x, y: lax.dot_general(x, y, dims, preferred_element_type=F32)
    return dot(a_hi, b_hi) + (dot(a_hi, b_lo) + dot(a_lo, b_hi))


def _sorting_network(n):
    assert n & (n - 1) == 0
    pairs = []
    p = 1
    while p < n:
        k = p
        while k >= 1:
            for j in range(k % p, n - k, 2 * k):
                for i in range(min(k, n - j - k)):
                    if (i + j) // (2 * p) == (i + j + k) // (2 * p):
                        pairs.append((i + j, i + j + k))
            k //= 2
        p *= 2
    return pairs


def _peer_route_body(h_ref, g_ref, w_ref, k1_ref, k2_ref, xn_ref, s1_ref, s2_ref, thr_ref, q_ref, *, tm):
    xn = _rms(h_ref[...], g_ref[...]).astype(BF16)
    xn_ref[...] = xn
    q_ref[...] = jnp.dot(xn, w_ref[...], preferred_element_type=F32)

    K = PEER_TOPK
    G = -(-(K + 1) // SUBLANES)
    row8 = lax.broadcasted_iota(jnp.int32, (SUBLANES, tm), 0)
    rowh = lax.broadcasted_iota(jnp.int32, (PEER_HEADS, tm), 0)
    thr_all = jnp.zeros((PEER_HEADS, tm), F32)
    neg_inf = jnp.full((SUBLANES, tm), -jnp.inf, F32)

    def sort_groups(groups):
        n = 1 << (len(groups) - 1).bit_length()
        lists = list(groups) + [None] * (n - len(groups))
        for a, b in _sorting_network(n):
            if lists[a] is None:
                lists[a], lists[b] = lists[b], None
            elif lists[b] is not None:
                lists[a], lists[b] = jnp.maximum(lists[a], lists[b]), jnp.minimum(lists[a], lists[b])
        return lists

    def pop(lists, depth):
        mx = jnp.max(lists[0], axis=0, keepdims=True)
        hit = lists[0] == mx
        for r in range(min(len(lists), depth)):
            if lists[r] is None:
                break
            below = lists[r + 1] if r + 1 < len(lists) and lists[r + 1] is not None else neg_inf
            lists[r] = jnp.where(hit, below, lists[r])
        return mx

    def top_values(s):
        lists = sort_groups([s[r * SUBLANES:(r + 1) * SUBLANES] for r in range(PEER_NKEYS // SUBLANES)])
        vals = [neg_inf] * G
        for it in range(K + 1):
            mx = pop(lists, K - it)
            vals[it // SUBLANES] = jnp.where(row8 == it % SUBLANES, mx, vals[it // SUBLANES])
        return vals

    for h in range(PEER_HEADS):
        q1 = q_ref[:, h * PEER_QDIM:h * PEER_QDIM + PEER_HALF]
        q2 = q_ref[:, h * PEER_QDIM + PEER_HALF:(h + 1) * PEER_QDIM]
        s1 = _dot_split(k1_ref[h], q1, NT)
        s2 = _dot_split(k2_ref[h], q2, NT)
        v1 = top_values(s1)
        v2 = top_values(s2)
        top1, top2 = v1[0][0:1], v2[0][0:1]
        cands = [top1 + g for g in v2]
        for a in range(1, SUBLANES):
            cands.append(jnp.where(row8 < (K + 1) // (a + 1), v1[0][a:a + 1] + v2[0], -jnp.inf))
        cands += [g + top2 for g in v1[1:]]
        lists = sort_groups(cands)
        m = top1 + top2
        z = jnp.zeros((1, tm), F32)
        kth = m
        for it in range(K):
            kth = pop(lists, K - it)
            z = z + jnp.exp(kth - m)
        nxt = jnp.max(lists[0], axis=0, keepdims=True)
        c = m + jnp.log(z)
        s1_ref[h] = (s1 - c) * LOG2E - 1.0
        s2_ref[h] = s2 * LOG2E
        thr_all = jnp.where(rowh == h, (0.5 * (kth + nxt) - c) * LOG2E - 1.0, thr_all)
    thr_ref[...] = thr_all


def peer_route(h, g, w, keys1, keys2, tm):
    T, D = h.shape
    N = w.shape[1]
    assert T % tm == 0 and tm % LANES == 0 and (PEER_TOPK + 1) // 2 <= SUBLANES and N == PEER_HEADS * PEER_QDIM
    tile = pl.BlockSpec((PEER_HEADS, PEER_NKEYS, tm), lambda i: (0, 0, i))
    tile_shape = jax.ShapeDtypeStruct((PEER_HEADS, PEER_NKEYS, T), F32)
    kspec = pl.BlockSpec((PEER_HEADS, PEER_NKEYS, PEER_HALF), lambda i: (0, 0, 0))
    return pl.pallas_call(
        functools.partial(_peer_route_body, tm=tm),
        grid=(T // tm,),
        in_specs=[
            pl.BlockSpec((tm, D), lambda i: (i, 0)),
            pl.BlockSpec((1, D), lambda i: (0, 0)),
            pl.BlockSpec((D, N), lambda i: (0, 0)),
            kspec, kspec,
        ],
        out_specs=[pl.BlockSpec((tm, D), lambda i: (i, 0)), tile, tile,
                   pl.BlockSpec((PEER_HEADS, tm), lambda i: (0, i))],
        out_shape=[jax.ShapeDtypeStruct((T, D), BF16), tile_shape, tile_shape,
                   jax.ShapeDtypeStruct((PEER_HEADS, T), F32)],
        scratch_shapes=[pltpu.VMEM((tm, N), F32)],
        compiler_params=_cparams("parallel"),
        name="peer_route",
    )(h, g, w, keys1, keys2)


def _peer_dense_body(xn_ref, u_ref, v_ref, s1_ref, s2_ref, thr_ref, h_ref, g_ref, y_ref, *, te, tm):
    e = pl.program_id(1)

    @pl.when(e == 0)
    def _():
        y_ref[...] = h_ref[...]

    rows = []
    for c in range(te // PEER_NKEYS):
        tiles = []
        for tc in range(tm // LANES):
            lanes = pl.ds(tc * LANES, LANES)
            gate = jnp.zeros((PEER_NKEYS, LANES), F32)
            for h in range(PEER_HEADS):
                ssum = s2_ref[h, :, lanes] + s1_ref[h, c:c + 1, lanes]
                gate += jnp.where(ssum >= thr_ref[pl.ds(h, 1), lanes], jnp.exp2(ssum), 0.0)
            tiles.append(gate)
        rows.append(jnp.concatenate(tiles, axis=1))
    half_gate = jnp.concatenate(rows, axis=0).T
    hx = lax.dot_general(xn_ref[...], u_ref[...], NT, preferred_element_type=F32)
    act = (hx + hx * lax.erf(hx * (2.0 ** -0.5))) * half_gate
    y_ref[...] += jnp.dot(act.astype(BF16), v_ref[...], preferred_element_type=F32)

    @pl.when(e == pl.num_programs(1) - 1)
    def _():
        y_ref[...] = _rms(y_ref[...], g_ref[...])


def peer_dense(xn, u, v, s1, s2, thr, h, g, tm):
    T, D = xn.shape
    E = u.shape[0]
    te = SUBLANES * PEER_NKEYS
    assert T % tm == 0 and E % te == 0 and tm % LANES == 0
    return pl.pallas_call(
        functools.partial(_peer_dense_body, te=te, tm=tm),
        grid=(T // tm, E // te),
        in_specs=[
            pl.BlockSpec((tm, D), lambda i, e: (i, 0)),
            pl.BlockSpec((te, D), lambda i, e: (e, 0)),
            pl.BlockSpec((te, D), lambda i, e: (e, 0)),
            pl.BlockSpec((PEER_HEADS, SUBLANES, tm), lambda i, e: (0, e, i)),
            pl.BlockSpec((PEER_HEADS, PEER_NKEYS, tm), lambda i, e: (0, 0, i)),
            pl.BlockSpec((PEER_HEADS, tm), lambda i, e: (0, i)),
            pl.BlockSpec((tm, D), lambda i, e: (i, 0)),
            pl.BlockSpec((1, D), lambda i, e: (0, 0)),
        ],
        out_specs=pl.BlockSpec((tm, D), lambda i, e: (i, 0)),
        out_shape=jax.ShapeDtypeStruct((T, D), F32),
        compiler_params=_cparams("parallel", "arbitrary"),
        name="peer_dense",
    )(xn, u, v, s1, s2, thr, h, g)


def _post_mix(x, o_attn, o_gla, w, tm, tm_proj):
    h = proj_residual(o_attn, o_gla, w["w_o_a"], w["w_o_b"], x, tm_proj)
    xn, s1, s2, thr = peer_route(h, w["norm_ffn"], w["peer_query"], w["peer_keys_1"], w["peer_keys_2"], tm)
    return peer_dense(xn, w["expert_u"], w["expert_v"], s1, s2, thr, h, w["final_norm"], tm)


def _prompt_group(x_prompt, w):
    B, S, _ = x_prompt.shape
    x = x_prompt.reshape(B * S, D_MODEL)
    tm, tm_proj = 512, 1024
    z, log_a, k_win, v_win = in_proj(x, w, jnp.arange(S), tm_proj, min(WINDOW_MAX, S))
    o_attn = attn_prompt(z, B, S)
    o_gla, state = gla_prompt(z, log_a, w["gla_norm"], B, S)
    out = _post_mix(x, o_attn, o_gla, w, tm, tm_proj)
    return out, k_win, v_win, state


def _sample_group(x_sample, cache_k, cache_v, state, w):
    Bs, Ls, _ = x_sample.shape
    assert Ls == 1
    x = x_sample.reshape(Bs, D_MODEL)
    tm = Bs
    z, log_a, k_new, v_new = in_proj(x, w, jnp.full((tm,), PAST_LEN, jnp.int32), tm, tm)
    o_attn = attn_sample(z, cache_k, cache_v)
    o_gla, new_state = gla_sample(z, log_a, w["gla_norm"], state)
    out = _post_mix(x, o_attn, o_gla, w, tm, tm)
    return out, k_new, v_new, new_state


def _prepare(norm_attn, w_in, w_gate_up, b_gate, gla_norm, w_o, norm_ffn, peer_query_w, peer_keys_1,
             peer_keys_2, expert_u, expert_v, final_norm_w):
    assert w_in.shape[0] == 1
    l = 0
    w_in_l = w_in[l]
    row = lambda v: v.reshape(1, -1).astype(F32)
    return {
        "norm_attn": row(norm_attn[l]),
        "w_in": w_in_l.astype(BF16),
        "w_in_lr": jnp.pad(w_in_l[:, MAIN_COLS:], ((0, 0), (0, LANES - GLA_LOWRANK))).astype(BF16),
        "w_gate_up": jnp.pad(w_gate_up[l], ((0, LANES - GLA_LOWRANK), (0, 0))),
        "b_gate": row(b_gate[l]),
        "gla_norm": row(gla_norm[l]),
        "w_o_a": w_o[l][:ATTN_WIDTH].astype(BF16),
        "w_o_b": w_o[l][ATTN_WIDTH:].astype(BF16),
        "norm_ffn": row(norm_ffn[l]),
        "peer_query": peer_query_w[l].astype(BF16),
        "peer_keys_1": peer_keys_1[l],
        "peer_keys_2": peer_keys_2[l],
        "expert_u": expert_u[l].astype(BF16),
        "expert_v": expert_v[l].astype(BF16),
        "final_norm": row(final_norm_w),
    }


def kernel(x_prompt, x_sample, cache_attn_k, cache_attn_v, state_gla, norm_attn, w_in, w_gate_up,
           b_gate, gla_norm, w_o, norm_ffn, peer_query, peer_keys_1, peer_keys_2, expert_u,
           expert_v, final_norm):
    Bp, Lp, _ = x_prompt.shape
    Bs, Ls, _ = x_sample.shape
    w = _prepare(norm_attn, w_in, w_gate_up, b_gate, gla_norm, w_o, norm_ffn, peer_query, peer_keys_1,
                 peer_keys_2, expert_u, expert_v, final_norm)
    win_p = min(WINDOW_MAX, Lp)
    yp, kp, vp, sp = _prompt_group(x_prompt, w)
    ys, ks, vs, ss = _sample_group(x_sample, cache_attn_k[0], cache_attn_v[0], state_gla[0], w)
    heads = lambda a, B, L: a.reshape(1, B, L, ATTN_HEADS, HEAD_DIM)
    return (yp.reshape(Bp, Lp, D_MODEL), ys.reshape(Bs, Ls, D_MODEL),
            heads(kp, Bp, win_p), heads(vp, Bp, win_p), sp[None],
            heads(ks, Bs, Ls), heads(vs, Bs, Ls), ss[None])


def sample_group(inp):
    names = ("norm_attn", "w_in", "w_gate_up", "b_gate", "gla_norm", "w_o", "norm_ffn", "peer_query",
             "peer_keys_1", "peer_keys_2", "expert_u", "expert_v", "final_norm")
    w = _prepare(*[inp[n] for n in names])
    out, k, v, st = _sample_group(inp["x_sample"], inp["cache_attn_k"][0], inp["cache_attn_v"][0],
                                  inp["state_gla"][0], w)
    Bs = out.shape[0]
    return (out, k.reshape(Bs, 1, ATTN_HEADS, HEAD_DIM), v.reshape(Bs, 1, ATTN_HEADS, HEAD_DIM), st)
```

```python
import functools
import math

import jax
import jax.numpy as jnp
import numpy as np
from jax import lax
from jax.experimental import pallas as pl
from jax.experimental.pallas import tpu as pltpu

F32 = jnp.float32
BF16 = jnp.bfloat16

D_MODEL = 2048
PAST_LEN = 2048
HEAD_DIM = 128
ATTN_HEADS = 8
ATTN_WIDTH = ATTN_HEADS * HEAD_DIM
DILATIONS = ((128, 1), (512, 4), (2048, 16))
WINDOW_MAX = 2048
QUERY_BLOCK = 128
ATTN_SCALE = HEAD_DIM ** -0.5
ROPE_THETA = 500000.0
ROT_DIM = HEAD_DIM // 4
ROT_HALF = ROT_DIM // 2
GLA_HEADS = 4
GLA_WIDTH = D_MODEL - ATTN_WIDTH
GLA_DV = GLA_WIDTH // GLA_HEADS
GLA_DK = GLA_DV // 2
GLA_KW = GLA_HEADS * GLA_DK
GLA_LOWRANK = 16
GLA_TAU = 16.0
GLA_SUB = 16
PEER_HEADS = 8
PEER_NKEYS = 128
PEER_EXPERTS = PEER_NKEYS * PEER_NKEYS
PEER_QDIM = 256
PEER_HALF = PEER_QDIM // 2
PEER_TOPK = 16
NORM_EPS = 1e-6
MAIN_COLS = 3 * ATTN_WIDTH + 2 * GLA_KW + 2 * GLA_WIDTH

LANES = 128
SUBLANES = 8
VMEM_LIMIT = 56 * 1024 * 1024
NEG = -1e30
LOG2E = 1.0 / math.log(2.0)
HIGHEST = lax.Precision.HIGHEST
NT = (((1,), (1,)), ((), ()))
TN = (((0,), (0,)), ((), ()))
NN = (((1,), (0,)), ((), ()))


def _cparams(*sem):
    return pltpu.CompilerParams(dimension_semantics=sem, vmem_limit_bytes=VMEM_LIMIT)


def _rms(x, g):
    return x * lax.rsqrt(jnp.mean(x * x, axis=-1, keepdims=True) + NORM_EPS) * g


def _in_proj_body(x_ref, g_ref, w_ref, wlr_ref, wup_ref, bg_ref, cos_ref, sin_ref,
                  z_ref, la_ref, kwin_ref, vwin_ref, xn_ref, *, tn, nb, wb):
    i = pl.program_id(0)
    j = pl.program_id(1)
    in_win = i % nb >= nb - wb
    kv_tiles = ATTN_WIDTH // tn

    @pl.when(j == 0)
    def _():
        xn = _rms(x_ref[...], g_ref[...]).astype(BF16)
        xn_ref[...] = xn
        lr = jnp.dot(xn, wlr_ref[...], preferred_element_type=F32)
        zg = _dot_split(lr, wup_ref[...], NN) + bg_ref[...]
        la_ref[...] = (jnp.minimum(zg, 0.0) - jnp.log(1.0 + jnp.exp(-jnp.abs(zg)))) * (1.0 / GLA_TAU)

    acc = jnp.dot(xn_ref[...], w_ref[...], preferred_element_type=F32)

    @pl.when(j < 2 * ATTN_WIDTH // tn)
    def _():
        lane = lax.broadcasted_iota(jnp.int32, (acc.shape[0], HEAD_DIM), 1)
        cos = cos_ref[...]
        sin = sin_ref[...]
        for c in range(tn // HEAD_DIM):
            a = acc[:, c * HEAD_DIM:(c + 1) * HEAD_DIM]
            partner = jnp.where(lane < ROT_HALF, pltpu.roll(a, HEAD_DIM - ROT_HALF, 1),
                                pltpu.roll(a, ROT_HALF, 1))
            z_ref[:, c * HEAD_DIM:(c + 1) * HEAD_DIM] = a * cos + partner * sin

        @pl.when(jnp.logical_and(in_win, j >= kv_tiles))
        def _():
            kwin_ref[...] = z_ref[...]

    @pl.when(j >= 2 * ATTN_WIDTH // tn)
    def _():
        z_ref[...] = acc

        @pl.when(jnp.logical_and(in_win, j < 3 * kv_tiles))
        def _():
            vwin_ref[...] = acc


def _rotary_tables(pos):
    inv = jnp.exp(-math.log(ROPE_THETA) * jnp.arange(ROT_HALF, dtype=F32) * (2.0 / ROT_DIM))
    ang = pos.astype(F32)[:, None] * inv[None, :]
    n = pos.shape[0]
    cos = jnp.concatenate([jnp.cos(ang), jnp.cos(ang), jnp.ones((n, HEAD_DIM - ROT_DIM), F32)], axis=1)
    sin = jnp.concatenate([-jnp.sin(ang), jnp.sin(ang), jnp.zeros((n, HEAD_DIM - ROT_DIM), F32)], axis=1)
    return cos, sin


def in_proj(x, w, pos, tm, win, tn=512):
    T, D = x.shape
    P = pos.shape[0]
    assert T % tm == 0 and MAIN_COLS % tn == 0 and P % tm == 0 and ATTN_WIDTH % tn == 0 and win % tm == 0
    nb, wb = P // tm, win // tm
    kv_tiles = ATTN_WIDTH // tn
    cos, sin = _rotary_tables(pos)
    const = lambda i, j: (0, 0)

    def win_spec(first_tile):
        def index(i, j):
            ib = i % nb
            row = (i // nb) * wb + jnp.maximum(ib - (nb - wb), 0)
            col = jnp.where(ib >= nb - wb, jnp.clip(j - first_tile, 0, kv_tiles - 1), 0)
            return row, col
        return pl.BlockSpec((tm, tn), index)

    win_shape = jax.ShapeDtypeStruct((T // P * win, ATTN_WIDTH), F32)
    return pl.pallas_call(
        functools.partial(_in_proj_body, tn=tn, nb=nb, wb=wb),
        grid=(T // tm, MAIN_COLS // tn),
        in_specs=[
            pl.BlockSpec((tm, D), lambda i, j: (i, 0)),
            pl.BlockSpec((1, D), const),
            pl.BlockSpec((D, tn), lambda i, j: (0, j)),
            pl.BlockSpec((D, LANES), const),
            pl.BlockSpec((LANES, GLA_KW), const),
            pl.BlockSpec((1, GLA_KW), const),
            pl.BlockSpec((tm, HEAD_DIM), lambda i, j: (i % (P // tm), 0)),
            pl.BlockSpec((tm, HEAD_DIM), lambda i, j: (i % (P // tm), 0)),
        ],
        out_specs=[
            pl.BlockSpec((tm, tn), lambda i, j: (i, j)),
            pl.BlockSpec((tm, GLA_KW), lambda i, j: (i, 0)),
            win_spec(kv_tiles),
            win_spec(2 * kv_tiles),
        ],
        out_shape=[
            jax.ShapeDtypeStruct((T, MAIN_COLS), F32),
            jax.ShapeDtypeStruct((T, GLA_KW), F32),
            win_shape,
            win_shape,
        ],
        scratch_shapes=[pltpu.VMEM((tm, D), BF16)],
        compiler_params=_cparams("arbitrary", "arbitrary"),
        name="in_proj",
    )(x, w["norm_attn"], w["w_in"], w["w_in_lr"], w["w_gate_up"], w["b_gate"], cos, sin)


def _attn_prompt_body(q_ref, k_ref, v_ref, o_ref, ob_ref, lse_ref, s_ref, p_ref, m_ref, *, S):
    QB = QUERY_BLOCK
    row = lax.broadcasted_iota(jnp.int32, (QB, QB), 0)
    col = lax.broadcasted_iota(jnp.int32, (QB, QB), 1)
    bias = jnp.concatenate([jnp.where(col <= row, 0.0, NEG), jnp.where(col >= row, 0.0, NEG)], axis=1)
    lane2 = lax.broadcasted_iota(jnp.int32, (QB, 2 * QB), 1)
    ones = jnp.ones((QB, HEAD_DIM), BF16)

    for bi, (window, d) in enumerate(DILATIONS):
        assert window == d * QB
        nblk = S // d // QB
        assert d * nblk == s_ref.shape[0]

        def windows(idx, d=d, nblk=nblk):
            r = idx // nblk
            i = idx % nblk
            start = r + i * (d * QB)
            if d == 1:
                return i, pl.ds(pl.multiple_of(start, QB), QB)
            return i, pl.ds(start, QB, stride=d)

        def scores(idx, k_prev):
            i, rows = windows(idx)
            qs = (q_ref[rows, :] * ATTN_SCALE).astype(BF16)
            k_cur = k_ref[rows, :].astype(BF16)
            kk = jnp.concatenate([k_cur, k_prev], axis=0)
            s = lax.dot_general(qs, kk, NT, preferred_element_type=F32) + bias
            s_ref[idx] = jnp.where(jnp.logical_or(lane2 < QB, i > 0), s, NEG)
            return k_cur

        def softmax(idx, carry):
            s = s_ref[idx]
            m = jnp.max(jnp.maximum(s[:, :QB], s[:, QB:]), axis=1, keepdims=True)
            p_ref[idx] = jnp.exp(s - m).astype(BF16)
            m_ref[idx] = jnp.broadcast_to(m, (QB, HEAD_DIM))
            return carry

        def values(idx, v_prev, bi=bi):
            i, rows = windows(idx)
            v_cur = jnp.concatenate([v_ref[rows, :].astype(BF16), ones], axis=1)
            vv = jnp.concatenate([v_cur, v_prev], axis=0)
            o = jnp.dot(p_ref[idx], vv, preferred_element_type=F32)
            den = o[:, HEAD_DIM:]
            ob_ref[bi, rows, :] = o[:, :HEAD_DIM] / den
            lse_ref[bi, rows, :] = m_ref[idx] + jnp.log(den)
            return v_cur

        lax.fori_loop(0, d * nblk, scores, jnp.zeros((QB, HEAD_DIM), BF16), unroll=4)
        lax.fori_loop(0, d * nblk, softmax, 0, unroll=4)
        lax.fori_loop(0, d * nblk, values, jnp.zeros((QB, 2 * HEAD_DIM), BF16), unroll=4)

    CH = 256

    def combine(c, carry):
        rows = pl.ds(pl.multiple_of(c * CH, CH), CH)
        l0, l1, l2 = lse_ref[0, rows, :], lse_ref[1, rows, :], lse_ref[2, rows, :]
        mx = jnp.maximum(jnp.maximum(l0, l1), l2)
        w0, w1, w2 = jnp.exp(l0 - mx), jnp.exp(l1 - mx), jnp.exp(l2 - mx)
        num = w0 * ob_ref[0, rows, :] + w1 * ob_ref[1, rows, :] + w2 * ob_ref[2, rows, :]
        o_ref[rows, :] = (num / (w0 + w1 + w2)).astype(o_ref.dtype)
        return carry

    lax.fori_loop(0, S // CH, combine, 0)


def attn_prompt(z, B, S):
    H = ATTN_HEADS
    assert S % (DILATIONS[-1][1] * QUERY_BLOCK) == 0
    return pl.pallas_call(
        functools.partial(_attn_prompt_body, S=S),
        grid=(B, H),
        in_specs=[
            pl.BlockSpec((S, HEAD_DIM), lambda b, h: (b, h)),
            pl.BlockSpec((S, HEAD_DIM), lambda b, h: (b, H + h)),
            pl.BlockSpec((S, HEAD_DIM), lambda b, h: (b, 2 * H + h)),
        ],
        out_specs=pl.BlockSpec((S, HEAD_DIM), lambda b, h: (b, h)),
        out_shape=jax.ShapeDtypeStruct((B * S, ATTN_WIDTH), BF16),
        scratch_shapes=[
            pltpu.VMEM((len(DILATIONS), S, HEAD_DIM), F32),
            pltpu.VMEM((len(DILATIONS), S, HEAD_DIM), F32),
            pltpu.VMEM((S // QUERY_BLOCK, QUERY_BLOCK, 2 * QUERY_BLOCK), F32),
            pltpu.VMEM((S // QUERY_BLOCK, QUERY_BLOCK, 2 * QUERY_BLOCK), BF16),
            pltpu.VMEM((S // QUERY_BLOCK, QUERY_BLOCK, HEAD_DIM), F32),
        ],
        compiler_params=_cparams("parallel", "parallel"),
        name="attn_prompt",
    )(z, z, z)


def _attn_sample_body(z_ref, k1_ref, k4_ref, k16_ref, v1_ref, v4_ref, v16_ref, o_ref, *, bb):
    H = ATTN_HEADS
    for b in range(bb):
        q = z_ref[b, 0:H, :] * ATTN_SCALE
        k_new = z_ref[b, H:2 * H, :]
        v_new = z_ref[b, 2 * H:3 * H, :]
        s_new = jnp.sum(k_new * q, axis=-1, keepdims=True)
        s_win = [jnp.sum(kr[b] * q, axis=-1, keepdims=True)
                 for kr in (k1_ref, k4_ref, k16_ref)]
        m = s_new
        for s in s_win:
            m = jnp.maximum(m, jnp.max(s, axis=0))
        p_new = jnp.exp(s_new - m) * float(len(DILATIONS))
        den = p_new
        acc = p_new * v_new
        for s, vr in zip(s_win, (v1_ref, v4_ref, v16_ref)):
            p = jnp.exp(s - m)
            den = den + jnp.sum(p, axis=0)
            acc = acc + jnp.sum(p * vr[b], axis=0)
        o_ref[b] = acc / den


def attn_sample(z, cache_k, cache_v, bb=4):
    Bs = z.shape[0]
    wbuf = cache_k.shape[1]
    QB, H = QUERY_BLOCK, ATTN_HEADS
    assert wbuf == PAST_LEN == WINDOW_MAX and Bs % bb == 0

    def views(c):
        out, specs = [], []
        for window, d in DILATIONS:
            n = wbuf // d
            if d == 1:
                out.append(c)
                specs.append(pl.BlockSpec((bb, QB, H, HEAD_DIM), lambda i, n=n: (i, n // QB - 1, 0, 0)))
            else:
                out.append(c.reshape(Bs, n, d, H, HEAD_DIM))
                specs.append(pl.BlockSpec((bb, QB, None, H, HEAD_DIM),
                                          lambda i, n=n: (i, n // QB - 1, 0, 0, 0)))
        return out, specs

    kv, kspecs = views(cache_k)
    vv, vspecs = views(cache_v)
    z3 = z.reshape(Bs, MAIN_COLS // HEAD_DIM, HEAD_DIM)
    out = pl.pallas_call(
        functools.partial(_attn_sample_body, bb=bb),
        grid=(Bs // bb,),
        in_specs=[pl.BlockSpec((bb, MAIN_COLS // HEAD_DIM, HEAD_DIM), lambda i: (i, 0, 0))] + kspecs + vspecs,
        out_specs=pl.BlockSpec((bb, H, HEAD_DIM), lambda i: (i, 0, 0)),
        out_shape=jax.ShapeDtypeStruct((Bs, H, HEAD_DIM), F32),
        compiler_params=_cparams("parallel"),
        name="attn_sample",
    )(z3, *kv, *vv)
    return out.reshape(Bs, ATTN_WIDTH).astype(BF16)


def _gla_prompt_body(q_ref, k_ref, v_ref, go_ref, la_ref, gn_ref, o_ref, st_ref, sT_ref, b_ref, oi_ref,
                     *, B, C):
    c = pl.program_id(0)
    nsub = C // GLA_SUB

    @pl.when(c == 0)
    def _():
        sT_ref[...] = jnp.zeros_like(sT_ref)

    tri = (lax.broadcasted_iota(jnp.int32, (C, C), 1)
           <= lax.broadcasted_iota(jnp.int32, (C, C), 0)).astype(F32).astype(BF16)
    srow = lax.broadcasted_iota(jnp.int32, (GLA_SUB, 1), 0)
    pairs = [(n, h) for n in range(B) for h in range(GLA_HEADS)]

    for n, h in pairs:
        kl = pl.ds(h * GLA_DK, GLA_DK)
        vl = pl.ds(h * GLA_DV, GLA_DV)
        la = la_ref[n, :, kl]
        la_hi = la.astype(BF16)
        la_lo = (la - la_hi.astype(F32)).astype(BF16)
        b = (jnp.dot(tri, la_hi, preferred_element_type=F32)
             + jnp.dot(tri, la_lo, preferred_element_type=F32))
        b_ref[n, h] = b
        qh = q_ref[n, :, kl] * (GLA_DK ** -0.5)
        kh = k_ref[n, :, kl]
        vh = v_ref[n, :, vl]
        b_end = b[C - 1:C, :]
        sT = sT_ref[n, h]
        oi_ref[n, h] = lax.dot_general((qh * jnp.exp(b)).astype(BF16), sT.astype(BF16), NT,
                                       preferred_element_type=F32)
        k_end = (kh * jnp.exp(b_end - b)).astype(BF16)
        sT_ref[n, h] = sT * jnp.exp(b_end) + lax.dot_general(vh.astype(BF16), k_end, TN,
                                                             preferred_element_type=F32)
        for I in range(1, nsub):
            r0 = I * GLA_SUB
            cI = b[r0 - 1:r0, :]
            qI = (qh[r0:r0 + GLA_SUB] * jnp.exp(b[r0:r0 + GLA_SUB] - cI)).astype(BF16)
            kI = (kh[:r0] * jnp.exp(cI - b[:r0])).astype(BF16)
            a = lax.dot_general(qI, kI, NT, preferred_element_type=F32)
            oi_ref[n, h, r0:r0 + GLA_SUB, :] += jnp.dot(a.astype(BF16), vh[:r0].astype(BF16),
                                                        preferred_element_type=F32)

    def diag(I, carry):
        rows = pl.ds(pl.multiple_of(I * GLA_SUB, GLA_SUB), GLA_SUB)
        for n, h in pairs:
            bI = b_ref[n, h, rows, :]
            qI = q_ref[n, rows, pl.ds(h * GLA_DK, GLA_DK)] * (GLA_DK ** -0.5)
            kI = k_ref[n, rows, pl.ds(h * GLA_DK, GLA_DK)]
            vI = v_ref[n, rows, pl.ds(h * GLA_DV, GLA_DV)]
            acc = jnp.zeros((GLA_SUB, GLA_DV), F32)
            for s in range(GLA_SUB):
                e = jnp.exp(jnp.minimum(bI - bI[s:s + 1, :], 0.0))
                a_col = jnp.sum(qI * kI[s:s + 1, :] * e, axis=1, keepdims=True)
                a_col = jnp.where(srow >= s, a_col, 0.0)
                acc += a_col * vI[s:s + 1, :]
            oi_ref[n, h, rows, :] += acc
        return carry

    lax.fori_loop(0, nsub, diag, 0)

    for n, h in pairs:
        vl = pl.ds(h * GLA_DV, GLA_DV)
        g = go_ref[n, :, vl]
        o_ref[n, :, vl] = (_rms(oi_ref[n, h], gn_ref[...]) * (g * jax.nn.sigmoid(g))).astype(o_ref.dtype)

    @pl.when(c == pl.num_programs(0) - 1)
    def _():
        for n, h in pairs:
            st_ref[n, h] = sT_ref[n, h].T


def gla_prompt(z, log_a, gla_norm, B, S, C=128):
    assert S % C == 0 and C % GLA_SUB == 0
    q_blk = 3 * ATTN_WIDTH // GLA_KW
    v_blk = (3 * ATTN_WIDTH + 2 * GLA_KW) // GLA_WIDTH
    z3 = z.reshape(B, S, z.shape[1])
    la3 = log_a.reshape(B, S, GLA_KW)
    o, state = pl.pallas_call(
        functools.partial(_gla_prompt_body, B=B, C=C),
        grid=(S // C,),
        in_specs=[
            pl.BlockSpec((B, C, GLA_KW), lambda c: (0, c, q_blk)),
            pl.BlockSpec((B, C, GLA_KW), lambda c: (0, c, q_blk + 1)),
            pl.BlockSpec((B, C, GLA_WIDTH), lambda c: (0, c, v_blk)),
            pl.BlockSpec((B, C, GLA_WIDTH), lambda c: (0, c, v_blk + 1)),
            pl.BlockSpec((B, C, GLA_KW), lambda c: (0, c, 0)),
            pl.BlockSpec((1, GLA_DV), lambda c: (0, 0)),
        ],
        out_specs=[
            pl.BlockSpec((B, C, GLA_WIDTH), lambda c: (0, c, 0)),
            pl.BlockSpec((B, GLA_HEADS, GLA_DK, GLA_DV), lambda c: (0, 0, 0, 0)),
        ],
        out_shape=[
            jax.ShapeDtypeStruct((B, S, GLA_WIDTH), BF16),
            jax.ShapeDtypeStruct((B, GLA_HEADS, GLA_DK, GLA_DV), F32),
        ],
        scratch_shapes=[
            pltpu.VMEM((B, GLA_HEADS, GLA_DV, GLA_DK), F32),
            pltpu.VMEM((B, GLA_HEADS, C, GLA_DK), F32),
            pltpu.VMEM((B, GLA_HEADS, C, GLA_DV), F32),
        ],
        compiler_params=_cparams("arbitrary"),
        name="gla_prompt",
    )(z3, z3, z3, z3, la3, gla_norm)
    return o.reshape(B * S, GLA_WIDTH), state


def _gla_sample_body(q_ref, k_ref, v_ref, go_ref, la_ref, gn_ref, s_ref, o_ref, so_ref, *, bb):
    for h in range(GLA_HEADS):
        kl = pl.ds(h * GLA_DK, GLA_DK)
        vl = pl.ds(h * GLA_DV, GLA_DV)
        aT = jnp.exp(la_ref[:, kl]).T
        kT = k_ref[:, kl].T
        qT = (q_ref[:, kl] * (GLA_DK ** -0.5)).T
        outs = []
        for b in range(bb):
            s_new = aT[:, b:b + 1] * s_ref[b, h] + kT[:, b:b + 1] * v_ref[b:b + 1, vl]
            so_ref[b, h] = s_new
            outs.append(jnp.sum(qT[:, b:b + 1] * s_new, axis=0, keepdims=True))
        o = jnp.concatenate(outs, axis=0)
        g = go_ref[:, vl]
        o_ref[:, vl] = (_rms(o, gn_ref[...]) * (g * jax.nn.sigmoid(g))).astype(o_ref.dtype)


def gla_sample(z, log_a, gla_norm, state, bb=SUBLANES):
    Bs = z.shape[0]
    assert Bs % bb == 0
    q_blk = 3 * ATTN_WIDTH // GLA_KW
    v_blk = (3 * ATTN_WIDTH + 2 * GLA_KW) // GLA_WIDTH
    st_spec = pl.BlockSpec((bb, GLA_HEADS, GLA_DK, GLA_DV), lambda i: (i, 0, 0, 0))
    return pl.pallas_call(
        functools.partial(_gla_sample_body, bb=bb),
        grid=(Bs // bb,),
        in_specs=[
            pl.BlockSpec((bb, GLA_KW), lambda i: (i, q_blk)),
            pl.BlockSpec((bb, GLA_KW), lambda i: (i, q_blk + 1)),
            pl.BlockSpec((bb, GLA_WIDTH), lambda i: (i, v_blk)),
            pl.BlockSpec((bb, GLA_WIDTH), lambda i: (i, v_blk + 1)),
            pl.BlockSpec((bb, GLA_KW), lambda i: (i, 0)),
            pl.BlockSpec((1, GLA_DV), lambda i: (0, 0)),
            st_spec,
        ],
        out_specs=[pl.BlockSpec((bb, GLA_WIDTH), lambda i: (i, 0)), st_spec],
        out_shape=[
            jax.ShapeDtypeStruct((Bs, GLA_WIDTH), BF16),
            jax.ShapeDtypeStruct(state.shape, F32),
        ],
        compiler_params=_cparams("parallel"),
        name="gla_sample",
    )(z, z, z, z, log_a, gla_norm, state)


def _proj_residual_body(a_ref, b_ref, wa_ref, wb_ref, res_ref, o_ref):
    acc = jnp.dot(a_ref[...], wa_ref[...], preferred_element_type=F32)
    acc += jnp.dot(b_ref[...], wb_ref[...], preferred_element_type=F32)
    o_ref[...] = res_ref[...] + acc


def proj_residual(a, b, wa, wb, res, tm, tn=1024):
    T, Ka = a.shape
    Kb = b.shape[1]
    N = wa.shape[1]
    assert T % tm == 0 and N % tn == 0
    return pl.pallas_call(
        _proj_residual_body,
        grid=(T // tm, N // tn),
        in_specs=[
            pl.BlockSpec((tm, Ka), lambda i, j: (i, 0)),
            pl.BlockSpec((tm, Kb), lambda i, j: (i, 0)),
            pl.BlockSpec((Ka, tn), lambda i, j: (0, j)),
            pl.BlockSpec((Kb, tn), lambda i, j: (0, j)),
            pl.BlockSpec((tm, tn), lambda i, j: (i, j)),
        ],
        out_specs=pl.BlockSpec((tm, tn), lambda i, j: (i, j)),
        out_shape=jax.ShapeDtypeStruct((T, N), F32),
        compiler_params=_cparams("parallel", "arbitrary"),
        name="proj_residual",
    )(a, b, wa, wb, res)


def _dot_split(a, b, dims):
    a_hi = a.astype(BF16)
    b_hi = b.astype(BF16)
    a_lo = (a - a_hi.astype(F32)).astype(BF16)
    b_lo = (b - b_hi.astype(F32)).astype(BF16)
    dot = lambda x, y: lax.dot_general(x, y, dims, preferred_element_type=F32)
    return dot(a_hi, b_hi) + (dot(a_hi, b_lo) + dot(a_lo, b_hi))


def _sorting_network(n):
    assert n & (n - 1) == 0
    pairs = []
    p = 1
    while p < n:
        k = p
        while k >= 1:
            for j in range(k % p, n - k, 2 * k):
                for i in range(min(k, n - j - k)):
                    if (i + j) // (2 * p) == (i + j + k) // (2 * p):
                        pairs.append((i + j, i + j + k))
            k //= 2
        p *= 2
    return pairs


def _peer_route_body(h_ref, g_ref, w_ref, k1_ref, k2_ref, xn_ref, s1_ref, s2_ref, thr_ref, q_ref, *, tm):
    xn = _rms(h_ref[...], g_ref[...]).astype(BF16)
    xn_ref[...] = xn
    q_ref[...] = jnp.dot(xn, w_ref[...], preferred_element_type=F32)

    K = PEER_TOPK
    G = -(-(K + 1) // SUBLANES)
    row8 = lax.broadcasted_iota(jnp.int32, (SUBLANES, tm), 0)
    rowh = lax.broadcasted_iota(jnp.int32, (PEER_HEADS, tm), 0)
    thr_all = jnp.zeros((PEER_HEADS, tm), F32)
    neg_inf = jnp.full((SUBLANES, tm), -jnp.inf, F32)

    def sort_groups(groups):
        n = 1 << (len(groups) - 1).bit_length()
        lists = list(groups) + [None] * (n - len(groups))
        for a, b in _sorting_network(n):
            if lists[a] is None:
                lists[a], lists[b] = lists[b], None
            elif lists[b] is not None:
                lists[a], lists[b] = jnp.maximum(lists[a], lists[b]), jnp.minimum(lists[a], lists[b])
        return lists

    def pop(lists, depth):
        mx = jnp.max(lists[0], axis=0, keepdims=True)
        hit = lists[0] == mx
        for r in range(min(len(lists), depth)):
            if lists[r] is None:
                break
            below = lists[r + 1] if r + 1 < len(lists) and lists[r + 1] is not None else neg_inf
            lists[r] = jnp.where(hit, below, lists[r])
        return mx

    def top_values(s):
        lists = sort_groups([s[r * SUBLANES:(r + 1) * SUBLANES] for r in range(PEER_NKEYS // SUBLANES)])
        vals = [neg_inf] * G
        for it in range(K + 1):
            mx = pop(lists, K - it)
            vals[it // SUBLANES] = jnp.where(row8 == it % SUBLANES, mx, vals[it // SUBLANES])
        return vals

    for h in range(PEER_HEADS):
        q1 = q_ref[:, h * PEER_QDIM:h * PEER_QDIM + PEER_HALF]
        q2 = q_ref[:, h * PEER_QDIM + PEER_HALF:(h + 1) * PEER_QDIM]
        s1 = _dot_split(k1_ref[h], q1, NT)
        s2 = _dot_split(k2_ref[h], q2, NT)
        v1 = top_values(s1)
        v2 = top_values(s2)
        top1, top2 = v1[0][0:1], v2[0][0:1]
        cands = [top1 + g for g in v2]
        for a in range(1, SUBLANES):
            cands.append(jnp.where(row8 < (K + 1) // (a + 1), v1[0][a:a + 1] + v2[0], -jnp.inf))
        cands += [g + top2 for g in v1[1:]]
        lists = sort_groups(cands)
        m = top1 + top2
        z = jnp.zeros((1, tm), F32)
        kth = m
        for it in range(K):
            kth = pop(lists, K - it)
            z = z + jnp.exp(kth - m)
        nxt = jnp.max(lists[0], axis=0, keepdims=True)
        c = m + jnp.log(z)
        s1_ref[h] = (s1 - c) * LOG2E - 1.0
        s2_ref[h] = s2 * LOG2E
        thr_all = jnp.where(rowh == h, (0.5 * (kth + nxt) - c) * LOG2E - 1.0, thr_all)
    thr_ref[...] = thr_all


def peer_route(h, g, w, keys1, keys2, tm):
    T, D = h.shape
    N = w.shape[1]
    assert T % tm == 0 and tm % LANES == 0 and (PEER_TOPK + 1) // 2 <= SUBLANES and N == PEER_HEADS * PEER_QDIM
    tile = pl.BlockSpec((PEER_HEADS, PEER_NKEYS, tm), lambda i: (0, 0, i))
    tile_shape = jax.ShapeDtypeStruct((PEER_HEADS, PEER_NKEYS, T), F32)
    kspec = pl.BlockSpec((PEER_HEADS, PEER_NKEYS, PEER_HALF), lambda i: (0, 0, 0))
    return pl.pallas_call(
        functools.partial(_peer_route_body, tm=tm),
        grid=(T // tm,),
        in_specs=[
            pl.BlockSpec((tm, D), lambda i: (i, 0)),
            pl.BlockSpec((1, D), lambda i: (0, 0)),
            pl.BlockSpec((D, N), lambda i: (0, 0)),
            kspec, kspec,
        ],
        out_specs=[pl.BlockSpec((tm, D), lambda i: (i, 0)), tile, tile,
                   pl.BlockSpec((PEER_HEADS, tm), lambda i: (0, i))],
        out_shape=[jax.ShapeDtypeStruct((T, D), BF16), tile_shape, tile_shape,
                   jax.ShapeDtypeStruct((PEER_HEADS, T), F32)],
        scratch_shapes=[pltpu.VMEM((tm, N), F32)],
        compiler_params=_cparams("parallel"),
        name="peer_route",
    )(h, g, w, keys1, keys2)


def _peer_dense_body(xn_ref, u_ref, v_ref, s1_ref, s2_ref, thr_ref, h_ref, g_ref, y_ref, *, te, tm):
    e = pl.program_id(1)

    @pl.when(e == 0)
    def _():
        y_ref[...] = h_ref[...]

    rows = []
    for c in range(te // PEER_NKEYS):
        tiles = []
        for tc in range(tm // LANES):
            lanes = pl.ds(tc * LANES, LANES)
            gate = jnp.zeros((PEER_NKEYS, LANES), F32)
            for h in range(PEER_HEADS):
                ssum = s2_ref[h, :, lanes] + s1_ref[h, c:c + 1, lanes]
                gate += jnp.where(ssum >= thr_ref[pl.ds(h, 1), lanes], jnp.exp2(ssum), 0.0)
            tiles.append(gate)
        rows.append(jnp.concatenate(tiles, axis=1))
    half_gate = jnp.concatenate(rows, axis=0).T
    hx = lax.dot_general(xn_ref[...], u_ref[...], NT, preferred_element_type=F32)
    act = (hx + hx * lax.erf(hx * (2.0 ** -0.5))) * half_gate
    y_ref[...] += jnp.dot(act.astype(BF16), v_ref[...], preferred_element_type=F32)

    @pl.when(e == pl.num_programs(1) - 1)
    def _():
        y_ref[...] = _rms(y_ref[...], g_ref[...])


def peer_dense(xn, u, v, s1, s2, thr, h, g, tm):
    T, D = xn.shape
    E = u.shape[0]
    te = SUBLANES * PEER_NKEYS
    assert T % tm == 0 and E % te == 0 and tm % LANES == 0
    return pl.pallas_call(
        functools.partial(_peer_dense_body, te=te, tm=tm),
        grid=(T // tm, E // te),
        in_specs=[
            pl.BlockSpec((tm, D), lambda i, e: (i, 0)),
            pl.BlockSpec((te, D), lambda i, e: (e, 0)),
            pl.BlockSpec((te, D), lambda i, e: (e, 0)),
            pl.BlockSpec((PEER_HEADS, SUBLANES, tm), lambda i, e: (0, e, i)),
            pl.BlockSpec((PEER_HEADS, PEER_NKEYS, tm), lambda i, e: (0, 0, i)),
            pl.BlockSpec((PEER_HEADS, tm), lambda i, e: (0, i)),
            pl.BlockSpec((tm, D), lambda i, e: (i, 0)),
            pl.BlockSpec((1, D), lambda i, e: (0, 0)),
        ],
        out_specs=pl.BlockSpec((tm, D), lambda i, e: (i, 0)),
        out_shape=jax.ShapeDtypeStruct((T, D), F32),
        compiler_params=_cparams("parallel", "arbitrary"),
        name="peer_dense",
    )(xn, u, v, s1, s2, thr, h, g)


def _post_mix(x, o_attn, o_gla, w, tm, tm_proj):
    h = proj_residual(o_attn, o_gla, w["w_o_a"], w["w_o_b"], x, tm_proj)
    xn, s1, s2, thr = peer_route(h, w["norm_ffn"], w["peer_query"], w["peer_keys_1"], w["peer_keys_2"], tm)
    return peer_dense(xn, w["expert_u"], w["expert_v"], s1, s2, thr, h, w["final_norm"], tm)


def _prompt_group(x_prompt, w):
    B, S, _ = x_prompt.shape
    x = x_prompt.reshape(B * S, D_MODEL)
    tm, tm_proj = 512, 1024
    z, log_a, k_win, v_win = in_proj(x, w, jnp.arange(S), tm_proj, min(WINDOW_MAX, S))
    o_attn = attn_prompt(z, B, S)
    o_gla, state = gla_prompt(z, log_a, w["gla_norm"], B, S)
    out = _post_mix(x, o_attn, o_gla, w, tm, tm_proj)
    return out, k_win, v_win, state


def _sample_group(x_sample, cache_k, cache_v, state, w):
    Bs, Ls, _ = x_sample.shape
    assert Ls == 1
    x = x_sample.reshape(Bs, D_MODEL)
    tm = Bs
    z, log_a, k_new, v_new = in_proj(x, w, jnp.full((tm,), PAST_LEN, jnp.int32), tm, tm)
    o_attn = attn_sample(z, cache_k, cache_v)
    o_gla, new_state = gla_sample(z, log_a, w["gla_norm"], state)
    out = _post_mix(x, o_attn, o_gla, w, tm, tm)
    return out, k_new, v_new, new_state


def _prepare(norm_attn, w_in, w_gate_up, b_gate, gla_norm, w_o, norm_ffn, peer_query_w, peer_keys_1,
             peer_keys_2, expert_u, expert_v, final_norm_w):
    assert w_in.shape[0] == 1
    l = 0
    w_in_l = w_in[l]
    row = lambda v: v.reshape(1, -1).astype(F32)
    return {
        "norm_attn": row(norm_attn[l]),
        "w_in": w_in_l.astype(BF16),
        "w_in_lr": jnp.pad(w_in_l[:, MAIN_COLS:], ((0, 0), (0, LANES - GLA_LOWRANK))).astype(BF16),
        "w_gate_up": jnp.pad(w_gate_up[l], ((0, LANES - GLA_LOWRANK), (0, 0))),
        "b_gate": row(b_gate[l]),
        "gla_norm": row(gla_norm[l]),
        "w_o_a": w_o[l][:ATTN_WIDTH].astype(BF16),
        "w_o_b": w_o[l][ATTN_WIDTH:].astype(BF16),
        "norm_ffn": row(norm_ffn[l]),
        "peer_query": peer_query_w[l].astype(BF16),
        "peer_keys_1": peer_keys_1[l],
        "peer_keys_2": peer_keys_2[l],
        "expert_u": expert_u[l].astype(BF16),
        "expert_v": expert_v[l].astype(BF16),
        "final_norm": row(final_norm_w),
    }


def kernel(x_prompt, x_sample, cache_attn_k, cache_attn_v, state_gla, norm_attn, w_in, w_gate_up,
           b_gate, gla_norm, w_o, norm_ffn, peer_query, peer_keys_1, peer_keys_2, expert_u,
           expert_v, final_norm):
    Bp, Lp, _ = x_prompt.shape
    Bs, Ls, _ = x_sample.shape
    w = _prepare(norm_attn, w_in, w_gate_up, b_gate, gla_norm, w_o, norm_ffn, peer_query, peer_keys_1,
                 peer_keys_2, expert_u, expert_v, final_norm)
    win_p = min(WINDOW_MAX, Lp)
    yp, kp, vp, sp = _prompt_group(x_prompt, w)
    ys, ks, vs, ss = _sample_group(x_sample, cache_attn_k[0], cache_attn_v[0], state_gla[0], w)
    heads = lambda a, B, L: a.reshape(1, B, L, ATTN_HEADS, HEAD_DIM)
    return (yp.reshape(Bp, Lp, D_MODEL), ys.reshape(Bs, Ls, D_MODEL),
            heads(kp, Bp, win_p), heads(vp, Bp, win_p), sp[None],
            heads(ks, Bs, Ls), heads(vs, Bs, Ls), ss[None])


def sample_group(inp):
    names = ("norm_attn", "w_in", "w_gate_up", "b_gate", "gla_norm", "w_o", "norm_ffn", "peer_query",
             "peer_keys_1", "peer_keys_2", "expert_u", "expert_v", "final_norm")
    w = _prepare(*[inp[n] for n in names])
    out, k, v, st = _sample_group(inp["x_sample"], inp["cache_attn_k"][0], inp["cache_attn_v"][0],
                                  inp["state_gla"][0], w)
    Bs = out.shape[0]
    return (out, k.reshape(Bs, 1, ATTN_HEADS, HEAD_DIM), v.reshape(Bs, 1, ATTN_HEADS, HEAD_DIM), st)
```
